```python
import math
import jax, jax.numpy as jnp
from jax import lax
import numpy as np

D_MODEL = 1024
BATCH = 16
SEQ = 256
DEPTH = 4
DEC_BATCH = 8
DEC_SEQ = 1024
PAST_LEN = 512

GRID_W = 64
D_BRANCH = D_MODEL // 2
S5_GROUP = 16
S5_GROUPS = D_BRANCH // S5_GROUP
S5_STATE = 64
N_DIR = 2
DH = 64
N_HEADS = D_BRANCH // (2 * DH)
DV = 2 * DH
POOL_WINDOWS = (2, 4, 8, 16)
POOL_GROUP = D_BRANCH // len(POOL_WINDOWS)
D_FF = ((8 * D_MODEL // 3 + 255) // 256) * 256
N_MOD = 9
N_BRANCH = 3
IN_W = 5 * D_BRANCH + N_BRANCH * D_MODEL
SPLITS = [D_BRANCH, 2 * D_BRANCH, 3 * D_BRANCH, 4 * D_BRANCH, 5 * D_BRANCH]
ROPE_BASE = 10000.0
EPS = 1e-6
Q_BLOCK = 128
DENSE_MAX_KEYS = 2048

kernel_name = 'hybrid_s5_diffattn_pool_diffusion_step'


def rmsnorm(x, g):
    xf = x.astype(jnp.float32)
    y = xf * lax.rsqrt(jnp.mean(xf * xf, axis=-1, keepdims=True) + EPS)
    return (y * g.astype(jnp.float32)).astype(x.dtype)


def swiglu(h, w_in, w_out):
    a, b = jnp.split(h @ w_in, 2, axis=-1)
    return (jax.nn.silu(a) * b) @ w_out


def adaln(cvec, w_mod, b_mod):
    m = jax.nn.silu(cvec) @ w_mod + b_mod
    return m.reshape(cvec.shape[0], 1, N_MOD, D_MODEL)


def axial_rope_tables(L):
    rows = L // GRID_W
    row = jnp.repeat(jnp.arange(rows, dtype=jnp.float32), GRID_W)
    col = jnp.tile(jnp.arange(GRID_W, dtype=jnp.float32), rows)
    n_freq = DH // 4
    inv = ROPE_BASE ** (-jnp.arange(n_freq, dtype=jnp.float32) / n_freq)
    ar = row[:, None] * inv
    ac = col[:, None] * inv
    return jnp.cos(ar), jnp.sin(ar), jnp.cos(ac), jnp.sin(ac)


def apply_rope(x, tabs):
    cr, sr, cc, sc = [t[None, :, None, None, :].astype(x.dtype) for t in tabs]

    def rot(y, cs, sn):
        y1, y2 = jnp.split(y, 2, axis=-1)
        return jnp.concatenate([y1 * cs - y2 * sn, y2 * cs + y1 * sn], axis=-1)

    xr, xc = jnp.split(x, 2, axis=-1)
    return jnp.concatenate([rot(xr, cr, sr), rot(xc, cc, sc)], axis=-1)


def cmul(ar, ai, br, bi):
    return ar * br - ai * bi, ar * bi + ai * br


def scan_combine(e1, e2):
    a1r, a1i, b1r, b1i = e1
    a2r, a2i, b2r, b2i = e2
    ar, ai = cmul(a2r, a2i, a1r, a1i)
    br, bi = cmul(a2r, a2i, b1r, b1i)
    return ar, ai, br + b2r, bi + b2i


def s5_mixer(u, lam_re, lam_im, log_dt, b_re, b_im, c_re, c_im, d_skip, w_glu, h0):
    f32 = jnp.float32
    bsz, L, _ = u.shape
    uf = u.astype(f32).reshape(bsz, L, S5_GROUPS, S5_GROUP)
    y = uf * d_skip.astype(f32).reshape(S5_GROUPS, S5_GROUP)
    finals = []
    for d in range(N_DIR):
        lr = lam_re[d].astype(f32)
        li = lam_im[d].astype(f32)
        dt = jnp.exp(log_dt[d].astype(f32))[:, None]
        mag = jnp.exp(lr * dt)
        abr = mag * jnp.cos(li * dt)
        abi = mag * jnp.sin(li * dt)
        den = lr * lr + li * li
        nr = abr - 1.0
        kr = (nr * lr + abi * li) / den
        ki = (abi * lr - nr * li) / den
        br = b_re[d].astype(f32)
        bi = b_im[d].astype(f32)
        bbr = kr[..., None] * br - ki[..., None] * bi
        bbi = kr[..., None] * bi + ki[..., None] * br
        xr = jnp.einsum('blgc,gpc->blgp', uf, bbr)
        xi = jnp.einsum('blgc,gpc->blgp', uf, bbi)
        ar = jnp.broadcast_to(abr, xr.shape)
        ai = jnp.broadcast_to(abi, xi.shape)
        cum_r, cum_i, hr, hi = lax.associative_scan(scan_combine, (ar, ai, xr, xi), axis=1, reverse=(d == 1))
        if h0 is None:
            end = L - 1 if d == 0 else 0
            finals.append(jnp.stack([hr[:, end], hi[:, end]], axis=1))
        else:
            h0r = h0[:, d, 0].astype(f32)[:, None]
            h0i = h0[:, d, 1].astype(f32)[:, None]
            hr = hr + cum_r * h0r - cum_i * h0i
            hi = hi + cum_r * h0i + cum_i * h0r
        y = y + jnp.einsum('gcp,blgp->blgc', c_re[d].astype(f32), hr) - jnp.einsum('gcp,blgp->blgc', c_im[d].astype(f32), hi)
    y = jax.nn.gelu(y.reshape(bsz, L, D_BRANCH)).astype(u.dtype)
    y = y * jax.nn.sigmoid(y @ w_glu)
    final = jnp.stack(finals, axis=1).astype(u.dtype) if h0 is None else None
    return y, final


def diff_attention(q, keys, vals, lam, lam_init, g):
    bsz, Lq = q.shape[0], q.shape[1]
    scale = DH ** -0.5

    def attend(qb):
        s = jnp.einsum('bqhmd,bkhmd->bhmqk', qb, keys).astype(jnp.float32) * scale
        p = jax.nn.softmax(s, axis=-1)
        a = (p[:, :, 0] - lam * p[:, :, 1]).astype(vals.dtype)
        return jnp.einsum('bhqk,bkhe->bqhe', a, vals)

    if keys.shape[1] >= DENSE_MAX_KEYS:
        nb = Lq // Q_BLOCK
        qb = jnp.moveaxis(q.reshape(bsz, nb, Q_BLOCK, N_HEADS, 2, DH), 1, 0)
        o = lax.map(attend, qb)
        o = jnp.moveaxis(o, 0, 1).reshape(bsz, Lq, N_HEADS, DV)
    else:
        o = attend(q)
    o = rmsnorm(o, g) * (1.0 - lam_init)
    return o.reshape(bsz, Lq, N_HEADS * DV)


def pool_mixer(z, w_pool, scale):
    bsz, L, _ = z.shape
    zf = z.astype(jnp.float32).reshape(bsz, L, len(POOL_WINDOWS), POOL_GROUP)
    cs = jnp.concatenate([jnp.zeros_like(zf[:, :1]), jnp.cumsum(zf, axis=1)], axis=1)
    t = jnp.arange(L)
    outs = []
    for gi, w in enumerate(POOL_WINDOWS):
        lo = jnp.clip(t - w // 2, 0, L)
        hi = jnp.clip(t - w // 2 + w, 0, L)
        csg = cs[:, :, gi]
        mean = (csg[:, hi] - csg[:, lo]) / (hi - lo).astype(jnp.float32)[None, :, None]
        outs.append(mean - zf[:, :, gi])
    pooled = jnp.stack(outs, axis=2).astype(z.dtype)
    y = jnp.einsum('blgc,gcd->blgd', pooled, w_pool).reshape(bsz, L, D_BRANCH)
    return y * scale


def token_mixer(h, lp, layer_idx, ctx):
    bsz, L, _ = h.shape
    u, q, k, v, z, g = jnp.split(h @ lp['w_in'], SPLITS, axis=-1)
    q = q.reshape(bsz, L, N_HEADS, 2, DH)
    k = k.reshape(bsz, L, N_HEADS, 2, DH)
    v = v.reshape(bsz, L, N_HEADS, DV)
    f32 = jnp.float32
    lam_init = 0.8 - 0.6 * math.exp(-0.3 * layer_idx)
    lam = (jnp.exp(jnp.sum(lp['lam_q1'].astype(f32) * lp['lam_k1'].astype(f32)))
           - jnp.exp(jnp.sum(lp['lam_q2'].astype(f32) * lp['lam_k2'].astype(f32))) + lam_init)
    s5_args = (lp['lam_re'], lp['lam_im'], lp['log_dt'], lp['b_re'], lp['b_im'], lp['c_re'], lp['c_im'], lp['d_skip'], lp['w_glu'])
    if ctx is None:
        ya, s_fin = s5_mixer(u, *s5_args, None)
        yb = diff_attention(q, k, v, lam, lam_init, lp['attn_norm_g'])
        new = (k, v, s_fin)
    else:
        k_ctx, v_ctx, s_ctx = ctx
        tabs = axial_rope_tables(L)
        q = apply_rope(q, tabs)
        k = apply_rope(k, tabs)
        ya, _ = s5_mixer(u, *s5_args, s_ctx)
        yb = diff_attention(q, jnp.concatenate([k, k_ctx], axis=1), jnp.concatenate([v, v_ctx], axis=1),
                            lam, lam_init, lp['attn_norm_g'])
        new = None
    yc = pool_mixer(z, lp['w_pool'], lp['pool_scale'])
    ys = jnp.stack([ya, yb, yc], axis=2)
    gates = jax.nn.sigmoid(g.reshape(bsz, L, N_BRANCH, D_MODEL))
    merged = jnp.sum(gates * jnp.einsum('blnc,ncd->blnd', ys, lp['w_branch']), axis=2)
    return merged @ lp['w_out'], new


def trunk_layer(x, mod, lp, layer_idx, ctx):
    ng = lp['norm_g']
    n = rmsnorm(x, ng[0]) * (1.0 + mod[..., 1, :]) + mod[..., 0, :]
    x = x + 0.5 * mod[..., 2, :] * swiglu(n, lp['w_ffn_in'][0], lp['w_ffn_out'][0])
    n = rmsnorm(x, ng[1]) * (1.0 + mod[..., 4, :]) + mod[..., 3, :]
    y, new = token_mixer(n, lp, layer_idx, ctx)
    x = x + mod[..., 5, :] * y
    n = rmsnorm(x, ng[2]) * (1.0 + mod[..., 7, :]) + mod[..., 6, :]
    x = x + 0.5 * mod[..., 8, :] * swiglu(n, lp['w_ffn_in'][1], lp['w_ffn_out'][1])
    return x, new


def setup_inputs(seed: int = 0) -> dict:
    key = jax.random.key(seed)
    keys = list(jax.random.split(key, 40))
    f32 = jnp.float32

    def nrm(shape, s):
        return jax.random.normal(keys.pop(), shape, f32) * s

    G, P = S5_GROUPS, S5_STATE
    return {
        'x_prompt': nrm((BATCH, SEQ, D_MODEL), 1.0),
        'x_sample': nrm((DEC_BATCH, DEC_SEQ, D_MODEL), 1.0),
        'cache_k': nrm((DEC_BATCH, DEPTH, PAST_LEN, N_HEADS, 2, DH), 1.0),
        'cache_v': nrm((DEC_BATCH, DEPTH, PAST_LEN, N_HEADS, DV), 1.0),
        'state_ssm': nrm((DEC_BATCH, DEPTH, N_DIR, 2, G, P), 0.3),
        'c': nrm((DEC_BATCH, D_MODEL), 1.0),
        'c_ctx': nrm((D_MODEL,), 1.0),
        'norm_g': 1.0 + nrm((DEPTH, 3, D_MODEL), 0.02),
        'w_mod': nrm((DEPTH, D_MODEL, N_MOD * D_MODEL), 0.3 * D_MODEL ** -0.5),
        'b_mod': nrm((DEPTH, N_MOD * D_MODEL), 0.02),
        'w_ffn_in': nrm((DEPTH, 2, D_MODEL, 2 * D_FF), D_MODEL ** -0.5),
        'w_ffn_out': nrm((DEPTH, 2, D_FF, D_MODEL), D_FF ** -0.5),
        'w_in': nrm((DEPTH, D_MODEL, IN_W), D_MODEL ** -0.5),
        'ssm_lam_re': -0.5 + nrm((DEPTH, N_DIR, G, P), 0.01),
        'ssm_lam_im': jnp.pi * jnp.arange(P, dtype=f32) + nrm((DEPTH, N_DIR, G, P), 0.01),
        'ssm_log_dt': jax.random.uniform(keys.pop(), (DEPTH, N_DIR, G), f32, math.log(1e-3), math.log(1e-1)),
        'ssm_b_re': nrm((DEPTH, N_DIR, G, P, S5_GROUP), (2 * S5_GROUP) ** -0.5),
        'ssm_b_im': nrm((DEPTH, N_DIR, G, P, S5_GROUP), (2 * S5_GROUP) ** -0.5),
        'ssm_c_re': nrm((DEPTH, N_DIR, G, S5_GROUP, P), 0.5),
        'ssm_c_im': nrm((DEPTH, N_DIR, G, S5_GROUP, P), 0.5),
        'ssm_d': nrm((DEPTH, D_BRANCH), 0.5),
        'w_glu': nrm((DEPTH, D_BRANCH, D_BRANCH), D_BRANCH ** -0.5),
        'lam_q1': nrm((DEPTH, DH), 0.1),
        'lam_k1': nrm((DEPTH, DH), 0.1),
        'lam_q2': nrm((DEPTH, DH), 0.1),
        'lam_k2': nrm((DEPTH, DH), 0.1),
        'attn_norm_g': 1.0 + nrm((DEPTH, DV), 0.02),
        'w_pool': nrm((DEPTH, len(POOL_WINDOWS), POOL_GROUP, POOL_GROUP), POOL_GROUP ** -0.5),
        'pool_scale': 1.0 + nrm((DEPTH, D_BRANCH), 0.02),
        'w_branch': nrm((DEPTH, N_BRANCH, D_BRANCH, D_MODEL), D_BRANCH ** -0.5),
        'w_out': nrm((DEPTH, D_MODEL, D_MODEL), D_MODEL ** -0.5),
        'final_norm_g': 1.0 + nrm((D_MODEL,), 0.02),
    }


def reference(x_prompt, x_sample, cache_k, cache_v, state_ssm, c, c_ctx, norm_g, w_mod, b_mod, w_ffn_in, w_ffn_out,
              w_in, ssm_lam_re, ssm_lam_im, ssm_log_dt, ssm_b_re, ssm_b_im, ssm_c_re, ssm_c_im, ssm_d, w_glu,
              lam_q1, lam_k1, lam_q2, lam_k2, attn_norm_g, w_pool, pool_scale, w_branch, w_out, final_norm_g):
    xp = x_prompt
    xs = x_sample
    new_k, new_v, new_s = [], [], []
    for l in range(DEPTH):
        lp = dict(norm_g=norm_g[l], w_ffn_in=w_ffn_in[l], w_ffn_out=w_ffn_out[l], w_in=w_in[l],
                  lam_re=ssm_lam_re[l], lam_im=ssm_lam_im[l], log_dt=ssm_log_dt[l],
                  b_re=ssm_b_re[l], b_im=ssm_b_im[l], c_re=ssm_c_re[l], c_im=ssm_c_im[l],
                  d_skip=ssm_d[l], w_glu=w_glu[l],
                  lam_q1=lam_q1[l], lam_k1=lam_k1[l], lam_q2=lam_q2[l], lam_k2=lam_k2[l],
                  attn_norm_g=attn_norm_g[l], w_pool=w_pool[l], pool_scale=pool_scale[l],
                  w_branch=w_branch[l], w_out=w_out[l])
        mod_ctx = adaln(c_ctx[None, :], w_mod[l], b_mod[l])
        mod_lat = adaln(c, w_mod[l], b_mod[l])
        xp, (k_l, v_l, s_l) = trunk_layer(xp, mod_ctx, lp, l, None)
        xs, _ = trunk_layer(xs, mod_lat, lp, l, (cache_k[:, l], cache_v[:, l], state_ssm[:, l]))
        new_k.append(k_l)
        new_v.append(v_l)
        new_s.append(s_l)
    y_prompt = rmsnorm(xp, final_norm_g)
    y_sample = rmsnorm(xs, final_norm_g)
    new_cache_k = jnp.stack(new_k, axis=1)
    new_cache_v = jnp.stack(new_v, axis=1)
    new_state_ssm = jnp.stack(new_s, axis=1)
    return (y_prompt, y_sample, new_cache_k, new_cache_v, new_state_ssm)
```

```python
import functools
import math

import jax
import jax.numpy as jnp
from jax import lax
from jax.experimental import pallas as pl
from jax.experimental.pallas import tpu as pltpu

F32 = jnp.float32
BF16 = jnp.bfloat16

D_MODEL = 1024
DEPTH = 4
GRID_W = 64
D_BRANCH = 512
S5_GROUP = 16
S5_GROUPS = 32
S5_STATE = 64
N_STATE = S5_GROUPS * S5_STATE
DH = 64
N_HEADS = 4
DV = 128
POOL_WINDOWS = (2, 4, 8, 16)
POOL_GROUP = 128
POOL_HALO = 8
D_FF = 2816
N_MOD = 9
N_BRANCH = 3
IN_W = 5 * D_BRANCH + N_BRANCH * D_MODEL
ROPE_BASE = 10000.0
EPS = 1e-6

LANES = 128
SUBLANES = 8
N_SLAB = D_BRANCH // LANES
SLAB_STATE = N_STATE // N_SLAB
ROW_TILE = 1024
MERGE_ROW_TILE = 512
FF_TILE = 256
S5_CHUNK_ROWS = 512
VMEM_LIMIT = 56 * 1024 * 1024


def _cparams(sem):
    return pltpu.CompilerParams(dimension_semantics=sem, vmem_limit_bytes=VMEM_LIMIT)


def _expand_rows(p, rows):
    c = p.shape[-1]
    return jnp.broadcast_to(p[None], (rows // SUBLANES, SUBLANES, c)).reshape(rows, c)


def _norm_mod(x, g, shift, scale):
    ms = jnp.mean(x * x, axis=-1, keepdims=True)
    y = (x * lax.rsqrt(ms + EPS)) * g
    rows = x.shape[0]
    return y * (1.0 + _expand_rows(scale, rows)) + _expand_rows(shift, rows)


def _dot(a, b):
    return jnp.dot(a, b, preferred_element_type=F32)


def _adaln_kernel(c_ref, w_ref, b_ref, o_ref):
    c = c_ref[...]
    s = (c * jax.nn.sigmoid(c)).astype(BF16)
    o_ref[...] = _dot(s, w_ref[...].astype(BF16)) + b_ref[...]


def _adaln(cvec, w_mod, b_mod):
    tn = 1024
    n_rows = cvec.shape[0]
    return pl.pallas_call(
        _adaln_kernel,
        grid=(DEPTH, N_MOD * D_MODEL // tn),
        in_specs=[
            pl.BlockSpec((n_rows, D_MODEL), lambda l, j: (0, 0)),
            pl.BlockSpec((None, D_MODEL, tn), lambda l, j: (l, 0, j)),
            pl.BlockSpec((None, 1, tn), lambda l, j: (l, 0, j)),
        ],
        out_specs=pl.BlockSpec((None, n_rows, tn), lambda l, j: (l, 0, j)),
        out_shape=jax.ShapeDtypeStruct((DEPTH, n_rows, N_MOD * D_MODEL), F32),
        compiler_params=_cparams(("parallel", "parallel")),
        name="adaln",
    )(cvec, w_mod, b_mod.reshape(DEPTH, 1, N_MOD * D_MODEL))


def _ffn_kernel(x_ref, pat_ref, g_ref, wa_ref, wb_ref, wo_ref, o_ref, n_scr, acc_scr, *, k0):
    j = pl.program_id(1)

    @pl.when(j == 0)
    def _():
        n = _norm_mod(x_ref[...], g_ref[...], pat_ref[k0], pat_ref[k0 + 1])
        n_scr[...] = n.astype(BF16)
        acc_scr[...] = jnp.zeros_like(acc_scr)

    n = n_scr[...]
    a = _dot(n, wa_ref[...].astype(BF16))
    b = _dot(n, wb_ref[...].astype(BF16))
    act = (a * jax.nn.sigmoid(a) * b).astype(BF16)
    acc_scr[...] += _dot(act, wo_ref[...].astype(BF16))

    @pl.when(j == pl.num_programs(1) - 1)
    def _():
        gate = _expand_rows(pat_ref[k0 + 2], x_ref.shape[0])
        o_ref[...] = x_ref[...] + (0.5 * gate) * acc_scr[...]


def _ffn(x, pat, norm_g, w_ffn_in, w_ffn_out, layer, sub, k0):
    n_tok = x.shape[0]
    n_ff = D_FF // FF_TILE
    return pl.pallas_call(
        functools.partial(_ffn_kernel, k0=k0),
        grid=(n_tok // ROW_TILE, n_ff),
        in_specs=[
            pl.BlockSpec((ROW_TILE, D_MODEL), lambda i, j: (i, 0)),
            pl.BlockSpec((N_MOD, SUBLANES, D_MODEL), lambda i, j: (0, 0, 0)),
            pl.BlockSpec((1, D_MODEL), lambda i, j: (0, 0)),
            pl.BlockSpec((None, None, D_MODEL, FF_TILE), lambda i, j: (layer, sub, 0, j)),
            pl.BlockSpec((None, None, D_MODEL, FF_TILE), lambda i, j: (layer, sub, 0, n_ff + j)),
            pl.BlockSpec((None, None, FF_TILE, D_MODEL), lambda i, j: (layer, sub, j, 0)),
        ],
        out_specs=pl.BlockSpec((ROW_TILE, D_MODEL), lambda i, j: (i, 0)),
        out_shape=jax.ShapeDtypeStruct((n_tok, D_MODEL), F32),
        scratch_shapes=[pltpu.VMEM((ROW_TILE, D_MODEL), BF16), pltpu.VMEM((ROW_TILE, D_MODEL), F32)],
        compiler_params=_cparams(("parallel", "arbitrary")),
        name="ffn",
    )(x, pat, norm_g, w_ffn_in, w_ffn_in, w_ffn_out)


def _rope(x, cos, sin):
    lane = lax.broadcasted_iota(jnp.int32, (x.shape[0], LANES), 1)
    first_half = (lane % 32) < 16
    out = []
    for c in range(x.shape[1] // LANES):
        xc = x[:, c * LANES:(c + 1) * LANES]
        partner = jnp.where(first_half, pltpu.roll(xc, LANES - 16, 1), pltpu.roll(xc, 16, 1))
        out.append(xc * cos + partner * sin)
    return jnp.concatenate(out, axis=1)


def _inproj_kernel(*refs, rope):
    if rope:
        x_ref, pat_ref, g_ref, w_ref, cos_ref, sin_ref, o_ref, n_scr = refs
    else:
        x_ref, pat_ref, g_ref, w_ref, o_ref, n_scr = refs
    j = pl.program_id(1)

    @pl.when(j == 0)
    def _():
        n_scr[...] = _norm_mod(x_ref[...], g_ref[...], pat_ref[3], pat_ref[4]).astype(BF16)

    y = _dot(n_scr[...], w_ref[...].astype(BF16))

    @pl.when(j == 1)
    def _():
        q = _rope(y, cos_ref[...], sin_ref[...]) if rope else y
        o_ref[...] = q * (DH ** -0.5)

    @pl.when(j == 2)
    def _():
        o_ref[...] = _rope(y, cos_ref[...], sin_ref[...]) if rope else y

    @pl.when((j != 1) & (j != 2))
    def _():
        o_ref[...] = y


def _inproj(x, pat, norm_g, w_in, layer, rope_tabs):
    n_tok = x.shape[0]
    rope = rope_tabs is not None
    in_specs = [
        pl.BlockSpec((ROW_TILE, D_MODEL), lambda i, j: (i, 0)),
        pl.BlockSpec((N_MOD, SUBLANES, D_MODEL), lambda i, j: (0, 0, 0)),
        pl.BlockSpec((1, D_MODEL), lambda i, j: (0, 0)),
        pl.BlockSpec((None, D_MODEL, D_BRANCH), lambda i, j: (layer, 0, j)),
    ]
    args = [x, pat, norm_g, w_in]
    if rope:
        in_specs += [pl.BlockSpec((ROW_TILE, LANES), lambda i, j: (i, 0))] * 2
        args += list(rope_tabs)
    return pl.pallas_call(
        functools.partial(_inproj_kernel, rope=rope),
        grid=(n_tok // ROW_TILE, 5),
        in_specs=in_specs,
        out_specs=pl.BlockSpec((None, ROW_TILE, D_BRANCH), lambda i, j: (j, i, 0)),
        out_shape=jax.ShapeDtypeStruct((5, n_tok, D_BRANCH), F32),
        scratch_shapes=[pltpu.VMEM((ROW_TILE, D_MODEL), BF16)],
        compiler_params=_cparams(("parallel", "arbitrary")),
        name="inproj",
    )(*args)


def _s5_kernel(u_ref, bb_ref, cc_ref, ar_ref, ai_ref, h0_ref, dsk_ref, wglu_ref, ya_ref, hfin_ref,
               xs_scr, ypart_scr, h_scr, *, n_seq, n_steps, n_chunks):
    s = pl.program_id(0)
    backward = s >= n_chunks
    chunk = jnp.where(backward, 2 * n_chunks - 1 - s, s)
    rows = n_seq * n_steps

    @pl.when((s == 0) | (s == n_chunks))
    def _():
        h_scr[...] = h0_ref[...]

    u = u_ref[...]
    ub = u.astype(BF16)
    ys = []
    for j in range(N_SLAB):
        lanes = slice(j * SLAB_STATE, (j + 1) * SLAB_STATE)
        xs_scr[...] = _dot(ub[:, j * LANES:(j + 1) * LANES], bb_ref[j])
        ar = jnp.broadcast_to(ar_ref[:, lanes], (n_seq, SLAB_STATE))
        ai = jnp.broadcast_to(ai_ref[:, lanes], (n_seq, SLAB_STATE))

        def step(t, carry, ar=ar, ai=ai):
            hr, hi = carry
            tt = jnp.where(backward, n_steps - 1 - t, t)
            r0 = pl.multiple_of(tt * n_seq, n_seq)
            xr = xs_scr[pl.ds(r0, n_seq), 0:SLAB_STATE]
            xi = xs_scr[pl.ds(r0, n_seq), SLAB_STATE:2 * SLAB_STATE]
            nhr = ar * hr - ai * hi + xr
            nhi = ar * hi + ai * hr + xi
            xs_scr[pl.ds(r0, n_seq), 0:SLAB_STATE] = nhr
            xs_scr[pl.ds(r0, n_seq), SLAB_STATE:2 * SLAB_STATE] = nhi
            return nhr, nhi

        hr, hi = lax.fori_loop(0, n_steps, step, (h_scr[0, :, lanes], h_scr[1, :, lanes]), unroll=4)
        h_scr[0, :, lanes] = hr
        h_scr[1, :, lanes] = hi
        ys.append(_dot(xs_scr[...].astype(BF16), cc_ref[j]))
    y = jnp.concatenate(ys, axis=1)
    row0 = pl.multiple_of(chunk * rows, rows)

    @pl.when(jnp.logical_not(backward))
    def _():
        ypart_scr[pl.ds(row0, rows), :] = y + dsk_ref[...] * u

    @pl.when(backward)
    def _():
        g = jax.nn.gelu(ypart_scr[pl.ds(row0, rows), :] + y)
        ya_ref[...] = g * jax.nn.sigmoid(_dot(g.astype(BF16), wglu_ref[...].astype(BF16)))

    @pl.when((s == n_chunks - 1) | (s == 2 * n_chunks - 1))
    def _():
        hfin_ref[...] = h_scr[...]


def _s5(proj, ssm, h0, n_seq):
    n_tok = proj.shape[1]
    n_steps = S5_CHUNK_ROWS // n_seq
    n_chunks = n_tok // S5_CHUNK_ROWS

    def chunk_of(s):
        return jnp.where(s < n_chunks, s, 2 * n_chunks - 1 - s)

    def out_chunk_of(s):
        return jnp.where(s < n_chunks, n_chunks - 1, 2 * n_chunks - 1 - s)

    def direction(s):
        return jnp.where(s < n_chunks, 0, 1)

    return pl.pallas_call(
        functools.partial(_s5_kernel, n_seq=n_seq, n_steps=n_steps, n_chunks=n_chunks),
        grid=(2 * n_chunks,),
        in_specs=[
            pl.BlockSpec((None, S5_CHUNK_ROWS, D_BRANCH), lambda s: (0, chunk_of(s), 0)),
            pl.BlockSpec((None, N_SLAB, LANES, 2 * SLAB_STATE), lambda s: (direction(s), 0, 0, 0)),
            pl.BlockSpec((None, N_SLAB, 2 * SLAB_STATE, LANES), lambda s: (direction(s), 0, 0, 0)),
            pl.BlockSpec((None, 1, N_STATE), lambda s: (direction(s), 0, 0)),
            pl.BlockSpec((None, 1, N_STATE), lambda s: (direction(s), 0, 0)),
            pl.BlockSpec((None, 2, n_seq, N_STATE), lambda s: (direction(s), 0, 0, 0)),
            pl.BlockSpec((1, D_BRANCH), lambda s: (0, 0)),
            pl.BlockSpec((D_BRANCH, D_BRANCH), lambda s: (0, 0)),
        ],
        out_specs=[
            pl.BlockSpec((S5_CHUNK_ROWS, D_BRANCH), lambda s: (out_chunk_of(s), 0)),
            pl.BlockSpec((None, 2, n_seq, N_STATE), lambda s: (direction(s), 0, 0, 0)),
        ],
        out_shape=[
            jax.ShapeDtypeStruct((n_tok, D_BRANCH), F32),
            jax.ShapeDtypeStruct((2, 2, n_seq, N_STATE), F32),
        ],
        scratch_shapes=[
            pltpu.VMEM((S5_CHUNK_ROWS, 2 * SLAB_STATE), F32),
            pltpu.VMEM((n_tok, D_BRANCH), F32),
            pltpu.VMEM((2, n_seq, N_STATE), F32),
        ],
        compiler_params=_cparams(("arbitrary",)),
        name="s5",
    )(proj, ssm["bb"], ssm["cc"], ssm["a_re"], ssm["a_im"], h0, ssm["d_skip"], ssm["w_glu"])


def _s5_params(lam_re, lam_im, log_dt, b_re, b_im, c_re, c_im):
    dt = jnp.exp(log_dt)[..., None]
    mag = jnp.exp(lam_re * dt)
    abr = mag * jnp.cos(lam_im * dt)
    abi = mag * jnp.sin(lam_im * dt)
    den = lam_re * lam_re + lam_im * lam_im
    nr = abr - 1.0
    kr = (nr * lam_re + abi * lam_im) / den
    ki = (abi * lam_re - nr * lam_im) / den
    bbr = kr[..., None] * b_re - ki[..., None] * b_im
    bbi = kr[..., None] * b_im + ki[..., None] * b_re
    per_slab = S5_GROUPS // N_SLAB
    eye = jnp.eye(per_slab, dtype=F32)

    def pack_in(w):
        w = w.reshape(2, N_SLAB, per_slab, S5_STATE, S5_GROUP)
        return jnp.einsum("djgpc,gh->djgchp", w, eye).reshape(2, N_SLAB, LANES, SLAB_STATE)

    def pack_out(w):
        w = w.reshape(2, N_SLAB, per_slab, S5_GROUP, S5_STATE)
        return jnp.einsum("djgcp,gh->djgphc", w, eye).reshape(2, N_SLAB, SLAB_STATE, LANES)

    bb = jnp.concatenate([pack_in(bbr), pack_in(bbi)], axis=-1).astype(BF16)
    cc = jnp.concatenate([pack_out(c_re), -pack_out(c_im)], axis=-2).astype(BF16)
    return bb, cc, abr.reshape(2, 1, N_STATE), abi.reshape(2, 1, N_STATE)


def _attn_kernel(*refs, n_ctx, lam_init):
    if n_ctx:
        lam_ref, q_ref, k_ref, v_ref, ck_ref, cv_ref, g_ref, o_ref, k_scr, v_scr = refs
    else:
        lam_ref, q_ref, k_ref, v_ref, g_ref, o_ref, k_scr, v_scr = refs
    n_own = k_ref.shape[0]

    @pl.when(pl.program_id(1) == 0)
    def _():
        k_scr[0:n_own, :] = k_ref[...].astype(BF16)
        v_scr[0:n_own, :] = v_ref[...].astype(BF16)
        if n_ctx:
            k_scr[n_own:n_own + n_ctx, :] = ck_ref[...].astype(BF16)
            v_scr[n_own:n_own + n_ctx, :] = cv_ref[...].astype(BF16)

    lam = lam_ref[0]
    q = q_ref[...]
    lane = lax.broadcasted_iota(jnp.int32, (q.shape[0], LANES), 1)
    for h in range(N_HEADS):
        lanes = slice(h * LANES, (h + 1) * LANES)
        qh = q[:, lanes]
        kh = k_scr[:, lanes]
        probs = []
        for m in range(2):
            qm = jnp.where((lane < DH) if m == 0 else (lane >= DH), qh, 0.0).astype(BF16)
            sc = lax.dot_general(qm, kh, (((1,), (1,)), ((), ())), preferred_element_type=F32)
            e = jnp.exp(sc - jnp.max(sc, axis=-1, keepdims=True))
            probs.append((e, 1.0 / jnp.sum(e, axis=-1, keepdims=True)))
        (e1, r1), (e2, r2) = probs
        a = (e1 * r1 - e2 * (lam * r2)).astype(BF16)
        o = _dot(a, v_scr[:, lanes])
        ms = jnp.mean(o * o, axis=-1, keepdims=True)
        o_ref[:, lanes] = ((o * lax.rsqrt(ms + EPS)) * g_ref[...]) * (1.0 - lam_init)


def _attention(proj, lam, attn_g, n_seq, lam_init, cache=None, layer=0, tq=256):
    n_tok = proj.shape[1]
    seq = n_tok // n_seq
    width = n_seq * D_BRANCH
    proj2 = proj.reshape(5, seq, width)
    n_ctx = 0 if cache is None else cache[0].shape[2]
    in_specs = [
        pl.BlockSpec(memory_space=pltpu.SMEM),
        pl.BlockSpec((None, tq, D_BRANCH), lambda b, i: (1, i, b)),
        pl.BlockSpec((None, seq, D_BRANCH), lambda b, i: (2, 0, b)),
        pl.BlockSpec((None, seq, D_BRANCH), lambda b, i: (3, 0, b)),
    ]
    args = [lam, proj2, proj2, proj2]
    if n_ctx:
        in_specs += [pl.BlockSpec((None, None, n_ctx, D_BRANCH), lambda b, i: (b, layer, 0, 0))] * 2
        args += list(cache)
    in_specs.append(pl.BlockSpec((1, DV), lambda b, i: (0, 0)))
    args.append(attn_g)
    out = pl.pallas_call(
        functools.partial(_attn_kernel, n_ctx=n_ctx, lam_init=lam_init),
        grid=(n_seq, seq // tq),
        in_specs=in_specs,
        out_specs=pl.BlockSpec((tq, D_BRANCH), lambda b, i: (i, b)),
        out_shape=jax.ShapeDtypeStruct((seq, width), F32),
        scratch_shapes=[pltpu.VMEM((seq + n_ctx, D_BRANCH), BF16)] * 2,
        compiler_params=_cparams(("parallel", "arbitrary")),
        name="attn",
    )(*args)
    return out.reshape(n_tok, D_BRANCH)


def _pool_kernel(zc_ref, zp_ref, zn_ref, w_ref, sc_ref, o_ref, pad_scr, *, n_seq, n_steps, seq):
    i = pl.program_id(0)
    halo = POOL_HALO * n_seq
    rows = n_steps * n_seq
    zc = zc_ref[...]
    pad_scr[0:halo, :] = jnp.where(i > 0, zp_ref[...], 0.0)
    pad_scr[halo:halo + rows, :] = zc
    pad_scr[halo + rows:2 * halo + rows, :] = jnp.where(i < pl.num_programs(0) - 1, zn_ref[...], 0.0)
    t = i * n_steps + lax.broadcasted_iota(jnp.int32, (rows, POOL_GROUP), 0) // n_seq
    outs = []
    for gi, w in enumerate(POOL_WINDOWS):
        lanes = slice(gi * POOL_GROUP, (gi + 1) * POOL_GROUP)
        tot = None
        for dt in range(-(w // 2), w - w // 2):
            r0 = halo + dt * n_seq
            part = pad_scr[r0:r0 + rows, lanes]
            tot = part if tot is None else tot + part
        lo = jnp.clip(t - w // 2, 0, seq)
        hi = jnp.clip(t - w // 2 + w, 0, seq)
        pooled = tot / (hi - lo).astype(F32) - zc[:, lanes]
        outs.append(_dot(pooled.astype(BF16), w_ref[gi].astype(BF16)))
    o_ref[...] = jnp.concatenate(outs, axis=1) * sc_ref[...]


def _pool(proj, w_pool, pool_scale, n_seq, rows=1024):
    n_tok = proj.shape[1]
    halo = POOL_HALO * n_seq
    per = rows // halo
    last = n_tok // halo - 1
    return pl.pallas_call(
        functools.partial(_pool_kernel, n_seq=n_seq, n_steps=rows // n_seq, seq=n_tok // n_seq),
        grid=(n_tok // rows,),
        in_specs=[
            pl.BlockSpec((None, rows, D_BRANCH), lambda i: (4, i, 0)),
            pl.BlockSpec((None, halo, D_BRANCH), lambda i: (4, jnp.maximum(i * per - 1, 0), 0)),
            pl.BlockSpec((None, halo, D_BRANCH), lambda i: (4, jnp.minimum((i + 1) * per, last), 0)),
            pl.BlockSpec((len(POOL_WINDOWS), POOL_GROUP, POOL_GROUP), lambda i: (0, 0, 0)),
            pl.BlockSpec((1, D_BRANCH), lambda i: (0, 0)),
        ],
        out_specs=pl.BlockSpec((rows, D_BRANCH), lambda i: (i, 0)),
        out_shape=jax.ShapeDtypeStruct((n_tok, D_BRANCH), F32),
        scratch_shapes=[pltpu.VMEM((rows + 2 * halo, D_BRANCH), F32)],
        compiler_params=_cparams(("parallel",)),
        name="pool",
    )(proj, proj, proj, w_pool, pool_scale)


def _merge_kernel(x_ref, pat_ref, g_ref, ya_ref, yb_ref, yc_ref, wg0_ref, wg1_ref, wbr_ref, wo_ref, o_ref,
                  n_scr, m_scr):
    j = pl.program_id(1)
    half = D_MODEL // 2

    @pl.when(j == 0)
    def _():
        n_scr[...] = _norm_mod(x_ref[...], g_ref[...], pat_ref[3], pat_ref[4]).astype(BF16)
        m_scr[...] = jnp.zeros_like(m_scr)

    n = n_scr[...]
    wbr = wbr_ref[...].astype(BF16)
    for n_branch, y_ref in enumerate((ya_ref, yb_ref, yc_ref)):
        @pl.when(j == n_branch)
        def _(y_ref=y_ref):
            t = _dot(y_ref[...].astype(BF16), wbr)
            g0 = jax.nn.sigmoid(_dot(n, wg0_ref[...].astype(BF16)))
            g1 = jax.nn.sigmoid(_dot(n, wg1_ref[...].astype(BF16)))
            m_scr[:, 0:half] += g0 * t[:, 0:half]
            m_scr[:, half:D_MODEL] += g1 * t[:, half:D_MODEL]

    @pl.when(j == N_BRANCH - 1)
    def _():
        y = _dot(m_scr[...].astype(BF16), wo_ref[...].astype(BF16))
        o_ref[...] = x_ref[...] + _expand_rows(pat_ref[5], x_ref.shape[0]) * y


def _merge(x, pat, norm_g, ya, yb, yc, w_in, w_branch, w_out, layer):
    n_tok = x.shape[0]
    half = D_MODEL // 2
    gate0 = 5 * D_BRANCH // half
    rt = MERGE_ROW_TILE
    tile = pl.BlockSpec((rt, D_BRANCH), lambda i, j: (i, 0))
    return pl.pallas_call(
        _merge_kernel,
        grid=(n_tok // rt, N_BRANCH),
        in_specs=[
            pl.BlockSpec((rt, D_MODEL), lambda i, j: (i, 0)),
            pl.BlockSpec((N_MOD, SUBLANES, D_MODEL), lambda i, j: (0, 0, 0)),
            pl.BlockSpec((1, D_MODEL), lambda i, j: (0, 0)),
            tile, tile, tile,
            pl.BlockSpec((None, D_MODEL, half), lambda i, j: (layer, 0, gate0 + 2 * j)),
            pl.BlockSpec((None, D_MODEL, half), lambda i, j: (layer, 0, gate0 + 2 * j + 1)),
            pl.BlockSpec((None, None, D_BRANCH, D_MODEL), lambda i, j: (layer, j, 0, 0)),
            pl.BlockSpec((None, D_MODEL, D_MODEL), lambda i, j: (layer, 0, 0)),
        ],
        out_specs=pl.BlockSpec((rt, D_MODEL), lambda i, j: (i, 0)),
        out_shape=jax.ShapeDtypeStruct((n_tok, D_MODEL), F32),
        scratch_shapes=[pltpu.VMEM((rt, D_MODEL), BF16), pltpu.VMEM((rt, D_MODEL), F32)],
        compiler_params=_cparams(("parallel", "arbitrary")),
        name="merge",
    )(x, pat, norm_g, ya, yb, yc, w_in, w_in, w_branch, w_out)


def _final_norm_kernel(x_ref, g_ref, o_ref):
    x = x_ref[...]
    ms = jnp.mean(x * x, axis=-1, keepdims=True)
    o_ref[...] = (x * lax.rsqrt(ms + EPS)) * g_ref[...]


def _final_norm(x, g):
    n_tok = x.shape[0]
    return pl.pallas_call(
        _final_norm_kernel,
        grid=(n_tok // ROW_TILE,),
        in_specs=[pl.BlockSpec((ROW_TILE, D_MODEL), lambda i: (i, 0)), pl.BlockSpec((1, D_MODEL), lambda i: (0, 0))],
        out_specs=pl.BlockSpec((ROW_TILE, D_MODEL), lambda i: (i, 0)),
        out_shape=jax.ShapeDtypeStruct((n_tok, D_MODEL), F32),
        compiler_params=_cparams(("parallel",)),
        name="final_norm",
    )(x, g)


def _rope_tables(seq, n_seq):
    rows = seq // GRID_W
    row = jnp.repeat(jnp.arange(rows, dtype=F32), GRID_W)
    col = jnp.tile(jnp.arange(GRID_W, dtype=F32), rows)
    n_freq = DH // 4
    inv = ROPE_BASE ** (-jnp.arange(n_freq, dtype=F32) / n_freq)
    ar = row[:, None] * inv
    ac = col[:, None] * inv
    cos = jnp.concatenate([jnp.cos(ar)] * 2 + [jnp.cos(ac)] * 2, axis=1)
    sin = jnp.concatenate([-jnp.sin(ar), jnp.sin(ar), -jnp.sin(ac), jnp.sin(ac)], axis=1)
    expand = lambda t: jnp.repeat(jnp.tile(t, (1, LANES // DH)), n_seq, axis=0)
    return expand(cos), expand(sin)


def _to_time_major(x):
    b, seq, d = x.shape
    return x.transpose(1, 0, 2).reshape(seq * b, d)


def _from_time_major(x, n_seq):
    n_tok, d = x.shape
    return x.reshape(n_tok // n_seq, n_seq, d).transpose(1, 0, 2)


def kernel(x_prompt, x_sample, cache_k, cache_v, state_ssm, c, c_ctx, norm_g, w_mod, b_mod, w_ffn_in, w_ffn_out, w_in, ssm_lam_re, ssm_lam_im, ssm_log_dt, ssm_b_re, ssm_b_im, ssm_c_re, ssm_c_im, ssm_d, w_glu, lam_q1, lam_k1, lam_q2, lam_k2, attn_norm_g, w_pool, pool_scale, w_branch, w_out, final_norm_g):
    n_p, seq_p, _ = x_prompt.shape
    n_s, seq_s, _ = x_sample.shape
    n_past = cache_k.shape[2]

    cvec = jnp.concatenate([c_ctx[None, :], c, jnp.zeros((16 - 1 - n_s, D_MODEL), F32)], axis=0)
    mod = _adaln(cvec, w_mod, b_mod).reshape(DEPTH, 16, N_MOD, D_MODEL)
    pat_p = jnp.broadcast_to(mod[:, 0, :, None, :], (DEPTH, N_MOD, SUBLANES, D_MODEL))
    pat_s = mod[:, 1:1 + n_s].transpose(0, 2, 1, 3)

    xp = _to_time_major(x_prompt)
    xs = _to_time_major(x_sample)
    rope_tabs = _rope_tables(seq_s, n_s)
    cache_k2 = cache_k.reshape(n_s, DEPTH, n_past, D_BRANCH)
    cache_v2 = cache_v.reshape(n_s, DEPTH, n_past, D_BRANCH)
    h0_p = jnp.zeros((2, 2, n_p, N_STATE), F32)

    new_k, new_v, new_s = [], [], []
    for l in range(DEPTH):
        lam_init = 0.8 - 0.6 * math.exp(-0.3 * l)
        lam = (jnp.exp(jnp.sum(lam_q1[l] * lam_k1[l])) - jnp.exp(jnp.sum(lam_q2[l] * lam_k2[l])) + lam_init).reshape(1)
        bb, cc, a_re, a_im = _s5_params(ssm_lam_re[l], ssm_lam_im[l], ssm_log_dt[l], ssm_b_re[l], ssm_b_im[l],
                                        ssm_c_re[l], ssm_c_im[l])
        ssm = dict(bb=bb, cc=cc, a_re=a_re, a_im=a_im, d_skip=ssm_d[l][None, :], w_glu=w_glu[l])
        ng = norm_g[l]
        h0_s = state_ssm[:, l].reshape(n_s, 2, 2, N_STATE).transpose(1, 2, 0, 3)

        def layer(x, pat, n_seq, h0, rope, cache):
            x = _ffn(x, pat, ng[0:1], w_ffn_in, w_ffn_out, l, 0, 0)
            proj = _inproj(x, pat, ng[1:2], w_in, l, rope)
            ya, h_fin = _s5(proj, ssm, h0, n_seq)
            yb = _attention(proj, lam, attn_norm_g[l][None, :], n_seq, lam_init, cache, l)
            yc = _pool(proj, w_pool[l], pool_scale[l][None, :], n_seq)
            x = _merge(x, pat, ng[1:2], ya, yb, yc, w_in, w_branch, w_out, l)
            x = _ffn(x, pat, ng[2:3], w_ffn_in, w_ffn_out, l, 1, 6)
            return x, proj, h_fin

        xp, proj_p, h_fin = layer(xp, pat_p[l], n_p, h0_p, None, None)
        xs, _, _ = layer(xs, pat_s[l], n_s, h0_s, rope_tabs, (cache_k2, cache_v2))
        new_k.append(_from_time_major(proj_p[2], n_p).reshape(n_p, seq_p, N_HEADS, 2, DH))
        new_v.append(_from_time_major(proj_p[3], n_p).reshape(n_p, seq_p, N_HEADS, DV))
        new_s.append(h_fin.transpose(2, 0, 1, 3).reshape(n_p, 2, 2, S5_GROUPS, S5_STATE))

    y_prompt = _from_time_major(_final_norm(xp, final_norm_g[None, :]), n_p)
    y_sample = _from_time_major(_final_norm(xs, final_norm_g[None, :]), n_s)
    return (y_prompt, y_sample, jnp.stack(new_k, axis=1), jnp.stack(new_v, axis=1), jnp.stack(new_s, axis=1))
```

```python
import functools
import math

import jax
import jax.numpy as jnp
from jax import lax
from jax.experimental import pallas as pl
from jax.experimental.pallas import tpu as pltpu

F32 = jnp.float32
BF16 = jnp.bfloat16

D_MODEL = 1024
DEPTH = 4
GRID_W = 64
D_BRANCH = 512
S5_GROUP = 16
S5_GROUPS = 32
S5_STATE = 64
N_STATE = S5_GROUPS * S5_STATE
DH = 64
N_HEADS = 4
DV = 128
POOL_WINDOWS = (2, 4, 8, 16)
POOL_GROUP = 128
POOL_PAD = 16
D_FF = 2816
N_MOD = 9
N_BRANCH = 3
N_PROJ = 5
ROPE_BASE = 10000.0
EPS = 1e-6

LANES = 128
N_SLAB = D_BRANCH // LANES
SLAB_STATE = N_STATE // N_SLAB
ROW_TILE = 512
FF_SPLIT = 2
S5_CHUNK_ROWS = 512
Q_TILE = 256
VMEM_LIMIT = 56 * 1024 * 1024


def _cparams(sem):
    return pltpu.CompilerParams(dimension_semantics=sem, vmem_limit_bytes=VMEM_LIMIT)


def _resident(block_shape, index_map):
    return pl.BlockSpec(block_shape, index_map, pipeline_mode=pl.Buffered(1))


def _norm_mod(x, g, shift, scale):
    ms = jnp.mean(x * x, axis=-1, keepdims=True)
    y = (x * lax.rsqrt(ms + EPS)) * g
    return y * (1.0 + scale) + shift


def _dot(a, b):
    return jnp.dot(a, b, preferred_element_type=F32)


class _Rows:
    def __init__(self, n_p, seq_p, n_s, seq_s):
        self.n_p, self.seq_p, self.n_s, self.seq_s = n_p, seq_p, n_s, seq_s
        self.tok_p = n_p * seq_p
        self.tok_s = n_s * seq_s
        self.n_tok = self.tok_p + self.tok_s
        assert self.tok_p % ROW_TILE == 0 and seq_s % ROW_TILE == 0
        self.tiles_p = self.tok_p // ROW_TILE

    def mod_row(self, i):
        return jnp.where(i < self.tiles_p, 0, 1 + (i - self.tiles_p) // (self.seq_s // ROW_TILE))


def _adaln_kernel(c_ref, w_ref, b_ref, o_ref):
    c = c_ref[...]
    s = (c * jax.nn.sigmoid(c)).astype(BF16)
    o_ref[...] = _dot(s, w_ref[...].astype(BF16)) + b_ref[...]


def _adaln(cvec, w_mod, b_mod):
    tn = 1024
    n_rows = cvec.shape[0]
    return pl.pallas_call(
        _adaln_kernel,
        grid=(DEPTH, N_MOD * D_MODEL // tn),
        in_specs=[
            pl.BlockSpec((n_rows, D_MODEL), lambda l, j: (0, 0)),
            pl.BlockSpec((None, D_MODEL, tn), lambda l, j: (l, 0, j)),
            pl.BlockSpec((None, 1, tn), lambda l, j: (l, 0, j)),
        ],
        out_specs=pl.BlockSpec((None, n_rows, tn), lambda l, j: (l, 0, j)),
        out_shape=jax.ShapeDtypeStruct((DEPTH, n_rows, N_MOD * D_MODEL), F32),
        compiler_params=_cparams(("parallel", "parallel")),
        name="adaln",
    )(cvec, w_mod, b_mod.reshape(DEPTH, 1, N_MOD * D_MODEL))


def _ffn_kernel(x_ref, pat_ref, g_ref, wa_ref, wb_ref, wo_ref, o_ref, *, k0):
    x = x_ref[...]
    n = _norm_mod(x, g_ref[...], pat_ref[k0:k0 + 1], pat_ref[k0 + 1:k0 + 2]).astype(BF16)
    piece = D_FF // FF_SPLIT
    y = None
    for h in range(FF_SPLIT):
        cols = slice(h * piece, (h + 1) * piece)
        a = _dot(n, wa_ref[:, cols])
        b = _dot(n, wb_ref[:, cols])
        act = (a * jax.nn.sigmoid(a) * b).astype(BF16)
        part = _dot(act, wo_ref[cols, :])
        y = part if y is None else y + part
    o_ref[...] = x + (0.5 * pat_ref[k0 + 2:k0 + 3]) * y


def _ffn(rows, x, pat, norm_g, w_ab, w_o, layer, sub, k0):
    return pl.pallas_call(
        functools.partial(_ffn_kernel, k0=k0),
        grid=(rows.n_tok // ROW_TILE,),
        in_specs=[
            pl.BlockSpec((ROW_TILE, D_MODEL), lambda i: (i, 0)),
            pl.BlockSpec((None, N_MOD, D_MODEL), lambda i: (rows.mod_row(i), 0, 0)),
            pl.BlockSpec((1, D_MODEL), lambda i: (0, 0)),
            _resident((None, None, D_MODEL, D_FF), lambda i: (layer, sub, 0, 0)),
            _resident((None, None, D_MODEL, D_FF), lambda i: (layer, sub, 0, 1)),
            _resident((None, None, D_FF, D_MODEL), lambda i: (layer, sub, 0, 0)),
        ],
        out_specs=pl.BlockSpec((ROW_TILE, D_MODEL), lambda i: (i, 0)),
        out_shape=jax.ShapeDtypeStruct((rows.n_tok, D_MODEL), F32),
        compiler_params=_cparams(("parallel",)),
        name="ffn",
    )(x, pat, norm_g, w_ab, w_ab, w_o)


def _rope(x, cos, sin):
    lane = lax.broadcasted_iota(jnp.int32, (x.shape[0], LANES), 1)
    first_half = (lane % 32) < 16
    out = []
    for c in range(x.shape[1] // LANES):
        xc = x[:, c * LANES:(c + 1) * LANES]
        partner = jnp.where(first_half, pltpu.roll(xc, LANES - 16, 1), pltpu.roll(xc, 16, 1))
        out.append(xc * cos + partner * sin)
    return jnp.concatenate(out, axis=1)


def _inproj_kernel(x_ref, pat_ref, g_ref, w_ref, cos_ref, sin_ref, u_ref, q_ref, k_ref, v_ref, z_ref, *, tiles_p):
    n = _norm_mod(x_ref[...], g_ref[...], pat_ref[3:4], pat_ref[4:5]).astype(BF16)
    y = _dot(n, w_ref[...])
    col = lambda c: y[:, c * D_BRANCH:(c + 1) * D_BRANCH]
    u_ref[...] = col(0)
    v_ref[...] = col(3)
    z_ref[...] = col(4)
    latent = pl.program_id(0) >= tiles_p
    scale = DH ** -0.5

    @pl.when(latent)
    def _():
        q_ref[...] = (_rope(col(1), cos_ref[...], sin_ref[...]) * scale).astype(BF16)
        k_ref[...] = _rope(col(2), cos_ref[...], sin_ref[...])

    @pl.when(jnp.logical_not(latent))
    def _():
        q_ref[...] = (col(1) * scale).astype(BF16)
        k_ref[...] = col(2)


def _inproj(rows, x, pat, norm_g, w_p, layer, rope_tabs):
    tabs_per_seq = rows.seq_s // ROW_TILE
    tab_spec = pl.BlockSpec((ROW_TILE, LANES), lambda i: (jnp.maximum(i - rows.tiles_p, 0) % tabs_per_seq, 0))
    out_spec = pl.BlockSpec((ROW_TILE, D_BRANCH), lambda i: (i, 0))
    f32_out = jax.ShapeDtypeStruct((rows.n_tok, D_BRANCH), F32)
    return pl.pallas_call(
        functools.partial(_inproj_kernel, tiles_p=rows.tiles_p),
        grid=(rows.n_tok // ROW_TILE,),
        in_specs=[
            pl.BlockSpec((ROW_TILE, D_MODEL), lambda i: (i, 0)),
            pl.BlockSpec((None, N_MOD, D_MODEL), lambda i: (rows.mod_row(i), 0, 0)),
            pl.BlockSpec((1, D_MODEL), lambda i: (0, 0)),
            _resident((None, D_MODEL, N_PROJ * D_BRANCH), lambda i: (layer, 0, 0)),
            tab_spec, tab_spec,
        ],
        out_specs=[out_spec] * N_PROJ,
        out_shape=[f32_out, jax.ShapeDtypeStruct((rows.n_tok, D_BRANCH), BF16), f32_out, f32_out, f32_out],
        compiler_params=_cparams(("parallel",)),
        name="inproj",
    )(x, pat, norm_g, w_p, *rope_tabs)


def _s5_kernel(u_ref, bb_ref, cc_ref, ar_ref, ai_ref, h0_ref, dsk_ref, wglu_ref, ya_ref, hfin_ref,
               xs_scr, ypart_scr, h_scr, *, n_seq, n_steps, n_chunks):
    s = pl.program_id(0)
    backward = s >= n_chunks
    chunk = jnp.where(backward, 2 * n_chunks - 1 - s, s)
    rows = n_seq * n_steps

    @pl.when((s == 0) | (s == n_chunks))
    def _():
        h_scr[...] = h0_ref[...]

    u = u_ref[...]
    ub = u.astype(BF16)
    ys = []
    for j in range(N_SLAB):
        lanes = slice(j * SLAB_STATE, (j + 1) * SLAB_STATE)
        xs_scr[...] = _dot(ub[:, j * LANES:(j + 1) * LANES], bb_ref[j])
        ar = jnp.broadcast_to(ar_ref[:, lanes], (n_seq, SLAB_STATE))
        ai = jnp.broadcast_to(ai_ref[:, lanes], (n_seq, SLAB_STATE))

        def step(t, carry, ar=ar, ai=ai):
            hr, hi = carry
            tt = jnp.where(backward, n_steps - 1 - t, t)
            r0 = pl.multiple_of(tt * n_seq, n_seq)
            xr = xs_scr[pl.ds(r0, n_seq), 0:SLAB_STATE]
            xi = xs_scr[pl.ds(r0, n_seq), SLAB_STATE:2 * SLAB_STATE]
            nhr = ar * hr - ai * hi + xr
            nhi = ar * hi + ai * hr + xi
            xs_scr[pl.ds(r0, n_seq), 0:SLAB_STATE] = nhr
            xs_scr[pl.ds(r0, n_seq), SLAB_STATE:2 * SLAB_STATE] = nhi
            return nhr, nhi

        hr, hi = lax.fori_loop(0, n_steps, step, (h_scr[0, :, lanes], h_scr[1, :, lanes]), unroll=4)
        h_scr[0, :, lanes] = hr
        h_scr[1, :, lanes] = hi
        ys.append(_dot(xs_scr[...].astype(BF16), cc_ref[j]))
    y = jnp.concatenate(ys, axis=1)
    row0 = pl.multiple_of(chunk * rows, rows)

    @pl.when(jnp.logical_not(backward))
    def _():
        ypart_scr[pl.ds(row0, rows), :] = y + dsk_ref[...] * u

    @pl.when(backward)
    def _():
        g = jax.nn.gelu(ypart_scr[pl.ds(row0, rows), :] + y)
        ya_ref[...] = g * jax.nn.sigmoid(_dot(g.astype(BF16), wglu_ref[...]))

    @pl.when((s == n_chunks - 1) | (s == 2 * n_chunks - 1))
    def _():
        hfin_ref[...] = h_scr[...]


def _s5(u_t, ssm, h0, n_seq):
    n_tok = u_t.shape[0]
    n_steps = S5_CHUNK_ROWS // n_seq
    n_chunks = n_tok // S5_CHUNK_ROWS

    def chunk_of(s):
        return jnp.where(s < n_chunks, s, 2 * n_chunks - 1 - s)

    def out_chunk_of(s):
        return jnp.where(s < n_chunks, n_chunks - 1, 2 * n_chunks - 1 - s)

    def direction(s):
        return jnp.where(s < n_chunks, 0, 1)

    return pl.pallas_call(
        functools.partial(_s5_kernel, n_seq=n_seq, n_steps=n_steps, n_chunks=n_chunks),
        grid=(2 * n_chunks,),
        in_specs=[
            pl.BlockSpec((S5_CHUNK_ROWS, D_BRANCH), lambda s: (chunk_of(s), 0)),
            pl.BlockSpec((None, N_SLAB, LANES, 2 * SLAB_STATE), lambda s: (direction(s), 0, 0, 0)),
            pl.BlockSpec((None, N_SLAB, 2 * SLAB_STATE, LANES), lambda s: (direction(s), 0, 0, 0)),
            pl.BlockSpec((None, 1, N_STATE), lambda s: (direction(s), 0, 0)),
            pl.BlockSpec((None, 1, N_STATE), lambda s: (direction(s), 0, 0)),
            pl.BlockSpec((None, 2, n_seq, N_STATE), lambda s: (direction(s), 0, 0, 0)),
            pl.BlockSpec((1, D_BRANCH), lambda s: (0, 0)),
            pl.BlockSpec((D_BRANCH, D_BRANCH), lambda s: (0, 0)),
        ],
        out_specs=[
            pl.BlockSpec((S5_CHUNK_ROWS, D_BRANCH), lambda s: (out_chunk_of(s), 0)),
            pl.BlockSpec((None, 2, n_seq, N_STATE), lambda s: (direction(s), 0, 0, 0)),
        ],
        out_shape=[
            jax.ShapeDtypeStruct((n_tok, D_BRANCH), F32),
            jax.ShapeDtypeStruct((2, 2, n_seq, N_STATE), F32),
        ],
        scratch_shapes=[
            pltpu.VMEM((S5_CHUNK_ROWS, 2 * SLAB_STATE), F32),
            pltpu.VMEM((n_tok, D_BRANCH), F32),
            pltpu.VMEM((2, n_seq, N_STATE), F32),
        ],
        compiler_params=_cparams(("arbitrary",)),
        name="s5",
    )(u_t, ssm["bb"], ssm["cc"], ssm["a_re"], ssm["a_im"], h0, ssm["d_skip"], ssm["w_glu"])


def _s5_params(lam_re, lam_im, log_dt, b_re, b_im, c_re, c_im):
    dt = jnp.exp(log_dt)[..., None]
    mag = jnp.exp(lam_re * dt)
    abr = mag * jnp.cos(lam_im * dt)
    abi = mag * jnp.sin(lam_im * dt)
    den = lam_re * lam_re + lam_im * lam_im
    nr = abr - 1.0
    kr = (nr * lam_re + abi * lam_im) / den
    ki = (abi * lam_re - nr * lam_im) / den
    bbr = kr[..., None] * b_re - ki[..., None] * b_im
    bbi = kr[..., None] * b_im + ki[..., None] * b_re
    per_slab = S5_GROUPS // N_SLAB
    eye = jnp.eye(per_slab, dtype=F32)

    def pack_in(w):
        w = w.reshape(2, N_SLAB, per_slab, S5_STATE, S5_GROUP)
        return jnp.einsum("djgpc,gh->djgchp", w, eye).reshape(2, N_SLAB, LANES, SLAB_STATE)

    def pack_out(w):
        w = w.reshape(2, N_SLAB, per_slab, S5_GROUP, S5_STATE)
        return jnp.einsum("djgcp,gh->djgphc", w, eye).reshape(2, N_SLAB, SLAB_STATE, LANES)

    bb = jnp.concatenate([pack_in(bbr), pack_in(bbi)], axis=-1).astype(BF16)
    cc = jnp.concatenate([pack_out(c_re), -pack_out(c_im)], axis=-2).astype(BF16)
    return bb, cc, abr.reshape(2, 1, N_STATE), abi.reshape(2, 1, N_STATE)


def _attn_kernel(*refs, n_ctx, lam_init):
    if n_ctx:
        lam_ref, q_ref, k_ref, v_ref, ck_ref, cv_ref, g_ref, o_ref, k_scr, v_scr = refs
    else:
        lam_ref, q_ref, k_ref, v_ref, g_ref, o_ref, k_scr, v_scr = refs
    n_own = k_ref.shape[0]

    @pl.when(pl.program_id(1) == 0)
    def _():
        k_scr[0:n_own, :] = k_ref[...].astype(BF16)
        v_scr[0:n_own, :] = v_ref[...].astype(BF16)
        if n_ctx:
            k_scr[n_own:n_own + n_ctx, :] = ck_ref[...].astype(BF16)
            v_scr[n_own:n_own + n_ctx, :] = cv_ref[...].astype(BF16)

    lam = lam_ref[0]
    q = q_ref[...]
    lane = lax.broadcasted_iota(jnp.int32, (q.shape[0], LANES), 1)
    for h in range(N_HEADS):
        lanes = slice(h * LANES, (h + 1) * LANES)
        qh = q[:, lanes]
        kh = k_scr[:, lanes]
        probs = []
        for m in range(2):
            qm = jnp.where((lane < DH) if m == 0 else (lane >= DH), qh, jnp.zeros_like(qh))
            sc = lax.dot_general(qm, kh, (((1,), (1,)), ((), ())), preferred_element_type=F32)
            e = jnp.exp(sc - jnp.max(sc, axis=-1, keepdims=True))
            probs.append((e, 1.0 / jnp.sum(e, axis=-1, keepdims=True)))
        (e1, r1), (e2, r2) = probs
        a = (e1 * r1 - e2 * (lam * r2)).astype(BF16)
        o = _dot(a, v_scr[:, lanes])
        ms = jnp.mean(o * o, axis=-1, keepdims=True)
        o_ref[:, lanes] = ((o * lax.rsqrt(ms + EPS)) * g_ref[...]) * (1.0 - lam_init)


def _attention(q, k, v, lam, attn_g, row0, n_seq, seq, lam_init, cache=None, layer=0):
    q_tiles = seq // Q_TILE
    q0 = row0 // Q_TILE
    s0 = row0 // seq
    n_ctx = 0 if cache is None else cache[0].shape[2]
    own = pl.BlockSpec((seq, D_BRANCH), lambda b, i: (s0 + b, 0))
    in_specs = [
        pl.BlockSpec(memory_space=pltpu.SMEM),
        pl.BlockSpec((Q_TILE, D_BRANCH), lambda b, i: (q0 + b * q_tiles + i, 0)),
        own, own,
    ]
    args = [lam, q, k, v]
    if n_ctx:
        in_specs += [pl.BlockSpec((None, None, n_ctx, D_BRANCH), lambda b, i: (b, layer, 0, 0))] * 2
        args += list(cache)
    in_specs.append(pl.BlockSpec((1, DV), lambda b, i: (0, 0)))
    args.append(attn_g)
    return pl.pallas_call(
        functools.partial(_attn_kernel, n_ctx=n_ctx, lam_init=lam_init),
        grid=(n_seq, q_tiles),
        in_specs=in_specs,
        out_specs=pl.BlockSpec((Q_TILE, D_BRANCH), lambda b, i: (b * q_tiles + i, 0)),
        out_shape=jax.ShapeDtypeStruct((n_seq * seq, D_BRANCH), F32),
        scratch_shapes=[pltpu.VMEM((seq + n_ctx, D_BRANCH), BF16)] * 2,
        compiler_params=_cparams(("parallel", "arbitrary")),
        name="attn",
    )(*args)


def _pool_kernel(z_ref, w_ref, sc_ref, o_ref, pad_scr):
    seq = z_ref.shape[0]
    padded = seq + 2 * POOL_PAD
    z = z_ref[...]
    zeros = jnp.zeros((POOL_PAD, D_BRANCH), F32)
    pad_scr[0:POOL_PAD, :] = zeros
    pad_scr[POOL_PAD:POOL_PAD + seq, :] = z
    pad_scr[POOL_PAD + seq:padded, :] = zeros
    t = lax.broadcasted_iota(jnp.int32, (seq, POOL_GROUP), 0)
    outs = []
    for gi, w in enumerate(POOL_WINDOWS):
        lanes = slice(gi * POOL_GROUP, (gi + 1) * POOL_GROUP)
        run = pad_scr[:, lanes]
        span = 1
        while span < w:
            run = run + pltpu.roll(run, padded - span, 0)
            span *= 2
        tot = pltpu.roll(run, w // 2, 0)[POOL_PAD:POOL_PAD + seq]
        cnt = jnp.minimum(t + w // 2, seq) - jnp.maximum(t - w // 2, 0)
        pooled = tot / cnt.astype(F32) - z[:, lanes]
        outs.append(_dot(pooled.astype(BF16), w_ref[gi]))
    o_ref[...] = jnp.concatenate(outs, axis=1) * sc_ref[...]


def _pool(z, w_pool, pool_scale, row0, n_seq, seq):
    s0 = row0 // seq
    return pl.pallas_call(
        _pool_kernel,
        grid=(n_seq,),
        in_specs=[
            pl.BlockSpec((seq, D_BRANCH), lambda b: (s0 + b, 0)),
            pl.BlockSpec((len(POOL_WINDOWS), POOL_GROUP, POOL_GROUP), lambda b: (0, 0, 0)),
            pl.BlockSpec((1, D_BRANCH), lambda b: (0, 0)),
        ],
        out_specs=pl.BlockSpec((seq, D_BRANCH), lambda b: (b, 0)),
        out_shape=jax.ShapeDtypeStruct((n_seq * seq, D_BRANCH), F32),
        scratch_shapes=[pltpu.VMEM((seq + 2 * POOL_PAD, D_BRANCH), F32)],
        compiler_params=_cparams(("parallel",)),
        name="pool",
    )(z, w_pool, pool_scale)


def _merge_kernel(x_ref, pat_ref, g_ref, *refs, tiles_p):
    branch_refs, (wg_ref, wbr_ref, wo_ref, o_ref) = refs[:2 * N_BRANCH], refs[2 * N_BRANCH:]
    x = x_ref[...]
    prompt = pl.program_id(0) < tiles_p
    n = _norm_mod(x, g_ref[...], pat_ref[3:4], pat_ref[4:5]).astype(BF16)
    merged = None
    for br in range(N_BRANCH):
        y = jnp.where(prompt, branch_refs[2 * br][...], branch_refs[2 * br + 1][...]).astype(BF16)
        gate = jax.nn.sigmoid(_dot(n, wg_ref[:, br * D_MODEL:(br + 1) * D_MODEL]))
        part = gate * _dot(y, wbr_ref[br])
        merged = part if merged is None else merged + part
    o_ref[...] = x + pat_ref[5:6] * _dot(merged.astype(BF16), wo_ref[...])


def _merge(rows, x, pat, norm_g, branches, w_g, w_br, w_o, layer):
    tiles_p = rows.tiles_p
    p_spec = pl.BlockSpec((ROW_TILE, D_BRANCH), lambda i: (jnp.minimum(i, tiles_p - 1), 0))
    s_spec = pl.BlockSpec((ROW_TILE, D_BRANCH), lambda i: (jnp.maximum(i - tiles_p, 0), 0))
    return pl.pallas_call(
        functools.partial(_merge_kernel, tiles_p=tiles_p),
        grid=(rows.n_tok // ROW_TILE,),
        in_specs=[
            pl.BlockSpec((ROW_TILE, D_MODEL), lambda i: (i, 0)),
            pl.BlockSpec((None, N_MOD, D_MODEL), lambda i: (rows.mod_row(i), 0, 0)),
            pl.BlockSpec((1, D_MODEL), lambda i: (0, 0)),
            p_spec, s_spec, p_spec, s_spec, p_spec, s_spec,
            _resident((None, D_MODEL, N_BRANCH * D_MODEL), lambda i: (layer, 0, 0)),
            _resident((None, N_BRANCH, D_BRANCH, D_MODEL), lambda i: (layer, 0, 0, 0)),
            _resident((None, D_MODEL, D_MODEL), lambda i: (layer, 0, 0)),
        ],
        out_specs=pl.BlockSpec((ROW_TILE, D_MODEL), lambda i: (i, 0)),
        out_shape=jax.ShapeDtypeStruct((rows.n_tok, D_MODEL), F32),
        compiler_params=_cparams(("parallel",)),
        name="merge",
    )(x, pat, norm_g, *[y for pair in branches for y in pair], w_g, w_br, w_o)


def _final_norm_kernel(x_ref, g_ref, op_ref, os_ref, *, tiles_p):
    x = x_ref[...]
    ms = jnp.mean(x * x, axis=-1, keepdims=True)
    y = (x * lax.rsqrt(ms + EPS)) * g_ref[...]
    prompt = pl.program_id(0) < tiles_p

    @pl.when(prompt)
    def _():
        op_ref[...] = y

    @pl.when(jnp.logical_not(prompt))
    def _():
        os_ref[...] = y


def _final_norm(rows, x, g):
    tiles_p = rows.tiles_p
    return pl.pallas_call(
        functools.partial(_final_norm_kernel, tiles_p=tiles_p),
        grid=(rows.n_tok // ROW_TILE,),
        in_specs=[pl.BlockSpec((ROW_TILE, D_MODEL), lambda i: (i, 0)), pl.BlockSpec((1, D_MODEL), lambda i: (0, 0))],
        out_specs=[
            pl.BlockSpec((ROW_TILE, D_MODEL), lambda i: (jnp.minimum(i, tiles_p - 1), 0)),
            pl.BlockSpec((ROW_TILE, D_MODEL), lambda i: (jnp.maximum(i - tiles_p, 0), 0)),
        ],
        out_shape=[
            jax.ShapeDtypeStruct((rows.tok_p, D_MODEL), F32),
            jax.ShapeDtypeStruct((rows.tok_s, D_MODEL), F32),
        ],
        compiler_params=_cparams(("arbitrary",)),
        name="final_norm",
    )(x, g)


def _rope_tables(seq):
    n_rows = seq // GRID_W
    row = jnp.repeat(jnp.arange(n_rows, dtype=F32), GRID_W)
    col = jnp.tile(jnp.arange(GRID_W, dtype=F32), n_rows)
    n_freq = DH // 4
    inv = ROPE_BASE ** (-jnp.arange(n_freq, dtype=F32) / n_freq)
    ar = row[:, None] * inv
    ac = col[:, None] * inv
    cos = jnp.concatenate([jnp.cos(ar)] * 2 + [jnp.cos(ac)] * 2, axis=1)
    sin = jnp.concatenate([-jnp.sin(ar), jnp.sin(ar), -jnp.sin(ac), jnp.sin(ac)], axis=1)
    return jnp.tile(cos, (1, LANES // DH)), jnp.tile(sin, (1, LANES // DH))


def _to_time_major(x, n_seq):
    n_tok, d = x.shape
    return x.reshape(n_seq, n_tok // n_seq, d).transpose(1, 0, 2).reshape(n_tok, d)


def _from_time_major(x, n_seq):
    n_tok, d = x.shape
    return x.reshape(n_tok // n_seq, n_seq, d).transpose(1, 0, 2).reshape(n_tok, d)


def kernel(x_prompt, x_sample, cache_k, cache_v, state_ssm, c, c_ctx, norm_g, w_mod, b_mod, w_ffn_in, w_ffn_out, w_in, ssm_lam_re, ssm_lam_im, ssm_log_dt, ssm_b_re, ssm_b_im, ssm_c_re, ssm_c_im, ssm_d, w_glu, lam_q1, lam_k1, lam_q2, lam_k2, attn_norm_g, w_pool, pool_scale, w_branch, w_out, final_norm_g):
    n_p, seq_p, _ = x_prompt.shape
    n_s, seq_s, _ = x_sample.shape
    n_past = cache_k.shape[2]
    rows = _Rows(n_p, seq_p, n_s, seq_s)
    tok_p = rows.tok_p

    cvec = jnp.concatenate([c_ctx[None, :], c, jnp.zeros((16 - 1 - n_s, D_MODEL), F32)], axis=0)
    pat = _adaln(cvec, w_mod, b_mod).reshape(DEPTH, 16, N_MOD, D_MODEL)

    w_ab = w_ffn_in.astype(BF16)
    w_fo = w_ffn_out.astype(BF16)
    w_p = w_in[:, :, :N_PROJ * D_BRANCH].astype(BF16)
    w_g = w_in[:, :, N_PROJ * D_BRANCH:].astype(BF16)
    w_br = w_branch.astype(BF16)
    w_o = w_out.astype(BF16)
    w_gl = w_glu.astype(BF16)
    w_pl = w_pool.astype(BF16)

    x = jnp.concatenate([x_prompt.reshape(tok_p, D_MODEL), x_sample.reshape(rows.tok_s, D_MODEL)], axis=0)
    rope_tabs = _rope_tables(seq_s)
    cache = (cache_k.reshape(n_s, DEPTH, n_past, D_BRANCH), cache_v.reshape(n_s, DEPTH, n_past, D_BRANCH))
    h0_p = jnp.zeros((2, 2, n_p, N_STATE), F32)

    new_k, new_v, new_s = [], [], []
    for l in range(DEPTH):
        lam_init = 0.8 - 0.6 * math.exp(-0.3 * l)
        lam = (jnp.exp(jnp.sum(lam_q1[l] * lam_k1[l])) - jnp.exp(jnp.sum(lam_q2[l] * lam_k2[l])) + lam_init).reshape(1)
        bb, cc, a_re, a_im = _s5_params(ssm_lam_re[l], ssm_lam_im[l], ssm_log_dt[l], ssm_b_re[l], ssm_b_im[l],
                                        ssm_c_re[l], ssm_c_im[l])
        ssm = dict(bb=bb, cc=cc, a_re=a_re, a_im=a_im, d_skip=ssm_d[l][None, :], w_glu=w_gl[l])
        ng = norm_g[l]
        attn_g = attn_norm_g[l][None, :]
        scale_c = pool_scale[l][None, :]
        h0_s = state_ssm[:, l].reshape(n_s, 2, 2, N_STATE).transpose(1, 2, 0, 3)

        x = _ffn(rows, x, pat[l], ng[0:1], w_ab, w_fo, l, 0, 0)
        u, q, k, v, z = _inproj(rows, x, pat[l], ng[1:2], w_p, l, rope_tabs)

        ya_p, h_fin = _s5(_to_time_major(u[:tok_p], n_p), ssm, h0_p, n_p)
        ya_s, _ = _s5(_to_time_major(u[tok_p:], n_s), ssm, h0_s, n_s)
        ya = (_from_time_major(ya_p, n_p), _from_time_major(ya_s, n_s))
        yb = (_attention(q, k, v, lam, attn_g, 0, n_p, seq_p, lam_init),
              _attention(q, k, v, lam, attn_g, tok_p, n_s, seq_s, lam_init, cache, l))
        yc = (_pool(z, w_pl[l], scale_c, 0, n_p, seq_p), _pool(z, w_pl[l], scale_c, tok_p, n_s, seq_s))

        x = _merge(rows, x, pat[l], ng[1:2], (ya, yb, yc), w_g, w_br, w_o, l)
        x = _ffn(rows, x, pat[l], ng[2:3], w_ab, w_fo, l, 1, 6)

        new_k.append(k[:tok_p].reshape(n_p, seq_p, N_HEADS, 2, DH))
        new_v.append(v[:tok_p].reshape(n_p, seq_p, N_HEADS, DV))
        new_s.append(h_fin.transpose(2, 0, 1, 3).reshape(n_p, 2, 2, S5_GROUPS, S5_STATE))

    y_p, y_s = _final_norm(rows, x, final_norm_g[None, :])
    return (y_p.reshape(n_p, seq_p, D_MODEL), y_s.reshape(n_s, seq_s, D_MODEL),
            jnp.stack(new_k, axis=1), jnp.stack(new_v, axis=1), jnp.stack(new_s, axis=1))
```

```python
import functools
import math

import jax
import jax.numpy as jnp
from jax import lax
from jax.experimental import pallas as pl
from jax.experimental.pallas import tpu as pltpu

F32 = jnp.float32
BF16 = jnp.bfloat16

D_MODEL = 1024
DEPTH = 4
GRID_W = 64
D_BRANCH = 512
S5_GROUP = 16
S5_GROUPS = 32
S5_STATE = 64
N_STATE = S5_GROUPS * S5_STATE
DH = 64
N_HEADS = 4
DV = 128
POOL_WINDOWS = (2, 4, 8, 16)
POOL_GROUP = 128
POOL_PAD = 16
D_FF = 2816
N_MOD = 9
N_BRANCH = 3
N_PROJ = 5
ROPE_BASE = 10000.0
EPS = 1e-6

LANES = 128
N_SLAB = D_BRANCH // LANES
SLAB_STATE = N_STATE // N_SLAB
ROW_TILE = 512
FF_SPLIT = 2
S5_CHUNK_ROWS = 512
Q_TILE = 256
VMEM_LIMIT = 56 * 1024 * 1024


def _cparams(sem):
    return pltpu.CompilerParams(dimension_semantics=sem, vmem_limit_bytes=VMEM_LIMIT)


def _resident(block_shape, index_map):
    return pl.BlockSpec(block_shape, index_map, pipeline_mode=pl.Buffered(1))


def _norm_mod(x, g, shift, scale):
    ms = jnp.mean(x * x, axis=-1, keepdims=True)
    y = (x * lax.rsqrt(ms + EPS)) * g
    return y * (1.0 + scale) + shift


def _dot(a, b):
    return jnp.dot(a, b, preferred_element_type=F32)


class _Rows:
    def __init__(self, n_p, seq_p, n_s, seq_s):
        self.n_p, self.seq_p, self.n_s, self.seq_s = n_p, seq_p, n_s, seq_s
        self.tok_p = n_p * seq_p
        self.tok_s = n_s * seq_s
        self.n_tok = self.tok_p + self.tok_s
        assert self.tok_p % ROW_TILE == 0 and seq_s % ROW_TILE == 0
        self.tiles_p = self.tok_p // ROW_TILE

    def mod_row(self, i):
        return jnp.where(i < self.tiles_p, 0, 1 + (i - self.tiles_p) // (self.seq_s // ROW_TILE))


def _adaln_kernel(c_ref, w_ref, b_ref, o_ref):
    c = c_ref[...]
    s = (c * jax.nn.sigmoid(c)).astype(BF16)
    o_ref[...] = _dot(s, w_ref[...].astype(BF16)) + b_ref[...]


def _adaln(cvec, w_mod, b_mod):
    tn = 1024
    n_rows = cvec.shape[0]
    return pl.pallas_call(
        _adaln_kernel,
        grid=(DEPTH, N_MOD * D_MODEL // tn),
        in_specs=[
            pl.BlockSpec((n_rows, D_MODEL), lambda l, j: (0, 0)),
            pl.BlockSpec((None, D_MODEL, tn), lambda l, j: (l, 0, j)),
            pl.BlockSpec((None, 1, tn), lambda l, j: (l, 0, j)),
        ],
        out_specs=pl.BlockSpec((None, n_rows, tn), lambda l, j: (l, 0, j)),
        out_shape=jax.ShapeDtypeStruct((DEPTH, n_rows, N_MOD * D_MODEL), F32),
        compiler_params=_cparams(("parallel", "parallel")),
        name="adaln",
    )(cvec, w_mod, b_mod.reshape(DEPTH, 1, N_MOD * D_MODEL))


def _ffn_kernel(*refs, k0, tiles_p, split_in, final):
    refs = list(refs)
    prompt = pl.program_id(0) < tiles_p
    if split_in:
        xp_ref, xs_ref = refs[:2]
        refs = refs[2:]
        x = jnp.where(prompt, xp_ref[...], xs_ref[...])
    else:
        x = refs.pop(0)[...]
    pat_ref, g_ref, wa_ref, wb_ref, wo_ref = refs[:5]
    refs = refs[5:]
    n = _norm_mod(x, g_ref[...], pat_ref[k0:k0 + 1], pat_ref[k0 + 1:k0 + 2]).astype(BF16)
    piece = D_FF // FF_SPLIT
    y = None
    for h in range(FF_SPLIT):
        cols = slice(h * piece, (h + 1) * piece)
        a = _dot(n, wa_ref[:, cols])
        b = _dot(n, wb_ref[:, cols])
        act = (a * jax.nn.sigmoid(a) * b).astype(BF16)
        part = _dot(act, wo_ref[cols, :])
        y = part if y is None else y + part
    out = x + (0.5 * pat_ref[k0 + 2:k0 + 3]) * y
    if not final:
        refs[0][...] = out
        return
    fg_ref, op_ref, os_ref = refs
    ms = jnp.mean(out * out, axis=-1, keepdims=True)
    out = (out * lax.rsqrt(ms + EPS)) * fg_ref[...]

    @pl.when(prompt)
    def _():
        op_ref[...] = out

    @pl.when(jnp.logical_not(prompt))
    def _():
        os_ref[...] = out


def _ffn(rows, x, pat, norm_g, w_ab, w_o, layer, sub, k0, final_g=None):
    split_in = isinstance(x, tuple)
    final = final_g is not None
    tile = (ROW_TILE, D_MODEL)
    p_spec = pl.BlockSpec(tile, lambda i: (jnp.minimum(i, rows.tiles_p - 1), 0))
    s_spec = pl.BlockSpec(tile, lambda i: (jnp.maximum(i - rows.tiles_p, 0), 0))
    all_spec = pl.BlockSpec(tile, lambda i: (i, 0))
    in_specs = ([p_spec, s_spec] if split_in else [all_spec]) + [
        pl.BlockSpec((None, N_MOD, D_MODEL), lambda i: (rows.mod_row(i), 0, 0)),
        pl.BlockSpec((1, D_MODEL), lambda i: (0, 0)),
        _resident((None, None, D_MODEL, D_FF), lambda i: (layer, sub, 0, 0)),
        _resident((None, None, D_MODEL, D_FF), lambda i: (layer, sub, 0, 1)),
        _resident((None, None, D_FF, D_MODEL), lambda i: (layer, sub, 0, 0)),
    ]
    args = (list(x) if split_in else [x]) + [pat, norm_g, w_ab, w_ab, w_o]
    if final:
        in_specs.append(pl.BlockSpec((1, D_MODEL), lambda i: (0, 0)))
        args.append(final_g)
        out_specs = [p_spec, s_spec]
        out_shape = [jax.ShapeDtypeStruct((rows.tok_p, D_MODEL), F32), jax.ShapeDtypeStruct((rows.tok_s, D_MODEL), F32)]
    else:
        out_specs = all_spec
        out_shape = jax.ShapeDtypeStruct((rows.n_tok, D_MODEL), F32)
    return pl.pallas_call(
        functools.partial(_ffn_kernel, k0=k0, tiles_p=rows.tiles_p, split_in=split_in, final=final),
        grid=(rows.n_tok // ROW_TILE,),
        in_specs=in_specs,
        out_specs=out_specs,
        out_shape=out_shape,
        compiler_params=_cparams(("arbitrary",) if final else ("parallel",)),
        name="ffn",
    )(*args)


def _rope(x, cos, sin):
    lane = lax.broadcasted_iota(jnp.int32, (x.shape[0], LANES), 1)
    first_half = (lane % 32) < 16
    out = []
    for c in range(x.shape[1] // LANES):
        xc = x[:, c * LANES:(c + 1) * LANES]
        partner = jnp.where(first_half, pltpu.roll(xc, LANES - 16, 1), pltpu.roll(xc, 16, 1))
        out.append(xc * cos + partner * sin)
    return jnp.concatenate(out, axis=1)


def _inproj_kernel(*refs, tiles_p, n_alias):
    x_ref, pat_ref, g_ref, w_ref, cos_ref, sin_ref = refs[:6]
    up_ref, us_ref, q_ref, ks_ref, vs_ref, kc_ref, vc_ref, z_ref = refs[6 + n_alias:]
    n = _norm_mod(x_ref[...], g_ref[...], pat_ref[3:4], pat_ref[4:5]).astype(BF16)
    y = _dot(n, w_ref[...])
    col = lambda c: y[:, c * D_BRANCH:(c + 1) * D_BRANCH]
    z_ref[...] = col(4)
    latent = pl.program_id(0) >= tiles_p
    scale = DH ** -0.5

    @pl.when(latent)
    def _():
        us_ref[...] = col(0)
        q_ref[...] = (_rope(col(1), cos_ref[...], sin_ref[...]) * scale).astype(BF16)
        ks_ref[...] = _rope(col(2), cos_ref[...], sin_ref[...]).astype(BF16)
        vs_ref[...] = col(3).astype(BF16)

    @pl.when(jnp.logical_not(latent))
    def _():
        up_ref[...] = col(0)
        q_ref[...] = (col(1) * scale).astype(BF16)
        kc_ref[...] = col(2).reshape(kc_ref.shape)
        vc_ref[...] = col(3).reshape(vc_ref.shape)


def _inproj(rows, x, pat, norm_g, w_p, layer, rope_tabs, cache_out):
    tiles_p = rows.tiles_p
    seqs_per_tile = ROW_TILE // rows.seq_p
    tabs_per_seq = rows.seq_s // ROW_TILE
    tab_spec = pl.BlockSpec((ROW_TILE, LANES), lambda i: (jnp.maximum(i - tiles_p, 0) % tabs_per_seq, 0))
    tile = (ROW_TILE, D_BRANCH)
    p_spec = pl.BlockSpec(tile, lambda i: (jnp.minimum(i, tiles_p - 1), 0))
    s_spec = pl.BlockSpec(tile, lambda i: (jnp.maximum(i - tiles_p, 0), 0))
    all_spec = pl.BlockSpec(tile, lambda i: (i, 0))
    cache_spec = pl.BlockSpec((seqs_per_tile, None, rows.seq_p, D_BRANCH),
                              lambda i: (jnp.minimum(i, tiles_p - 1), layer, 0, 0))
    cache_shape = jax.ShapeDtypeStruct((rows.n_p, DEPTH, rows.seq_p, D_BRANCH), F32)
    in_specs = [
        pl.BlockSpec((ROW_TILE, D_MODEL), lambda i: (i, 0)),
        pl.BlockSpec((None, N_MOD, D_MODEL), lambda i: (rows.mod_row(i), 0, 0)),
        pl.BlockSpec((1, D_MODEL), lambda i: (0, 0)),
        _resident((None, D_MODEL, N_PROJ * D_BRANCH), lambda i: (layer, 0, 0)),
        tab_spec, tab_spec,
    ]
    args = [x, pat, norm_g, w_p, *rope_tabs]
    aliases = {}
    if cache_out is not None:
        in_specs += [pl.BlockSpec(memory_space=pl.ANY)] * 2
        aliases = {len(args): 5, len(args) + 1: 6}
        args += list(cache_out)
    return pl.pallas_call(
        functools.partial(_inproj_kernel, tiles_p=tiles_p, n_alias=len(aliases)),
        grid=(rows.n_tok // ROW_TILE,),
        in_specs=in_specs,
        out_specs=[p_spec, s_spec, all_spec, s_spec, s_spec, cache_spec, cache_spec, all_spec],
        out_shape=[
            jax.ShapeDtypeStruct((rows.tok_p, D_BRANCH), F32),
            jax.ShapeDtypeStruct((rows.tok_s, D_BRANCH), F32),
            jax.ShapeDtypeStruct((rows.n_tok, D_BRANCH), BF16),
            jax.ShapeDtypeStruct((rows.tok_s, D_BRANCH), BF16),
            jax.ShapeDtypeStruct((rows.tok_s, D_BRANCH), BF16),
            cache_shape, cache_shape,
            jax.ShapeDtypeStruct((rows.n_tok, D_BRANCH), F32),
        ],
        input_output_aliases=aliases,
        compiler_params=_cparams(("arbitrary",)),
        name="inproj",
    )(*args)


def _s5_kernel(*refs, n_seq, n_steps, backward):
    if backward:
        u_ref, part_ref, bb_ref, cc_ref, ar_ref, ai_ref, h0_ref, wglu_ref, ya_ref, hfin_ref, xs_scr, h_scr = refs
    else:
        u_ref, dsk_ref, bb_ref, cc_ref, ar_ref, ai_ref, h0_ref, part_ref, hfin_ref, xs_scr, h_scr = refs
    rows = n_seq * n_steps

    @pl.when(pl.program_id(0) == 0)
    def _():
        h_scr[...] = h0_ref[...]

    u = jnp.swapaxes(u_ref[...], 0, 1).reshape(rows, D_BRANCH)
    ub = u.astype(BF16)
    order = range(n_steps - 1, -1, -1) if backward else range(n_steps)

    def project(j):
        xs_scr[j] = _dot(ub[:, j * LANES:(j + 1) * LANES], bb_ref[j])

    def scan(j):
        lanes = slice(j * SLAB_STATE, (j + 1) * SLAB_STATE)
        ar = jnp.broadcast_to(ar_ref[:, lanes], (n_seq, SLAB_STATE))
        ai = jnp.broadcast_to(ai_ref[:, lanes], (n_seq, SLAB_STATE))
        hr = h_scr[0, :, lanes]
        hi = h_scr[1, :, lanes]
        for t in order:
            r = slice(t * n_seq, (t + 1) * n_seq)
            xr = xs_scr[j, r, 0:SLAB_STATE]
            xi = xs_scr[j, r, SLAB_STATE:2 * SLAB_STATE]
            hr, hi = ar * hr - ai * hi + xr, ar * hi + ai * hr + xi
            xs_scr[j, r, 0:SLAB_STATE] = hr
            xs_scr[j, r, SLAB_STATE:2 * SLAB_STATE] = hi
        h_scr[0, :, lanes] = hr
        h_scr[1, :, lanes] = hi

    ys = []
    project(0)
    for j in range(N_SLAB):
        if j + 1 < N_SLAB:
            project(j + 1)
        scan(j)
        ys.append(_dot(xs_scr[j].astype(BF16), cc_ref[j]))
    y = jnp.concatenate(ys, axis=1)

    if backward:
        g = jax.nn.gelu(part_ref[...] + y)
        ya = g * jax.nn.sigmoid(_dot(g.astype(BF16), wglu_ref[...]))
        ya_ref[...] = jnp.swapaxes(ya.reshape(n_steps, n_seq, D_BRANCH), 0, 1)
    else:
        part_ref[...] = y + dsk_ref[...] * u

    @pl.when(pl.program_id(0) == pl.num_programs(0) - 1)
    def _():
        hfin_ref[...] = h_scr[...]


def _s5(u, ssm, h0, n_seq):
    n_tok = u.shape[0]
    seq = n_tok // n_seq
    n_steps = S5_CHUNK_ROWS // n_seq
    n_chunks = seq // n_steps
    u3 = u.reshape(n_seq, seq, D_BRANCH)

    def specs(backward):
        d = int(backward)
        chunk = (lambda s: n_chunks - 1 - s) if backward else (lambda s: s)
        seq_block = pl.BlockSpec((n_seq, n_steps, D_BRANCH), lambda s: (0, chunk(s), 0))
        part_block = pl.BlockSpec((S5_CHUNK_ROWS, D_BRANCH), lambda s: (chunk(s), 0))
        params = [
            pl.BlockSpec((None, N_SLAB, LANES, 2 * SLAB_STATE), lambda s: (d, 0, 0, 0)),
            pl.BlockSpec((None, N_SLAB, 2 * SLAB_STATE, LANES), lambda s: (d, 0, 0, 0)),
            pl.BlockSpec((None, 1, N_STATE), lambda s: (d, 0, 0)),
            pl.BlockSpec((None, 1, N_STATE), lambda s: (d, 0, 0)),
            pl.BlockSpec((None, 2, n_seq, N_STATE), lambda s: (d, 0, 0, 0)),
        ]
        return seq_block, part_block, params

    param_args = (ssm["bb"], ssm["cc"], ssm["a_re"], ssm["a_im"], h0)
    fin_spec = pl.BlockSpec((2, n_seq, N_STATE), lambda s: (0, 0, 0))
    fin_shape = jax.ShapeDtypeStruct((2, n_seq, N_STATE), F32)
    scratch = [pltpu.VMEM((N_SLAB, S5_CHUNK_ROWS, 2 * SLAB_STATE), F32), pltpu.VMEM((2, n_seq, N_STATE), F32)]

    seq_block, part_block, params = specs(False)
    part, fin_f = pl.pallas_call(
        functools.partial(_s5_kernel, n_seq=n_seq, n_steps=n_steps, backward=False),
        grid=(n_chunks,),
        in_specs=[seq_block, pl.BlockSpec((1, D_BRANCH), lambda s: (0, 0))] + params,
        out_specs=[part_block, fin_spec],
        out_shape=[jax.ShapeDtypeStruct((n_tok, D_BRANCH), F32), fin_shape],
        scratch_shapes=scratch,
        compiler_params=_cparams(("arbitrary",)),
        name="s5_fwd",
    )(u3, ssm["d_skip"], *param_args)

    seq_block, part_block, params = specs(True)
    ya, fin_b = pl.pallas_call(
        functools.partial(_s5_kernel, n_seq=n_seq, n_steps=n_steps, backward=True),
        grid=(n_chunks,),
        in_specs=[seq_block, part_block] + params + [pl.BlockSpec((D_BRANCH, D_BRANCH), lambda s: (0, 0))],
        out_specs=[seq_block, fin_spec],
        out_shape=[jax.ShapeDtypeStruct((n_seq, seq, D_BRANCH), F32), fin_shape],
        scratch_shapes=scratch,
        compiler_params=_cparams(("arbitrary",)),
        name="s5_bwd",
    )(u3, part, *param_args, ssm["w_glu"])
    return ya.reshape(n_tok, D_BRANCH), jnp.stack([fin_f, fin_b], axis=0)


def _s5_params(lam_re, lam_im, log_dt, b_re, b_im, c_re, c_im):
    dt = jnp.exp(log_dt)[..., None]
    mag = jnp.exp(lam_re * dt)
    abr = mag * jnp.cos(lam_im * dt)
    abi = mag * jnp.sin(lam_im * dt)
    den = lam_re * lam_re + lam_im * lam_im
    nr = abr - 1.0
    kr = (nr * lam_re + abi * lam_im) / den
    ki = (abi * lam_re - nr * lam_im) / den
    bbr = kr[..., None] * b_re - ki[..., None] * b_im
    bbi = kr[..., None] * b_im + ki[..., None] * b_re
    per_slab = S5_GROUPS // N_SLAB
    eye = jnp.eye(per_slab, dtype=F32)

    def pack_in(w):
        w = w.reshape(2, N_SLAB, per_slab, S5_STATE, S5_GROUP)
        return jnp.einsum("djgpc,gh->djgchp", w, eye).reshape(2, N_SLAB, LANES, SLAB_STATE)

    def pack_out(w):
        w = w.reshape(2, N_SLAB, per_slab, S5_GROUP, S5_STATE)
        return jnp.einsum("djgcp,gh->djgphc", w, eye).reshape(2, N_SLAB, SLAB_STATE, LANES)

    bb = jnp.concatenate([pack_in(bbr), pack_in(bbi)], axis=-1).astype(BF16)
    cc = jnp.concatenate([pack_out(c_re), -pack_out(c_im)], axis=-2).astype(BF16)
    return bb, cc, abr.reshape(2, 1, N_STATE), abi.reshape(2, 1, N_STATE)


def _attn_kernel(*refs, n_ctx, lam_init):
    if n_ctx:
        lam_ref, q_ref, k_ref, v_ref, ck_ref, cv_ref, g_ref, o_ref, k_scr, v_scr = refs
    else:
        lam_ref, q_ref, k_ref, v_ref, g_ref, o_ref, k_scr, v_scr = refs
    n_own = k_ref.shape[0]

    @pl.when(pl.program_id(1) == 0)
    def _():
        k_scr[0:n_own, :] = k_ref[...].astype(BF16)
        v_scr[0:n_own, :] = v_ref[...].astype(BF16)
        if n_ctx:
            k_scr[n_own:n_own + n_ctx, :] = ck_ref[...].astype(BF16)
            v_scr[n_own:n_own + n_ctx, :] = cv_ref[...].astype(BF16)

    lam = lam_ref[0]
    q = q_ref[...]
    lane = lax.broadcasted_iota(jnp.int32, (q.shape[0], LANES), 1)
    for h in range(N_HEADS):
        lanes = slice(h * LANES, (h + 1) * LANES)
        qh = q[:, lanes]
        kh = k_scr[:, lanes]
        probs = []
        for m in range(2):
            qm = jnp.where((lane < DH) if m == 0 else (lane >= DH), qh, jnp.zeros_like(qh))
            sc = lax.dot_general(qm, kh, (((1,), (1,)), ((), ())), preferred_element_type=F32)
            e = jnp.exp(sc - jnp.max(sc, axis=-1, keepdims=True))
            probs.append((e, 1.0 / jnp.sum(e, axis=-1, keepdims=True)))
        (e1, r1), (e2, r2) = probs
        a = (e1 * r1 - e2 * (lam * r2)).astype(BF16)
        o = _dot(a, v_scr[:, lanes])
        ms = jnp.mean(o * o, axis=-1, keepdims=True)
        o_ref[:, lanes] = ((o * lax.rsqrt(ms + EPS)) * g_ref[...]) * (1.0 - lam_init)


def _attention(q, k, v, lam, attn_g, row0, n_seq, seq, lam_init, layer, cache=None):
    q_tiles = seq // Q_TILE
    q0 = row0 // Q_TILE
    n_ctx = 0 if cache is None else cache[0].shape[2]
    if k.ndim == 4:
        own = pl.BlockSpec((None, None, seq, D_BRANCH), lambda b, i: (b, layer, 0, 0))
    else:
        own = pl.BlockSpec((seq, D_BRANCH), lambda b, i: (b, 0))
    in_specs = [
        pl.BlockSpec(memory_space=pltpu.SMEM),
        pl.BlockSpec((Q_TILE, D_BRANCH), lambda b, i: (q0 + b * q_tiles + i, 0)),
        own, own,
    ]
    args = [lam, q, k, v]
    if n_ctx:
        in_specs += [pl.BlockSpec((None, None, n_ctx, D_BRANCH), lambda b, i: (b, layer, 0, 0))] * 2
        args += list(cache)
    in_specs.append(pl.BlockSpec((1, DV), lambda b, i: (0, 0)))
    args.append(attn_g)
    return pl.pallas_call(
        functools.partial(_attn_kernel, n_ctx=n_ctx, lam_init=lam_init),
        grid=(n_seq, q_tiles),
        in_specs=in_specs,
        out_specs=pl.BlockSpec((Q_TILE, D_BRANCH), lambda b, i: (b * q_tiles + i, 0)),
        out_shape=jax.ShapeDtypeStruct((n_seq * seq, D_BRANCH), F32),
        scratch_shapes=[pltpu.VMEM((seq + n_ctx, D_BRANCH), BF16)] * 2,
        compiler_params=_cparams(("parallel", "arbitrary")),
        name="attn",
    )(*args)


def _pool_kernel(z_ref, w_ref, sc_ref, o_ref, pad_scr):
    seq = z_ref.shape[0]
    padded = seq + 2 * POOL_PAD
    z = z_ref[...]
    zeros = jnp.zeros((POOL_PAD, D_BRANCH), F32)
    pad_scr[0:POOL_PAD, :] = zeros
    pad_scr[POOL_PAD:POOL_PAD + seq, :] = z
    pad_scr[POOL_PAD + seq:padded, :] = zeros
    t = lax.broadcasted_iota(jnp.int32, (seq, POOL_GROUP), 0)
    outs = []
    for gi, w in enumerate(POOL_WINDOWS):
        lanes = slice(gi * POOL_GROUP, (gi + 1) * POOL_GROUP)
        run = pad_scr[:, lanes]
        span = 1
        while span < w:
            run = run + pltpu.roll(run, padded - span, 0)
            span *= 2
        tot = pltpu.roll(run, w // 2, 0)[POOL_PAD:POOL_PAD + seq]
        cnt = jnp.minimum(t + w // 2, seq) - jnp.maximum(t - w // 2, 0)
        pooled = tot / cnt.astype(F32) - z[:, lanes]
        outs.append(_dot(pooled.astype(BF16), w_ref[gi]))
    o_ref[...] = jnp.concatenate(outs, axis=1) * sc_ref[...]


def _pool(z, w_pool, pool_scale, row0, n_seq, seq):
    s0 = row0 // seq
    return pl.pallas_call(
        _pool_kernel,
        grid=(n_seq,),
        in_specs=[
            pl.BlockSpec((seq, D_BRANCH), lambda b: (s0 + b, 0)),
            pl.BlockSpec((len(POOL_WINDOWS), POOL_GROUP, POOL_GROUP), lambda b: (0, 0, 0)),
            pl.BlockSpec((1, D_BRANCH), lambda b: (0, 0)),
        ],
        out_specs=pl.BlockSpec((seq, D_BRANCH), lambda b: (b, 0)),
        out_shape=jax.ShapeDtypeStruct((n_seq * seq, D_BRANCH), F32),
        scratch_shapes=[pltpu.VMEM((seq + 2 * POOL_PAD, D_BRANCH), F32)],
        compiler_params=_cparams(("parallel",)),
        name="pool",
    )(z, w_pool, pool_scale)


def _merge_kernel(x_ref, pat_ref, g_ref, *refs, tiles_p):
    branch_refs, (wg_ref, wbr_ref, wo_ref, o_ref) = refs[:2 * N_BRANCH], refs[2 * N_BRANCH:]
    x = x_ref[...]
    prompt = pl.program_id(0) < tiles_p
    n = _norm_mod(x, g_ref[...], pat_ref[3:4], pat_ref[4:5]).astype(BF16)
    merged = None
    for br in range(N_BRANCH):
        y = jnp.where(prompt, branch_refs[2 * br][...], branch_refs[2 * br + 1][...]).astype(BF16)
        gate = jax.nn.sigmoid(_dot(n, wg_ref[:, br * D_MODEL:(br + 1) * D_MODEL]))
        part = gate * _dot(y, wbr_ref[br])
        merged = part if merged is None else merged + part
    o_ref[...] = x + pat_ref[5:6] * _dot(merged.astype(BF16), wo_ref[...])


def _merge(rows, x, pat, norm_g, branches, w_g, w_br, w_o, layer):
    tiles_p = rows.tiles_p
    p_spec = pl.BlockSpec((ROW_TILE, D_BRANCH), lambda i: (jnp.minimum(i, tiles_p - 1), 0))
    s_spec = pl.BlockSpec((ROW_TILE, D_BRANCH), lambda i: (jnp.maximum(i - tiles_p, 0), 0))
    return pl.pallas_call(
        functools.partial(_merge_kernel, tiles_p=tiles_p),
        grid=(rows.n_tok // ROW_TILE,),
        in_specs=[
            pl.BlockSpec((ROW_TILE, D_MODEL), lambda i: (i, 0)),
            pl.BlockSpec((None, N_MOD, D_MODEL), lambda i: (rows.mod_row(i), 0, 0)),
            pl.BlockSpec((1, D_MODEL), lambda i: (0, 0)),
            p_spec, s_spec, p_spec, s_spec, p_spec, s_spec,
            _resident((None, D_MODEL, N_BRANCH * D_MODEL), lambda i: (layer, 0, 0)),
            _resident((None, N_BRANCH, D_BRANCH, D_MODEL), lambda i: (layer, 0, 0, 0)),
            _resident((None, D_MODEL, D_MODEL), lambda i: (layer, 0, 0)),
        ],
        out_specs=pl.BlockSpec((ROW_TILE, D_MODEL), lambda i: (i, 0)),
        out_shape=jax.ShapeDtypeStruct((rows.n_tok, D_MODEL), F32),
        compiler_params=_cparams(("parallel",)),
        name="merge",
    )(x, pat, norm_g, *[y for pair in branches for y in pair], w_g, w_br, w_o)


def _rope_tables(seq):
    n_rows = seq // GRID_W
    row = jnp.repeat(jnp.arange(n_rows, dtype=F32), GRID_W)
    col = jnp.tile(jnp.arange(GRID_W, dtype=F32), n_rows)
    n_freq = DH // 4
    inv = ROPE_BASE ** (-jnp.arange(n_freq, dtype=F32) / n_freq)
    ar = row[:, None] * inv
    ac = col[:, None] * inv
    cos = jnp.concatenate([jnp.cos(ar)] * 2 + [jnp.cos(ac)] * 2, axis=1)
    sin = jnp.concatenate([-jnp.sin(ar), jnp.sin(ar), -jnp.sin(ac), jnp.sin(ac)], axis=1)
    return jnp.tile(cos, (1, LANES // DH)), jnp.tile(sin, (1, LANES // DH))


def kernel(x_prompt, x_sample, cache_k, cache_v, state_ssm, c, c_ctx, norm_g, w_mod, b_mod, w_ffn_in, w_ffn_out, w_in, ssm_lam_re, ssm_lam_im, ssm_log_dt, ssm_b_re, ssm_b_im, ssm_c_re, ssm_c_im, ssm_d, w_glu, lam_q1, lam_k1, lam_q2, lam_k2, attn_norm_g, w_pool, pool_scale, w_branch, w_out, final_norm_g):
    n_p, seq_p, _ = x_prompt.shape
    n_s, seq_s, _ = x_sample.shape
    n_past = cache_k.shape[2]
    rows = _Rows(n_p, seq_p, n_s, seq_s)
    tok_p = rows.tok_p

    cvec = jnp.concatenate([c_ctx[None, :], c, jnp.zeros((16 - 1 - n_s, D_MODEL), F32)], axis=0)
    pat = _adaln(cvec, w_mod, b_mod).reshape(DEPTH, 16, N_MOD, D_MODEL)

    w_ab = w_ffn_in.astype(BF16)
    w_fo = w_ffn_out.astype(BF16)
    w_p = w_in[:, :, :N_PROJ * D_BRANCH].astype(BF16)
    w_g = w_in[:, :, N_PROJ * D_BRANCH:].astype(BF16)
    w_br = w_branch.astype(BF16)
    w_o = w_out.astype(BF16)
    w_gl = w_glu.astype(BF16)
    w_pl = w_pool.astype(BF16)

    x = (x_prompt.reshape(tok_p, D_MODEL), x_sample.reshape(rows.tok_s, D_MODEL))
    rope_tabs = _rope_tables(seq_s)
    cache = (cache_k.reshape(n_s, DEPTH, n_past, D_BRANCH), cache_v.reshape(n_s, DEPTH, n_past, D_BRANCH))
    h0_p = jnp.zeros((2, 2, n_p, N_STATE), F32)

    new_cache, new_s = None, []
    for l in range(DEPTH):
        lam_init = 0.8 - 0.6 * math.exp(-0.3 * l)
        lam = (jnp.exp(jnp.sum(lam_q1[l] * lam_k1[l])) - jnp.exp(jnp.sum(lam_q2[l] * lam_k2[l])) + lam_init).reshape(1)
        bb, cc, a_re, a_im = _s5_params(ssm_lam_re[l], ssm_lam_im[l], ssm_log_dt[l], ssm_b_re[l], ssm_b_im[l],
                                        ssm_c_re[l], ssm_c_im[l])
        ssm = dict(bb=bb, cc=cc, a_re=a_re, a_im=a_im, d_skip=ssm_d[l][None, :], w_glu=w_gl[l])
        ng = norm_g[l]
        attn_g = attn_norm_g[l][None, :]
        scale_c = pool_scale[l][None, :]
        h0_s = state_ssm[:, l].reshape(n_s, 2, 2, N_STATE).transpose(1, 2, 0, 3)

        x = _ffn(rows, x, pat[l], ng[0:1], w_ab, w_fo, l, 0, 0)
        u_p, u_s, q, k_s, v_s, k_new, v_new, z = _inproj(rows, x, pat[l], ng[1:2], w_p, l, rope_tabs, new_cache)
        new_cache = (k_new, v_new)

        ya_p, h_fin = _s5(u_p, ssm, h0_p, n_p)
        ya_s, _ = _s5(u_s, ssm, h0_s, n_s)
        yb = (_attention(q, k_new, v_new, lam, attn_g, 0, n_p, seq_p, lam_init, l),
              _attention(q, k_s, v_s, lam, attn_g, tok_p, n_s, seq_s, lam_init, l, cache))
        yc = (_pool(z, w_pl[l], scale_c, 0, n_p, seq_p), _pool(z, w_pl[l], scale_c, tok_p, n_s, seq_s))

        x = _merge(rows, x, pat[l], ng[1:2], ((ya_p, ya_s), yb, yc), w_g, w_br, w_o, l)
        last = l == DEPTH - 1
        x = _ffn(rows, x, pat[l], ng[2:3], w_ab, w_fo, l, 1, 6, final_norm_g[None, :] if last else None)
        new_s.append(h_fin.transpose(2, 0, 1, 3).reshape(n_p, 2, 2, S5_GROUPS, S5_STATE))

    y_p, y_s = x
    return (y_p.reshape(n_p, seq_p, D_MODEL), y_s.reshape(n_s, seq_s, D_MODEL),
            new_cache[0].reshape(n_p, DEPTH, seq_p, N_HEADS, 2, DH), new_cache[1].reshape(n_p, DEPTH, seq_p, N_HEADS, DV),
            jnp.stack(new_s, axis=1))
```

```python
import functools
import math

import jax
import jax.numpy as jnp
from jax import lax
from jax.experimental import pallas as pl
from jax.experimental.pallas import tpu as pltpu

F32 = jnp.float32
BF16 = jnp.bfloat16

D_MODEL = 1024
DEPTH = 4
GRID_W = 64
D_BRANCH = 512
S5_GROUP = 16
S5_GROUPS = 32
S5_STATE = 64
N_STATE = S5_GROUPS * S5_STATE
DH = 64
N_HEADS = 4
DV = 128
POOL_WINDOWS = (2, 4, 8, 16)
POOL_GROUP = 128
POOL_PAD = 16
D_FF = 2816
N_MOD = 9
N_BRANCH = 3
N_PROJ = 5
ROPE_BASE = 10000.0
EPS = 1e-6

LANES = 128
N_SLAB = D_BRANCH // LANES
SLAB_STATE = N_STATE // N_SLAB
ROW_TILE = 512
FF_CHUNK = 256
S5_CHUNK_ROWS = 512
Q_TILE = 512
VMEM_LIMIT = 56 * 1024 * 1024


def _cparams(sem):
    return pltpu.CompilerParams(dimension_semantics=sem, vmem_limit_bytes=VMEM_LIMIT)


def _resident(block_shape, index_map):
    return pl.BlockSpec(block_shape, index_map, pipeline_mode=pl.Buffered(1))


def _norm_mod(x, g, shift, scale):
    ms = jnp.mean(x * x, axis=-1, keepdims=True)
    y = (x * lax.rsqrt(ms + EPS)) * g
    return y * (1.0 + scale) + shift


def _dot(a, b):
    return jnp.dot(a, b, preferred_element_type=F32)


class _Rows:
    def __init__(self, n_p, seq_p, n_s, seq_s):
        self.n_p, self.seq_p, self.n_s, self.seq_s = n_p, seq_p, n_s, seq_s
        self.tok_p = n_p * seq_p
        self.tok_s = n_s * seq_s
        self.n_tok = self.tok_p + self.tok_s
        assert self.tok_p % ROW_TILE == 0 and seq_s % ROW_TILE == 0
        self.tiles_p = self.tok_p // ROW_TILE

    def mod_row(self, i):
        return jnp.where(i < self.tiles_p, 0, 1 + (i - self.tiles_p) // (self.seq_s // ROW_TILE))


def _adaln_kernel(c_ref, w_ref, b_ref, o_ref):
    c = c_ref[...]
    s = (c * jax.nn.sigmoid(c)).astype(BF16)
    o_ref[...] = _dot(s, w_ref[...].astype(BF16)) + b_ref[...]


def _adaln(cvec, w_mod, b_mod):
    tn = 1024
    n_rows = cvec.shape[0]
    return pl.pallas_call(
        _adaln_kernel,
        grid=(DEPTH, N_MOD * D_MODEL // tn),
        in_specs=[
            pl.BlockSpec((n_rows, D_MODEL), lambda l, j: (0, 0)),
            pl.BlockSpec((None, D_MODEL, tn), lambda l, j: (l, 0, j)),
            pl.BlockSpec((None, 1, tn), lambda l, j: (l, 0, j)),
        ],
        out_specs=pl.BlockSpec((None, n_rows, tn), lambda l, j: (l, 0, j)),
        out_shape=jax.ShapeDtypeStruct((DEPTH, n_rows, N_MOD * D_MODEL), F32),
        compiler_params=_cparams(("parallel", "parallel")),
        name="adaln",
    )(cvec, w_mod, b_mod.reshape(DEPTH, 1, N_MOD * D_MODEL))


def _ffn_kernel(*refs, k0, tiles_p, n_chunks, split_in, final):
    refs = list(refs)
    s = pl.program_id(0)
    tile = jnp.maximum(s - (n_chunks - 1), 0)
    prompt = tile < tiles_p
    if split_in:
        xp_ref, xs_ref = refs[:2]
        refs = refs[2:]
        x = jnp.where(prompt, xp_ref[...], xs_ref[...])
    else:
        x = refs.pop(0)[...]
    pat_ref, g_ref, wa_ref, wb_ref, wo_ref = refs[:5]
    refs = refs[5:]
    wa_res, wb_res, wo_res, n_scr, acc_scr, act_scr = refs[-6:]
    refs = refs[:-6]

    def normed():
        return _norm_mod(x, g_ref[...], pat_ref[k0:k0 + 1], pat_ref[k0 + 1:k0 + 2]).astype(BF16)

    def finish(y):
        out = x + (0.5 * pat_ref[k0 + 2:k0 + 3]) * y
        if not final:
            refs[0][...] = out
            return
        fg_ref, op_ref, os_ref = refs
        ms = jnp.mean(out * out, axis=-1, keepdims=True)
        out = (out * lax.rsqrt(ms + EPS)) * fg_ref[...]

        @pl.when(prompt)
        def _():
            op_ref[...] = out

        @pl.when(jnp.logical_not(prompt))
        def _():
            os_ref[...] = out

    @pl.when(s < n_chunks)
    def _():
        @pl.when(s == 0)
        def _():
            n_scr[...] = normed()
            acc_scr[...] = jnp.zeros_like(acc_scr)

        wa = wa_ref[...].astype(BF16)
        wb = wb_ref[...].astype(BF16)
        wo = wo_ref[...].astype(BF16)
        wa_res[s] = wa
        wb_res[s] = wb
        wo_res[pl.ds(pl.multiple_of(s * FF_CHUNK, FF_CHUNK), FF_CHUNK), :] = wo
        n = n_scr[...]
        a = _dot(n, wa)
        b = _dot(n, wb)
        acc_scr[...] += _dot((a * jax.nn.sigmoid(a) * b).astype(BF16), wo)

        @pl.when(s == n_chunks - 1)
        def _():
            finish(acc_scr[...])

    @pl.when(s >= n_chunks)
    def _():
        n = normed()
        for j in range(n_chunks):
            a = _dot(n, wa_res[j])
            b = _dot(n, wb_res[j])
            act_scr[:, j * FF_CHUNK:(j + 1) * FF_CHUNK] = (a * jax.nn.sigmoid(a) * b).astype(BF16)
        finish(_dot(act_scr[...], wo_res[...]))


def _ffn(rows, x, pat, norm_g, w_ffn_in, w_ffn_out, layer, sub, k0, final_g=None):
    split_in = isinstance(x, tuple)
    final = final_g is not None
    n_chunks = D_FF // FF_CHUNK
    tiles_p = rows.tiles_p
    tile_of = lambda s: jnp.maximum(s - (n_chunks - 1), 0)
    chunk_of = lambda s: jnp.minimum(s, n_chunks - 1)
    tile = (ROW_TILE, D_MODEL)
    p_spec = pl.BlockSpec(tile, lambda s: (jnp.minimum(tile_of(s), tiles_p - 1), 0))
    s_spec = pl.BlockSpec(tile, lambda s: (jnp.maximum(tile_of(s) - tiles_p, 0), 0))
    all_spec = pl.BlockSpec(tile, lambda s: (tile_of(s), 0))
    in_specs = ([p_spec, s_spec] if split_in else [all_spec]) + [
        pl.BlockSpec((None, N_MOD, D_MODEL), lambda s: (rows.mod_row(tile_of(s)), 0, 0)),
        pl.BlockSpec((1, D_MODEL), lambda s: (0, 0)),
        pl.BlockSpec((None, None, D_MODEL, FF_CHUNK), lambda s: (layer, sub, 0, chunk_of(s))),
        pl.BlockSpec((None, None, D_MODEL, FF_CHUNK), lambda s: (layer, sub, 0, n_chunks + chunk_of(s))),
        pl.BlockSpec((None, None, FF_CHUNK, D_MODEL), lambda s: (layer, sub, chunk_of(s), 0)),
    ]
    args = (list(x) if split_in else [x]) + [pat, norm_g, w_ffn_in, w_ffn_in, w_ffn_out]
    if final:
        in_specs.append(pl.BlockSpec((1, D_MODEL), lambda s: (0, 0)))
        args.append(final_g)
        out_specs = [p_spec, s_spec]
        out_shape = [jax.ShapeDtypeStruct((rows.tok_p, D_MODEL), F32), jax.ShapeDtypeStruct((rows.tok_s, D_MODEL), F32)]
    else:
        out_specs = all_spec
        out_shape = jax.ShapeDtypeStruct((rows.n_tok, D_MODEL), F32)
    return pl.pallas_call(
        functools.partial(_ffn_kernel, k0=k0, tiles_p=tiles_p, n_chunks=n_chunks, split_in=split_in, final=final),
        grid=(n_chunks - 1 + rows.n_tok // ROW_TILE,),
        in_specs=in_specs,
        out_specs=out_specs,
        out_shape=out_shape,
        scratch_shapes=[
            pltpu.VMEM((n_chunks, D_MODEL, FF_CHUNK), BF16),
            pltpu.VMEM((n_chunks, D_MODEL, FF_CHUNK), BF16),
            pltpu.VMEM((D_FF, D_MODEL), BF16),
            pltpu.VMEM((ROW_TILE, D_MODEL), BF16),
            pltpu.VMEM((ROW_TILE, D_MODEL), F32),
            pltpu.VMEM((ROW_TILE, D_FF), BF16),
        ],
        compiler_params=_cparams(("arbitrary",)),
        name="ffn",
    )(*args)


def _rope(x, cos, sin):
    lane = lax.broadcasted_iota(jnp.int32, (x.shape[0], LANES), 1)
    first_half = (lane % 32) < 16
    out = []
    for c in range(x.shape[1] // LANES):
        xc = x[:, c * LANES:(c + 1) * LANES]
        partner = jnp.where(first_half, pltpu.roll(xc, LANES - 16, 1), pltpu.roll(xc, 16, 1))
        out.append(xc * cos + partner * sin)
    return jnp.concatenate(out, axis=1)


def _inproj_kernel(*refs, tiles_p, n_alias):
    x_ref, pat_ref, g_ref, w_ref, cos_ref, sin_ref = refs[:6]
    up_ref, us_ref, q_ref, ks_ref, vs_ref, kc_ref, vc_ref, z_ref = refs[6 + n_alias:]
    n = _norm_mod(x_ref[...], g_ref[...], pat_ref[3:4], pat_ref[4:5]).astype(BF16)
    y = _dot(n, w_ref[...])
    col = lambda c: y[:, c * D_BRANCH:(c + 1) * D_BRANCH]
    z_ref[...] = col(4)
    latent = pl.program_id(0) >= tiles_p
    scale = DH ** -0.5 * math.log2(math.e)

    @pl.when(latent)
    def _():
        us_ref[...] = col(0)
        q_ref[...] = (_rope(col(1), cos_ref[...], sin_ref[...]) * scale).astype(BF16)
        ks_ref[...] = _rope(col(2), cos_ref[...], sin_ref[...]).astype(BF16)
        vs_ref[...] = col(3).astype(BF16)

    @pl.when(jnp.logical_not(latent))
    def _():
        up_ref[...] = col(0)
        q_ref[...] = (col(1) * scale).astype(BF16)
        kc_ref[...] = col(2).reshape(kc_ref.shape)
        vc_ref[...] = col(3).reshape(vc_ref.shape)


def _inproj(rows, x, pat, norm_g, w_p, layer, rope_tabs, cache_out):
    tiles_p = rows.tiles_p
    seqs_per_tile = ROW_TILE // rows.seq_p
    tabs_per_seq = rows.seq_s // ROW_TILE
    tab_spec = pl.BlockSpec((ROW_TILE, LANES), lambda i: (jnp.maximum(i - tiles_p, 0) % tabs_per_seq, 0))
    tile = (ROW_TILE, D_BRANCH)
    p_spec = pl.BlockSpec(tile, lambda i: (jnp.minimum(i, tiles_p - 1), 0))
    s_spec = pl.BlockSpec(tile, lambda i: (jnp.maximum(i - tiles_p, 0), 0))
    all_spec = pl.BlockSpec(tile, lambda i: (i, 0))
    cache_spec = pl.BlockSpec((seqs_per_tile, None, rows.seq_p, D_BRANCH),
                              lambda i: (jnp.minimum(i, tiles_p - 1), layer, 0, 0))
    cache_shape = jax.ShapeDtypeStruct((rows.n_p, DEPTH, rows.seq_p, D_BRANCH), F32)
    in_specs = [
        pl.BlockSpec((ROW_TILE, D_MODEL), lambda i: (i, 0)),
        pl.BlockSpec((None, N_MOD, D_MODEL), lambda i: (rows.mod_row(i), 0, 0)),
        pl.BlockSpec((1, D_MODEL), lambda i: (0, 0)),
        _resident((None, D_MODEL, N_PROJ * D_BRANCH), lambda i: (layer, 0, 0)),
        tab_spec, tab_spec,
    ]
    args = [x, pat, norm_g, w_p, *rope_tabs]
    aliases = {}
    if cache_out is not None:
        in_specs += [pl.BlockSpec(memory_space=pl.ANY)] * 2
        aliases = {len(args): 5, len(args) + 1: 6}
        args += list(cache_out)
    return pl.pallas_call(
        functools.partial(_inproj_kernel, tiles_p=tiles_p, n_alias=len(aliases)),
        grid=(rows.n_tok // ROW_TILE,),
        in_specs=in_specs,
        out_specs=[p_spec, s_spec, all_spec, s_spec, s_spec, cache_spec, cache_spec, all_spec],
        out_shape=[
            jax.ShapeDtypeStruct((rows.tok_p, D_BRANCH), F32),
            jax.ShapeDtypeStruct((rows.tok_s, D_BRANCH), F32),
            jax.ShapeDtypeStruct((rows.n_tok, D_BRANCH), BF16),
            jax.ShapeDtypeStruct((rows.tok_s, D_BRANCH), BF16),
            jax.ShapeDtypeStruct((rows.tok_s, D_BRANCH), BF16),
            cache_shape, cache_shape,
            jax.ShapeDtypeStruct((rows.n_tok, D_BRANCH), F32),
        ],
        input_output_aliases=aliases,
        compiler_params=_cparams(("arbitrary",)),
        name="inproj",
    )(*args)


def _s5_kernel(*refs, n_seq, n_steps, backward):
    if backward:
        u_ref, part_ref, bb_ref, cc_ref, ar_ref, ai_ref, h0_ref, wglu_ref, ya_ref, hfin_ref, xs_scr, h_scr = refs
    else:
        u_ref, dsk_ref, bb_ref, cc_ref, ar_ref, ai_ref, h0_ref, part_ref, hfin_ref, xs_scr, h_scr = refs
    rows = n_seq * n_steps

    @pl.when(pl.program_id(0) == 0)
    def _():
        h_scr[...] = h0_ref[...]

    u = jnp.swapaxes(u_ref[...], 0, 1).reshape(rows, D_BRANCH)
    ub = u.astype(BF16)
    order = range(n_steps - 1, -1, -1) if backward else range(n_steps)

    def project(j):
        xs_scr[j] = _dot(ub[:, j * LANES:(j + 1) * LANES], bb_ref[j])

    def scan(j):
        lanes = slice(j * SLAB_STATE, (j + 1) * SLAB_STATE)
        ar = jnp.broadcast_to(ar_ref[:, lanes], (n_seq, SLAB_STATE))
        ai = jnp.broadcast_to(ai_ref[:, lanes], (n_seq, SLAB_STATE))
        hr = h_scr[0, :, lanes]
        hi = h_scr[1, :, lanes]
        for t in order:
            r = slice(t * n_seq, (t + 1) * n_seq)
            xr = xs_scr[j, r, 0:SLAB_STATE]
            xi = xs_scr[j, r, SLAB_STATE:2 * SLAB_STATE]
            hr, hi = ar * hr - ai * hi + xr, ar * hi + ai * hr + xi
            xs_scr[j, r, 0:SLAB_STATE] = hr
            xs_scr[j, r, SLAB_STATE:2 * SLAB_STATE] = hi
        h_scr[0, :, lanes] = hr
        h_scr[1, :, lanes] = hi

    ys = []
    project(0)
    for j in range(N_SLAB):
        if j + 1 < N_SLAB:
            project(j + 1)
        scan(j)
        ys.append(_dot(xs_scr[j].astype(BF16), cc_ref[j]))
    y = jnp.concatenate(ys, axis=1)

    if backward:
        g = jax.nn.gelu(part_ref[...] + y)
        ya = g * jax.nn.sigmoid(_dot(g.astype(BF16), wglu_ref[...]))
        ya_ref[...] = jnp.swapaxes(ya.reshape(n_steps, n_seq, D_BRANCH), 0, 1)
    else:
        part_ref[...] = y + dsk_ref[...] * u

    @pl.when(pl.program_id(0) == pl.num_programs(0) - 1)
    def _():
        hfin_ref[...] = h_scr[...]


def _s5(u, ssm, h0, n_seq):
    n_tok = u.shape[0]
    seq = n_tok // n_seq
    n_steps = S5_CHUNK_ROWS // n_seq
    n_chunks = seq // n_steps
    u3 = u.reshape(n_seq, seq, D_BRANCH)

    def specs(backward):
        d = int(backward)
        chunk = (lambda s: n_chunks - 1 - s) if backward else (lambda s: s)
        seq_block = pl.BlockSpec((n_seq, n_steps, D_BRANCH), lambda s: (0, chunk(s), 0))
        part_block = pl.BlockSpec((S5_CHUNK_ROWS, D_BRANCH), lambda s: (chunk(s), 0))
        params = [
            pl.BlockSpec((None, N_SLAB, LANES, 2 * SLAB_STATE), lambda s: (d, 0, 0, 0)),
            pl.BlockSpec((None, N_SLAB, 2 * SLAB_STATE, LANES), lambda s: (d, 0, 0, 0)),
            pl.BlockSpec((None, 1, N_STATE), lambda s: (d, 0, 0)),
            pl.BlockSpec((None, 1, N_STATE), lambda s: (d, 0, 0)),
            pl.BlockSpec((None, 2, n_seq, N_STATE), lambda s: (d, 0, 0, 0)),
        ]
        return seq_block, part_block, params

    param_args = (ssm["bb"], ssm["cc"], ssm["a_re"], ssm["a_im"], h0)
    fin_spec = pl.BlockSpec((2, n_seq, N_STATE), lambda s: (0, 0, 0))
    fin_shape = jax.ShapeDtypeStruct((2, n_seq, N_STATE), F32)
    scratch = [pltpu.VMEM((N_SLAB, S5_CHUNK_ROWS, 2 * SLAB_STATE), F32), pltpu.VMEM((2, n_seq, N_STATE), F32)]

    seq_block, part_block, params = specs(False)
    part, fin_f = pl.pallas_call(
        functools.partial(_s5_kernel, n_seq=n_seq, n_steps=n_steps, backward=False),
        grid=(n_chunks,),
        in_specs=[seq_block, pl.BlockSpec((1, D_BRANCH), lambda s: (0, 0))] + params,
        out_specs=[part_block, fin_spec],
        out_shape=[jax.ShapeDtypeStruct((n_tok, D_BRANCH), F32), fin_shape],
        scratch_shapes=scratch,
        compiler_params=_cparams(("arbitrary",)),
        name="s5_fwd",
    )(u3, ssm["d_skip"], *param_args)

    seq_block, part_block, params = specs(True)
    ya, fin_b = pl.pallas_call(
        functools.partial(_s5_kernel, n_seq=n_seq, n_steps=n_steps, backward=True),
        grid=(n_chunks,),
        in_specs=[seq_block, part_block] + params + [pl.BlockSpec((D_BRANCH, D_BRANCH), lambda s: (0, 0))],
        out_specs=[seq_block, fin_spec],
        out_shape=[jax.ShapeDtypeStruct((n_seq, seq, D_BRANCH), F32), fin_shape],
        scratch_shapes=scratch,
        compiler_params=_cparams(("arbitrary",)),
        name="s5_bwd",
    )(u3, part, *param_args, ssm["w_glu"])
    return ya.reshape(n_tok, D_BRANCH), jnp.stack([fin_f, fin_b], axis=0)


def _s5_params(lam_re, lam_im, log_dt, b_re, b_im, c_re, c_im):
    dt = jnp.exp(log_dt)[..., None]
    mag = jnp.exp(lam_re * dt)
    abr = mag * jnp.cos(lam_im * dt)
    abi = mag * jnp.sin(lam_im * dt)
    den = lam_re * lam_re + lam_im * lam_im
    nr = abr - 1.0
    kr = (nr * lam_re + abi * lam_im) / den
    ki = (abi * lam_re - nr * lam_im) / den
    bbr = kr[..., None] * b_re - ki[..., None] * b_im
    bbi = kr[..., None] * b_im + ki[..., None] * b_re
    per_slab = S5_GROUPS // N_SLAB
    eye = jnp.eye(per_slab, dtype=F32)

    def pack_in(w):
        w = w.reshape(2, N_SLAB, per_slab, S5_STATE, S5_GROUP)
        return jnp.einsum("djgpc,gh->djgchp", w, eye).reshape(2, N_SLAB, LANES, SLAB_STATE)

    def pack_out(w):
        w = w.reshape(2, N_SLAB, per_slab, S5_GROUP, S5_STATE)
        return jnp.einsum("djgcp,gh->djgphc", w, eye).reshape(2, N_SLAB, SLAB_STATE, LANES)

    bb = jnp.concatenate([pack_in(bbr), pack_in(bbi)], axis=-1).astype(BF16)
    cc = jnp.concatenate([pack_out(c_re), -pack_out(c_im)], axis=-2).astype(BF16)
    return bb, cc, abr.reshape(2, 1, N_STATE), abi.reshape(2, 1, N_STATE)


def _attn_kernel(*refs, n_ctx, lam_init):
    if n_ctx:
        lam_ref, q_ref, k_ref, v_ref, ck_ref, cv_ref, g_ref, o_ref, k_scr, v_scr = refs
    else:
        lam_ref, q_ref, k_ref, v_ref, g_ref, o_ref, k_scr, v_scr = refs
    n_own = k_ref.shape[0]

    @pl.when(pl.program_id(1) == 0)
    def _():
        k_scr[0:n_own, :] = k_ref[...].astype(BF16)
        v_scr[0:n_own, :] = v_ref[...].astype(BF16)
        if n_ctx:
            k_scr[n_own:n_own + n_ctx, :] = ck_ref[...].astype(BF16)
            v_scr[n_own:n_own + n_ctx, :] = cv_ref[...].astype(BF16)

    lam = lam_ref[0]
    q = q_ref[...]
    lane = lax.broadcasted_iota(jnp.int32, (q.shape[0], LANES), 1)

    def scores(h):
        lanes = slice(h * LANES, (h + 1) * LANES)
        qh = q[:, lanes]
        kh = k_scr[:, lanes]
        out = []
        for m in range(2):
            qm = jnp.where((lane < DH) if m == 0 else (lane >= DH), qh, jnp.zeros_like(qh))
            out.append(lax.dot_general(qm, kh, (((1,), (1,)), ((), ())), preferred_element_type=F32))
        return out

    sc = scores(0)
    for h in range(N_HEADS):
        lanes = slice(h * LANES, (h + 1) * LANES)
        nxt = scores(h + 1) if h + 1 < N_HEADS else None
        e1, e2 = [jnp.exp2(s_m - jnp.max(s_m, axis=-1, keepdims=True)) for s_m in sc]
        l1 = jnp.sum(e1, axis=-1, keepdims=True)
        l2 = jnp.sum(e2, axis=-1, keepdims=True)
        a = (e1 - e2 * (lam * l1 / l2)).astype(BF16)
        o = _dot(a, v_scr[:, lanes]) * (1.0 / l1)
        ms = jnp.mean(o * o, axis=-1, keepdims=True)
        o_ref[:, lanes] = ((o * lax.rsqrt(ms + EPS)) * g_ref[...]) * (1.0 - lam_init)
        sc = nxt


def _attention(q, k, v, lam, attn_g, row0, n_seq, seq, lam_init, layer, cache=None):
    tq = min(Q_TILE, seq)
    q_tiles = seq // tq
    q0 = row0 // tq
    n_ctx = 0 if cache is None else cache[0].shape[2]
    if k.ndim == 4:
        own = pl.BlockSpec((None, None, seq, D_BRANCH), lambda b, i: (b, layer, 0, 0))
    else:
        own = pl.BlockSpec((seq, D_BRANCH), lambda b, i: (b, 0))
    in_specs = [
        pl.BlockSpec(memory_space=pltpu.SMEM),
        pl.BlockSpec((tq, D_BRANCH), lambda b, i: (q0 + b * q_tiles + i, 0)),
        own, own,
    ]
    args = [lam, q, k, v]
    if n_ctx:
        in_specs += [pl.BlockSpec((None, None, n_ctx, D_BRANCH), lambda b, i: (b, layer, 0, 0))] * 2
        args += list(cache)
    in_specs.append(pl.BlockSpec((1, DV), lambda b, i: (0, 0)))
    args.append(attn_g)
    return pl.pallas_call(
        functools.partial(_attn_kernel, n_ctx=n_ctx, lam_init=lam_init),
        grid=(n_seq, q_tiles),
        in_specs=in_specs,
        out_specs=pl.BlockSpec((tq, D_BRANCH), lambda b, i: (b * q_tiles + i, 0)),
        out_shape=jax.ShapeDtypeStruct((n_seq * seq, D_BRANCH), F32),
        scratch_shapes=[pltpu.VMEM((seq + n_ctx, D_BRANCH), BF16)] * 2,
        compiler_params=_cparams(("parallel", "arbitrary")),
        name="attn",
    )(*args)


def _pool_kernel(z_ref, w_ref, sc_ref, o_ref, pad_scr):
    seq = z_ref.shape[0]
    padded = seq + 2 * POOL_PAD
    z = z_ref[...]
    zeros = jnp.zeros((POOL_PAD, D_BRANCH), F32)
    pad_scr[0:POOL_PAD, :] = zeros
    pad_scr[POOL_PAD:POOL_PAD + seq, :] = z
    pad_scr[POOL_PAD + seq:padded, :] = zeros
    t = lax.broadcasted_iota(jnp.int32, (seq, POOL_GROUP), 0)
    outs = []
    for gi, w in enumerate(POOL_WINDOWS):
        lanes = slice(gi * POOL_GROUP, (gi + 1) * POOL_GROUP)
        run = pad_scr[:, lanes]
        span = 1
        while span < w:
            run = run + pltpu.roll(run, padded - span, 0)
            span *= 2
        tot = pltpu.roll(run, w // 2, 0)[POOL_PAD:POOL_PAD + seq]
        cnt = jnp.minimum(t + w // 2, seq) - jnp.maximum(t - w // 2, 0)
        pooled = tot / cnt.astype(F32) - z[:, lanes]
        outs.append(_dot(pooled.astype(BF16), w_ref[gi]))
    o_ref[...] = jnp.concatenate(outs, axis=1) * sc_ref[...]


def _pool(z, w_pool, pool_scale, row0, n_seq, seq):
    s0 = row0 // seq
    return pl.pallas_call(
        _pool_kernel,
        grid=(n_seq,),
        in_specs=[
            pl.BlockSpec((seq, D_BRANCH), lambda b: (s0 + b, 0)),
            pl.BlockSpec((len(POOL_WINDOWS), POOL_GROUP, POOL_GROUP), lambda b: (0, 0, 0)),
            pl.BlockSpec((1, D_BRANCH), lambda b: (0, 0)),
        ],
        out_specs=pl.BlockSpec((seq, D_BRANCH), lambda b: (b, 0)),
        out_shape=jax.ShapeDtypeStruct((n_seq * seq, D_BRANCH), F32),
        scratch_shapes=[pltpu.VMEM((seq + 2 * POOL_PAD, D_BRANCH), F32)],
        compiler_params=_cparams(("parallel",)),
        name="pool",
    )(z, w_pool, pool_scale)


def _merge_kernel(x_ref, pat_ref, g_ref, *refs, tiles_p):
    branch_refs, (wg_ref, wbr_ref, wo_ref, o_ref) = refs[:2 * N_BRANCH], refs[2 * N_BRANCH:]
    x = x_ref[...]
    prompt = pl.program_id(0) < tiles_p
    n = _norm_mod(x, g_ref[...], pat_ref[3:4], pat_ref[4:5]).astype(BF16)
    merged = None
    for br in range(N_BRANCH):
        y = jnp.where(prompt, branch_refs[2 * br][...], branch_refs[2 * br + 1][...]).astype(BF16)
        gate = jax.nn.sigmoid(_dot(n, wg_ref[:, br * D_MODEL:(br + 1) * D_MODEL]))
        part = gate * _dot(y, wbr_ref[br])
        merged = part if merged is None else merged + part
    o_ref[...] = x + pat_ref[5:6] * _dot(merged.astype(BF16), wo_ref[...])


def _merge(rows, x, pat, norm_g, branches, w_g, w_br, w_o, layer):
    tiles_p = rows.tiles_p
    p_spec = pl.BlockSpec((ROW_TILE, D_BRANCH), lambda i: (jnp.minimum(i, tiles_p - 1), 0))
    s_spec = pl.BlockSpec((ROW_TILE, D_BRANCH), lambda i: (jnp.maximum(i - tiles_p, 0), 0))
    return pl.pallas_call(
        functools.partial(_merge_kernel, tiles_p=tiles_p),
        grid=(rows.n_tok // ROW_TILE,),
        in_specs=[
            pl.BlockSpec((ROW_TILE, D_MODEL), lambda i: (i, 0)),
            pl.BlockSpec((None, N_MOD, D_MODEL), lambda i: (rows.mod_row(i), 0, 0)),
            pl.BlockSpec((1, D_MODEL), lambda i: (0, 0)),
            p_spec, s_spec, p_spec, s_spec, p_spec, s_spec,
            _resident((None, D_MODEL, N_BRANCH * D_MODEL), lambda i: (layer, 0, 0)),
            _resident((None, N_BRANCH, D_BRANCH, D_MODEL), lambda i: (layer, 0, 0, 0)),
            _resident((None, D_MODEL, D_MODEL), lambda i: (layer, 0, 0)),
        ],
        out_specs=pl.BlockSpec((ROW_TILE, D_MODEL), lambda i: (i, 0)),
        out_shape=jax.ShapeDtypeStruct((rows.n_tok, D_MODEL), F32),
        compiler_params=_cparams(("parallel",)),
        name="merge",
    )(x, pat, norm_g, *[y for pair in branches for y in pair], w_g, w_br, w_o)


def _rope_tables(seq):
    n_rows = seq // GRID_W
    row = jnp.repeat(jnp.arange(n_rows, dtype=F32), GRID_W)
    col = jnp.tile(jnp.arange(GRID_W, dtype=F32), n_rows)
    n_freq = DH // 4
    inv = ROPE_BASE ** (-jnp.arange(n_freq, dtype=F32) / n_freq)
    ar = row[:, None] * inv
    ac = col[:, None] * inv
    cos = jnp.concatenate([jnp.cos(ar)] * 2 + [jnp.cos(ac)] * 2, axis=1)
    sin = jnp.concatenate([-jnp.sin(ar), jnp.sin(ar), -jnp.sin(ac), jnp.sin(ac)], axis=1)
    return jnp.tile(cos, (1, LANES // DH)), jnp.tile(sin, (1, LANES // DH))


def kernel(x_prompt, x_sample, cache_k, cache_v, state_ssm, c, c_ctx, norm_g, w_mod, b_mod, w_ffn_in, w_ffn_out, w_in, ssm_lam_re, ssm_lam_im, ssm_log_dt, ssm_b_re, ssm_b_im, ssm_c_re, ssm_c_im, ssm_d, w_glu, lam_q1, lam_k1, lam_q2, lam_k2, attn_norm_g, w_pool, pool_scale, w_branch, w_out, final_norm_g):
    n_p, seq_p, _ = x_prompt.shape
    n_s, seq_s, _ = x_sample.shape
    n_past = cache_k.shape[2]
    rows = _Rows(n_p, seq_p, n_s, seq_s)
    tok_p = rows.tok_p

    cvec = jnp.concatenate([c_ctx[None, :], c, jnp.zeros((16 - 1 - n_s, D_MODEL), F32)], axis=0)
    pat = _adaln(cvec, w_mod, b_mod).reshape(DEPTH, 16, N_MOD, D_MODEL)

    w_ab = w_ffn_in
    w_fo = w_ffn_out
    w_p =w_in[:, :, :N_PROJ * D_BRANCH].astype(BF16)
    w_g = w_in[:, :, N_PROJ * D_BRANCH:].astype(BF16)
    w_br = w_branch.astype(BF16)
    w_o = w_out.astype(BF16)
    w_gl = w_glu.astype(BF16)
    w_pl = w_pool.astype(BF16)

    x = (x_prompt.reshape(tok_p, D_MODEL), x_sample.reshape(rows.tok_s, D_MODEL))
    rope_tabs = _rope_tables(seq_s)
    cache = (cache_k.reshape(n_s, DEPTH, n_past, D_BRANCH), cache_v.reshape(n_s, DEPTH, n_past, D_BRANCH))
    h0_p = jnp.zeros((2, 2, n_p, N_STATE), F32)

    new_cache, new_s = None, []
    for l in range(DEPTH):
        lam_init = 0.8 - 0.6 * math.exp(-0.3 * l)
        lam = (jnp.exp(jnp.sum(lam_q1[l] * lam_k1[l])) - jnp.exp(jnp.sum(lam_q2[l] * lam_k2[l])) + lam_init).reshape(1)
        bb, cc, a_re, a_im = _s5_params(ssm_lam_re[l], ssm_lam_im[l], ssm_log_dt[l], ssm_b_re[l], ssm_b_im[l],
                                        ssm_c_re[l], ssm_c_im[l])
        ssm = dict(bb=bb, cc=cc, a_re=a_re, a_im=a_im, d_skip=ssm_d[l][None, :], w_glu=w_gl[l])
        ng = norm_g[l]
        attn_g = attn_norm_g[l][None, :]
        scale_c = pool_scale[l][None, :]
        h0_s = state_ssm[:, l].reshape(n_s, 2, 2, N_STATE).transpose(1, 2, 0, 3)

        x = _ffn(rows, x, pat[l], ng[0:1], w_ab, w_fo, l, 0, 0)
        u_p, u_s, q, k_s, v_s, k_new, v_new, z = _inproj(rows, x, pat[l], ng[1:2], w_p, l, rope_tabs, new_cache)
        new_cache = (k_new, v_new)

        ya_p, h_fin = _s5(u_p, ssm, h0_p, n_p)
        ya_s, _ = _s5(u_s, ssm, h0_s, n_s)
        yb = (_attention(q, k_new, v_new, lam, attn_g, 0, n_p, seq_p, lam_init, l),
              _attention(q, k_s, v_s, lam, attn_g, tok_p, n_s, seq_s, lam_init, l, cache))
        yc = (_pool(z, w_pl[l], scale_c, 0, n_p, seq_p), _pool(z, w_pl[l], scale_c, tok_p, n_s, seq_s))

        x = _merge(rows, x, pat[l], ng[1:2], ((ya_p, ya_s), yb, yc), w_g, w_br, w_o, l)
        last = l == DEPTH - 1
        x = _ffn(rows, x, pat[l], ng[2:3], w_ab, w_fo, l, 1, 6, final_norm_g[None, :] if last else None)
        new_s.append(h_fin.transpose(2, 0, 1, 3).reshape(n_p, 2, 2, S5_GROUPS, S5_STATE))

    y_p, y_s = x
    return (y_p.reshape(n_p, seq_p, D_MODEL), y_s.reshape(n_s, seq_s, D_MODEL),
            new_cache[0].reshape(n_p, DEPTH, seq_p, N_HEADS, 2, DH), new_cache[1].reshape(n_p, DEPTH, seq_p, N_HEADS, DV),
            jnp.stack(new_s, axis=1))
```

```python
import functools
import math

import jax
import jax.numpy as jnp
from jax import lax
from jax.experimental import pallas as pl
from jax.experimental.pallas import tpu as pltpu

F32 = jnp.float32
BF16 = jnp.bfloat16

D_MODEL = 1024
DEPTH = 4
GRID_W = 64
D_BRANCH = 512
S5_GROUP = 16
S5_GROUPS = 32
S5_STATE = 64
N_STATE = S5_GROUPS * S5_STATE
DH = 64
N_HEADS = 4
DV = 128
POOL_WINDOWS = (2, 4, 8, 16)
POOL_GROUP = 128
POOL_PAD = 16
D_FF = 2816
N_MOD = 9
N_BRANCH = 3
N_PROJ = 5
ROPE_BASE = 10000.0
EPS = 1e-6

LANES = 128
N_SLAB = D_BRANCH // LANES
SLAB_STATE = N_STATE // N_SLAB
ROW_TILE = 512
FF_CHUNK = 256
S5_CHUNK_ROWS = 512
Q_TILE = 512
VMEM_LIMIT = 56 * 1024 * 1024


def _cparams(sem):
    return pltpu.CompilerParams(dimension_semantics=sem, vmem_limit_bytes=VMEM_LIMIT)


def _resident(block_shape, index_map):
    return pl.BlockSpec(block_shape, index_map, pipeline_mode=pl.Buffered(1))


def _norm_mod(x, g, shift, scale):
    ms = jnp.mean(x * x, axis=-1, keepdims=True)
    y = (x * lax.rsqrt(ms + EPS)) * g
    return y * (1.0 + scale) + shift


def _dot(a, b):
    return jnp.dot(a, b, preferred_element_type=F32)


class _Rows:
    def __init__(self, n_p, seq_p, n_s, seq_s):
        self.n_p, self.seq_p, self.n_s, self.seq_s = n_p, seq_p, n_s, seq_s
        self.tok_p = n_p * seq_p
        self.tok_s = n_s * seq_s
        self.n_tok = self.tok_p + self.tok_s
        assert self.tok_p % ROW_TILE == 0 and seq_s % ROW_TILE == 0
        self.tiles_p = self.tok_p // ROW_TILE

    def mod_row(self, i):
        return jnp.where(i < self.tiles_p, 0, 1 + (i - self.tiles_p) // (self.seq_s // ROW_TILE))


def _adaln_kernel(c_ref, w_ref, b_ref, o_ref):
    c = c_ref[...]
    s = (c * jax.nn.sigmoid(c)).astype(BF16)
    o_ref[...] = _dot(s, w_ref[...].astype(BF16)) + b_ref[...]


def _adaln(cvec, w_mod, b_mod):
    tn = 1024
    n_rows = cvec.shape[0]
    return pl.pallas_call(
        _adaln_kernel,
        grid=(DEPTH, N_MOD * D_MODEL // tn),
        in_specs=[
            pl.BlockSpec((n_rows, D_MODEL), lambda l, j: (0, 0)),
            pl.BlockSpec((None, D_MODEL, tn), lambda l, j: (l, 0, j)),
            pl.BlockSpec((None, 1, tn), lambda l, j: (l, 0, j)),
        ],
        out_specs=pl.BlockSpec((None, n_rows, tn), lambda l, j: (l, 0, j)),
        out_shape=jax.ShapeDtypeStruct((DEPTH, n_rows, N_MOD * D_MODEL), F32),
        compiler_params=_cparams(("parallel", "parallel")),
        name="adaln",
    )(cvec, w_mod, b_mod.reshape(DEPTH, 1, N_MOD * D_MODEL))


def _ffn_kernel(*refs, k0, tiles_p, n_chunks, split_in, final):
    refs = list(refs)
    s = pl.program_id(0)
    tile = jnp.maximum(s - (n_chunks - 1), 0)
    prompt = tile < tiles_p
    if split_in:
        xp_ref, xs_ref = refs[:2]
        refs = refs[2:]
        x = jnp.where(prompt, xp_ref[...], xs_ref[...])
    else:
        x = refs.pop(0)[...]
    pat_ref, g_ref, wa_ref, wb_ref, wo_ref = refs[:5]
    refs = refs[5:]
    wa_res, wb_res, wo_res, n_scr, acc_scr, act_scr = refs[-6:]
    refs = refs[:-6]

    def normed():
        return _norm_mod(x, g_ref[...], pat_ref[k0:k0 + 1], pat_ref[k0 + 1:k0 + 2]).astype(BF16)

    def finish(y):
        out = x + (0.5 * pat_ref[k0 + 2:k0 + 3]) * y
        if not final:
            refs[0][...] = out
            return
        fg_ref, op_ref, os_ref = refs
        ms = jnp.mean(out * out, axis=-1, keepdims=True)
        out = (out * lax.rsqrt(ms + EPS)) * fg_ref[...]

        @pl.when(prompt)
        def _():
            op_ref[...] = out

        @pl.when(jnp.logical_not(prompt))
        def _():
            os_ref[...] = out

    @pl.when(s < n_chunks)
    def _():
        @pl.when(s == 0)
        def _():
            n_scr[...] = normed()
            acc_scr[...] = jnp.zeros_like(acc_scr)

        wa = wa_ref[...].astype(BF16)
        wb = wb_ref[...].astype(BF16)
        wo = wo_ref[...].astype(BF16)
        wa_res[s] = wa
        wb_res[s] = wb
        wo_res[pl.ds(pl.multiple_of(s * FF_CHUNK, FF_CHUNK), FF_CHUNK), :] = wo
        n = n_scr[...]
        a = _dot(n, wa)
        b = _dot(n, wb)
        acc_scr[...] += _dot((a * jax.nn.sigmoid(a) * b).astype(BF16), wo)

        @pl.when(s == n_chunks - 1)
        def _():
            finish(acc_scr[...])

    @pl.when(s >= n_chunks)
    def _():
        n = normed()
        for j in range(n_chunks):
            a = _dot(n, wa_res[j])
            b = _dot(n, wb_res[j])
            act_scr[:, j * FF_CHUNK:(j + 1) * FF_CHUNK] = (a * jax.nn.sigmoid(a) * b).astype(BF16)
        finish(_dot(act_scr[...], wo_res[...]))


def _ffn(rows, x, pat, norm_g, w_ffn_in, w_ffn_out, layer, sub, k0, final_g=None):
    split_in = isinstance(x, tuple)
    final = final_g is not None
    n_chunks = D_FF // FF_CHUNK
    tiles_p = rows.tiles_p
    tile_of = lambda s: jnp.maximum(s - (n_chunks - 1), 0)
    chunk_of = lambda s: jnp.minimum(s, n_chunks - 1)
    tile = (ROW_TILE, D_MODEL)
    p_spec = pl.BlockSpec(tile, lambda s: (jnp.minimum(tile_of(s), tiles_p - 1), 0))
    s_spec = pl.BlockSpec(tile, lambda s: (jnp.maximum(tile_of(s) - tiles_p, 0), 0))
    all_spec = pl.BlockSpec(tile, lambda s: (tile_of(s), 0))
    in_specs = ([p_spec, s_spec] if split_in else [all_spec]) + [
        pl.BlockSpec((None, N_MOD, D_MODEL), lambda s: (rows.mod_row(tile_of(s)), 0, 0)),
        pl.BlockSpec((1, D_MODEL), lambda s: (0, 0)),
        pl.BlockSpec((None, None, D_MODEL, FF_CHUNK), lambda s: (layer, sub, 0, chunk_of(s))),
        pl.BlockSpec((None, None, D_MODEL, FF_CHUNK), lambda s: (layer, sub, 0, n_chunks + chunk_of(s))),
        pl.BlockSpec((None, None, FF_CHUNK, D_MODEL), lambda s: (layer, sub, chunk_of(s), 0)),
    ]
    args = (list(x) if split_in else [x]) + [pat, norm_g, w_ffn_in, w_ffn_in, w_ffn_out]
    if final:
        in_specs.append(pl.BlockSpec((1, D_MODEL), lambda s: (0, 0)))
        args.append(final_g)
        out_specs = [p_spec, s_spec]
        out_shape = [jax.ShapeDtypeStruct((rows.tok_p, D_MODEL), F32), jax.ShapeDtypeStruct((rows.tok_s, D_MODEL), F32)]
    else:
        out_specs = all_spec
        out_shape = jax.ShapeDtypeStruct((rows.n_tok, D_MODEL), F32)
    return pl.pallas_call(
        functools.partial(_ffn_kernel, k0=k0, tiles_p=tiles_p, n_chunks=n_chunks, split_in=split_in, final=final),
        grid=(n_chunks - 1 + rows.n_tok // ROW_TILE,),
        in_specs=in_specs,
        out_specs=out_specs,
        out_shape=out_shape,
        scratch_shapes=[
            pltpu.VMEM((n_chunks, D_MODEL, FF_CHUNK), BF16),
            pltpu.VMEM((n_chunks, D_MODEL, FF_CHUNK), BF16),
            pltpu.VMEM((D_FF, D_MODEL), BF16),
            pltpu.VMEM((ROW_TILE, D_MODEL), BF16),
            pltpu.VMEM((ROW_TILE, D_MODEL), F32),
            pltpu.VMEM((ROW_TILE, D_FF), BF16),
        ],
        compiler_params=_cparams(("arbitrary",)),
        name="ffn",
    )(*args)


def _rope(x, cos, sin):
    lane = lax.broadcasted_iota(jnp.int32, (x.shape[0], LANES), 1)
    first_half = (lane % 32) < 16
    out = []
    for c in range(x.shape[1] // LANES):
        xc = x[:, c * LANES:(c + 1) * LANES]
        partner = jnp.where(first_half, pltpu.roll(xc, LANES - 16, 1), pltpu.roll(xc, 16, 1))
        out.append(xc * cos + partner * sin)
    return jnp.concatenate(out, axis=1)


def _inproj_kernel(*refs, tiles_p, n_alias):
    x_ref, pat_ref, g_ref, w_ref, cos_ref, sin_ref = refs[:6]
    up_ref, us_ref, q_ref, ks_ref, vs_ref, kc_ref, vc_ref, z_ref, w_res, n_scr = refs[6 + n_alias:]
    s = pl.program_id(0)
    latent = jnp.maximum(s - (N_PROJ - 1), 0) >= tiles_p
    scale = DH ** -0.5 * math.log2(math.e)

    def normed():
        return _norm_mod(x_ref[...], g_ref[...], pat_ref[3:4], pat_ref[4:5]).astype(BF16)

    def store_prompt(c, y):
        if c == 0:
            up_ref[...] = y
        elif c == 1:
            q_ref[...] = (y * scale).astype(BF16)
        elif c == 2:
            kc_ref[...] = y.reshape(kc_ref.shape)
        elif c == 3:
            vc_ref[...] = y.reshape(vc_ref.shape)
        else:
            z_ref[...] = y

    @pl.when(s < N_PROJ)
    def _():
        @pl.when(s == 0)
        def _():
            n_scr[...] = normed()

        w = w_ref[...].astype(BF16)
        w_res[s] = w
        y = _dot(n_scr[...], w)
        for c in range(N_PROJ):
            @pl.when(s == c)
            def _(c=c):
                store_prompt(c, y)

    @pl.when(s >= N_PROJ)
    def _():
        n = normed()
        col = lambda c: _dot(n, w_res[c])

        @pl.when(latent)
        def _():
            q_ref[...] = (_rope(col(1), cos_ref[...], sin_ref[...]) * scale).astype(BF16)
            ks_ref[...] = _rope(col(2), cos_ref[...], sin_ref[...]).astype(BF16)
            us_ref[...] = col(0)
            vs_ref[...] = col(3).astype(BF16)
            z_ref[...] = col(4)

        @pl.when(jnp.logical_not(latent))
        def _():
            for c in range(N_PROJ):
                store_prompt(c, col(c))


def _inproj(rows, x, pat, norm_g, w_in, layer, rope_tabs, cache_out):
    tiles_p = rows.tiles_p
    seqs_per_tile = ROW_TILE // rows.seq_p
    tabs_per_seq = rows.seq_s // ROW_TILE
    tile_of = lambda s: jnp.maximum(s - (N_PROJ - 1), 0)
    tab_spec = pl.BlockSpec((ROW_TILE, LANES), lambda s: (jnp.maximum(tile_of(s) - tiles_p, 0) % tabs_per_seq, 0))
    tile = (ROW_TILE, D_BRANCH)
    p_spec = pl.BlockSpec(tile, lambda s: (jnp.minimum(tile_of(s), tiles_p - 1), 0))
    s_spec = pl.BlockSpec(tile, lambda s: (jnp.maximum(tile_of(s) - tiles_p, 0), 0))
    all_spec = pl.BlockSpec(tile, lambda s: (tile_of(s), 0))
    cache_spec = pl.BlockSpec((seqs_per_tile, None, rows.seq_p, D_BRANCH),
                              lambda s: (jnp.minimum(tile_of(s), tiles_p - 1), layer, 0, 0))
    cache_shape = jax.ShapeDtypeStruct((rows.n_p, DEPTH, rows.seq_p, D_BRANCH), F32)
    in_specs = [
        pl.BlockSpec((ROW_TILE, D_MODEL), lambda s: (tile_of(s), 0)),
        pl.BlockSpec((None, N_MOD, D_MODEL), lambda s: (rows.mod_row(tile_of(s)), 0, 0)),
        pl.BlockSpec((1, D_MODEL), lambda s: (0, 0)),
        pl.BlockSpec((None, D_MODEL, D_BRANCH), lambda s: (layer, 0, jnp.minimum(s, N_PROJ - 1))),
        tab_spec, tab_spec,
    ]
    args = [x, pat, norm_g, w_in, *rope_tabs]
    aliases = {}
    if cache_out is not None:
        in_specs += [pl.BlockSpec(memory_space=pl.ANY)] * 2
        aliases = {len(args): 5, len(args) + 1: 6}
        args += list(cache_out)
    return pl.pallas_call(
        functools.partial(_inproj_kernel, tiles_p=tiles_p, n_alias=len(aliases)),
        grid=(N_PROJ - 1 + rows.n_tok // ROW_TILE,),
        in_specs=in_specs,
        out_specs=[p_spec, s_spec, all_spec, s_spec, s_spec, cache_spec, cache_spec, all_spec],
        out_shape=[
            jax.ShapeDtypeStruct((rows.tok_p, D_BRANCH), F32),
            jax.ShapeDtypeStruct((rows.tok_s, D_BRANCH), F32),
            jax.ShapeDtypeStruct((rows.n_tok, D_BRANCH), BF16),
            jax.ShapeDtypeStruct((rows.tok_s, D_BRANCH), BF16),
            jax.ShapeDtypeStruct((rows.tok_s, D_BRANCH), BF16),
            cache_shape, cache_shape,
            jax.ShapeDtypeStruct((rows.n_tok, D_BRANCH), F32),
        ],
        scratch_shapes=[pltpu.VMEM((N_PROJ, D_MODEL, D_BRANCH), BF16), pltpu.VMEM((ROW_TILE, D_MODEL), BF16)],
        input_output_aliases=aliases,
        compiler_params=_cparams(("arbitrary",)),
        name="inproj",
    )(*args)


def _s5_kernel(*refs, n_seq, n_steps, backward):
    if backward:
        u_ref, part_ref, bb_ref, cc_ref, ar_ref, ai_ref, h0_ref, wglu_ref, ya_ref, hfin_ref, xs_scr, h_scr = refs
    else:
        u_ref, dsk_ref, bb_ref, cc_ref, ar_ref, ai_ref, h0_ref, part_ref, hfin_ref, xs_scr, h_scr = refs
    rows = n_seq * n_steps

    @pl.when(pl.program_id(0) == 0)
    def _():
        h_scr[...] = h0_ref[...]

    u = jnp.swapaxes(u_ref[...], 0, 1).reshape(rows, D_BRANCH)
    ub = u.astype(BF16)
    order = range(n_steps - 1, -1, -1) if backward else range(n_steps)

    def project(j):
        xs_scr[j] = _dot(ub[:, j * LANES:(j + 1) * LANES], bb_ref[j])

    def scan(j):
        lanes = slice(j * SLAB_STATE, (j + 1) * SLAB_STATE)
        ar = jnp.broadcast_to(ar_ref[:, lanes], (n_seq, SLAB_STATE))
        ai = jnp.broadcast_to(ai_ref[:, lanes], (n_seq, SLAB_STATE))
        hr = h_scr[0, :, lanes]
        hi = h_scr[1, :, lanes]
        for t in order:
            r = slice(t * n_seq, (t + 1) * n_seq)
            xr = xs_scr[j, r, 0:SLAB_STATE]
            xi = xs_scr[j, r, SLAB_STATE:2 * SLAB_STATE]
            hr, hi = ar * hr - ai * hi + xr, ar * hi + ai * hr + xi
            xs_scr[j, r, 0:SLAB_STATE] = hr
            xs_scr[j, r, SLAB_STATE:2 * SLAB_STATE] = hi
        h_scr[0, :, lanes] = hr
        h_scr[1, :, lanes] = hi

    ys = []
    project(0)
    for j in range(N_SLAB):
        if j + 1 < N_SLAB:
            project(j + 1)
        scan(j)
        ys.append(_dot(xs_scr[j].astype(BF16), cc_ref[j]))
    y = jnp.concatenate(ys, axis=1)

    if backward:
        g = jax.nn.gelu(part_ref[...] + y)
        ya = g * jax.nn.sigmoid(_dot(g.astype(BF16), wglu_ref[...]))
        ya_ref[...] = jnp.swapaxes(ya.reshape(n_steps, n_seq, D_BRANCH), 0, 1).astype(BF16)
    else:
        part_ref[...] = y + dsk_ref[...] * u

    @pl.when(pl.program_id(0) == pl.num_programs(0) - 1)
    def _():
        hfin_ref[...] = h_scr[...]


def _s5(u, ssm, h0, n_seq):
    n_tok = u.shape[0]
    seq = n_tok // n_seq
    n_steps = S5_CHUNK_ROWS // n_seq
    n_chunks = seq // n_steps
    u3 = u.reshape(n_seq, seq, D_BRANCH)

    def specs(backward):
        d = int(backward)
        chunk = (lambda s: n_chunks - 1 - s) if backward else (lambda s: s)
        seq_block = pl.BlockSpec((n_seq, n_steps, D_BRANCH), lambda s: (0, chunk(s), 0))
        part_block = pl.BlockSpec((S5_CHUNK_ROWS, D_BRANCH), lambda s: (chunk(s), 0))
        params = [
            pl.BlockSpec((None, N_SLAB, LANES, 2 * SLAB_STATE), lambda s: (d, 0, 0, 0)),
            pl.BlockSpec((None, N_SLAB, 2 * SLAB_STATE, LANES), lambda s: (d, 0, 0, 0)),
            pl.BlockSpec((None, 1, N_STATE), lambda s: (d, 0, 0)),
            pl.BlockSpec((None, 1, N_STATE), lambda s: (d, 0, 0)),
            pl.BlockSpec((None, 2, n_seq, N_STATE), lambda s: (d, 0, 0, 0)),
        ]
        return seq_block, part_block, params

    param_args = (ssm["bb"], ssm["cc"], ssm["a_re"], ssm["a_im"], h0)
    fin_spec = pl.BlockSpec((2, n_seq, N_STATE), lambda s: (0, 0, 0))
    fin_shape = jax.ShapeDtypeStruct((2, n_seq, N_STATE), F32)
    scratch = [pltpu.VMEM((N_SLAB, S5_CHUNK_ROWS, 2 * SLAB_STATE), F32), pltpu.VMEM((2, n_seq, N_STATE), F32)]

    seq_block, part_block, params = specs(False)
    part, fin_f = pl.pallas_call(
        functools.partial(_s5_kernel, n_seq=n_seq, n_steps=n_steps, backward=False),
        grid=(n_chunks,),
        in_specs=[seq_block, pl.BlockSpec((1, D_BRANCH), lambda s: (0, 0))] + params,
        out_specs=[part_block, fin_spec],
        out_shape=[jax.ShapeDtypeStruct((n_tok, D_BRANCH), F32), fin_shape],
        scratch_shapes=scratch,
        compiler_params=_cparams(("arbitrary",)),
        name="s5_fwd",
    )(u3, ssm["d_skip"], *param_args)

    seq_block, part_block, params = specs(True)
    ya, fin_b = pl.pallas_call(
        functools.partial(_s5_kernel, n_seq=n_seq, n_steps=n_steps, backward=True),
        grid=(n_chunks,),
        in_specs=[seq_block, part_block] + params + [pl.BlockSpec((D_BRANCH, D_BRANCH), lambda s: (0, 0))],
        out_specs=[seq_block, fin_spec],
        out_shape=[jax.ShapeDtypeStruct((n_seq, seq, D_BRANCH), BF16), fin_shape],
        scratch_shapes=scratch,
        compiler_params=_cparams(("arbitrary",)),
        name="s5_bwd",
    )(u3, part, *param_args, ssm["w_glu"])
    return ya.reshape(n_tok, D_BRANCH), jnp.stack([fin_f, fin_b], axis=0)


def _s5_params(lam_re, lam_im, log_dt, b_re, b_im, c_re, c_im):
    dt = jnp.exp(log_dt)[..., None]
    mag = jnp.exp(lam_re * dt)
    abr = mag * jnp.cos(lam_im * dt)
    abi = mag * jnp.sin(lam_im * dt)
    den = lam_re * lam_re + lam_im * lam_im
    nr = abr - 1.0
    kr = (nr * lam_re + abi * lam_im) / den
    ki = (abi * lam_re - nr * lam_im) / den
    bbr = kr[..., None] * b_re - ki[..., None] * b_im
    bbi = kr[..., None] * b_im + ki[..., None] * b_re
    per_slab = S5_GROUPS // N_SLAB
    eye = jnp.eye(per_slab, dtype=F32)

    def pack_in(w):
        w = w.reshape(2, N_SLAB, per_slab, S5_STATE, S5_GROUP)
        return jnp.einsum("djgpc,gh->djgchp", w, eye).reshape(2, N_SLAB, LANES, SLAB_STATE)

    def pack_out(w):
        w = w.reshape(2, N_SLAB, per_slab, S5_GROUP, S5_STATE)
        return jnp.einsum("djgcp,gh->djgphc", w, eye).reshape(2, N_SLAB, SLAB_STATE, LANES)

    bb = jnp.concatenate([pack_in(bbr), pack_in(bbi)], axis=-1).astype(BF16)
    cc = jnp.concatenate([pack_out(c_re), -pack_out(c_im)], axis=-2).astype(BF16)
    return bb, cc, abr.reshape(2, 1, N_STATE), abi.reshape(2, 1, N_STATE)


def _attn_kernel(*refs, n_ctx, lam_init):
    if n_ctx:
        lam_ref, q_ref, k_ref, v_ref, ck_ref, cv_ref, g_ref, o_ref, k_scr, v_scr = refs
    else:
        lam_ref, q_ref, k_ref, v_ref, g_ref, o_ref, k_scr, v_scr = refs
    n_own = k_ref.shape[0]

    @pl.when(pl.program_id(1) == 0)
    def _():
        k_scr[0:n_own, :] = k_ref[...].astype(BF16)
        v_scr[0:n_own, :] = v_ref[...].astype(BF16)
        if n_ctx:
            k_scr[n_own:n_own + n_ctx, :] = ck_ref[...].astype(BF16)
            v_scr[n_own:n_own + n_ctx, :] = cv_ref[...].astype(BF16)

    lam = lam_ref[0]
    q = q_ref[...]
    lane = lax.broadcasted_iota(jnp.int32, (q.shape[0], LANES), 1)

    def scores(h):
        lanes = slice(h * LANES, (h + 1) * LANES)
        qh = q[:, lanes]
        kh = k_scr[:, lanes]
        out = []
        for m in range(2):
            qm = jnp.where((lane < DH) if m == 0 else (lane >= DH), qh, jnp.zeros_like(qh))
            out.append(lax.dot_general(qm, kh, (((1,), (1,)), ((), ())), preferred_element_type=F32))
        return out

    sc = scores(0)
    for h in range(N_HEADS):
        lanes = slice(h * LANES, (h + 1) * LANES)
        nxt = scores(h + 1) if h + 1 < N_HEADS else None
        e1, e2 = [jnp.exp2(s_m - jnp.max(s_m, axis=-1, keepdims=True)) for s_m in sc]
        l1 = jnp.sum(e1, axis=-1, keepdims=True)
        l2 = jnp.sum(e2, axis=-1, keepdims=True)
        a = (e1 - e2 * (lam * l1 / l2)).astype(BF16)
        o = _dot(a, v_scr[:, lanes]) * (1.0 / l1)
        ms = jnp.mean(o * o, axis=-1, keepdims=True)
        o_ref[:, lanes] = (((o * lax.rsqrt(ms + EPS)) * g_ref[...]) * (1.0 - lam_init)).astype(BF16)
        sc = nxt


def _attention(q, k, v, lam, attn_g, row0, n_seq, seq, lam_init, layer, cache=None):
    tq = min(Q_TILE, seq)
    q_tiles = seq // tq
    q0 = row0 // tq
    n_ctx = 0 if cache is None else cache[0].shape[2]
    if k.ndim == 4:
        own = pl.BlockSpec((None, None, seq, D_BRANCH), lambda b, i: (b, layer, 0, 0))
    else:
        own = pl.BlockSpec((seq, D_BRANCH), lambda b, i: (b, 0))
    in_specs = [
        pl.BlockSpec(memory_space=pltpu.SMEM),
        pl.BlockSpec((tq, D_BRANCH), lambda b, i: (q0 + b * q_tiles + i, 0)),
        own, own,
    ]
    args = [lam, q, k, v]
    if n_ctx:
        in_specs += [pl.BlockSpec((None, None, n_ctx, D_BRANCH), lambda b, i: (b, layer, 0, 0))] * 2
        args += list(cache)
    in_specs.append(pl.BlockSpec((1, DV), lambda b, i: (0, 0)))
    args.append(attn_g)
    return pl.pallas_call(
        functools.partial(_attn_kernel, n_ctx=n_ctx, lam_init=lam_init),
        grid=(n_seq, q_tiles),
        in_specs=in_specs,
        out_specs=pl.BlockSpec((tq, D_BRANCH), lambda b, i: (b * q_tiles + i, 0)),
        out_shape=jax.ShapeDtypeStruct((n_seq * seq, D_BRANCH), BF16),
        scratch_shapes=[pltpu.VMEM((seq + n_ctx, D_BRANCH), BF16)] * 2,
        compiler_params=_cparams(("parallel", "arbitrary")),
        name="attn",
    )(*args)


def _pool_kernel(z_ref, w_ref, sc_ref, o_ref, pad_scr):
    seq = z_ref.shape[0]
    padded = seq + 2 * POOL_PAD
    z = z_ref[...]
    zeros = jnp.zeros((POOL_PAD, D_BRANCH), F32)
    pad_scr[0:POOL_PAD, :] = zeros
    pad_scr[POOL_PAD:POOL_PAD + seq, :] = z
    pad_scr[POOL_PAD + seq:padded, :] = zeros
    t = lax.broadcasted_iota(jnp.int32, (seq, POOL_GROUP), 0)
    outs = []
    for gi, w in enumerate(POOL_WINDOWS):
        lanes = slice(gi * POOL_GROUP, (gi + 1) * POOL_GROUP)
        run = pad_scr[:, lanes]
        span = 1
        while span < w:
            run = run + pltpu.roll(run, padded - span, 0)
            span *= 2
        tot = pltpu.roll(run, w // 2, 0)[POOL_PAD:POOL_PAD + seq]
        cnt = jnp.minimum(t + w // 2, seq) - jnp.maximum(t - w // 2, 0)
        pooled = tot / cnt.astype(F32) - z[:, lanes]
        outs.append(_dot(pooled.astype(BF16), w_ref[gi]))
    o_ref[...] = (jnp.concatenate(outs, axis=1) * sc_ref[...]).astype(BF16)


def _pool(z, w_pool, pool_scale, row0, n_seq, seq):
    s0 = row0 // seq
    return pl.pallas_call(
        _pool_kernel,
        grid=(n_seq,),
        in_specs=[
            pl.BlockSpec((seq, D_BRANCH), lambda b: (s0 + b, 0)),
            pl.BlockSpec((len(POOL_WINDOWS), POOL_GROUP, POOL_GROUP), lambda b: (0, 0, 0)),
            pl.BlockSpec((1, D_BRANCH), lambda b: (0, 0)),
        ],
        out_specs=pl.BlockSpec((seq, D_BRANCH), lambda b: (b, 0)),
        out_shape=jax.ShapeDtypeStruct((n_seq * seq, D_BRANCH), BF16),
        scratch_shapes=[pltpu.VMEM((seq + 2 * POOL_PAD, D_BRANCH), F32)],
        compiler_params=_cparams(("parallel",)),
        name="pool",
    )(z, w_pool, pool_scale)


def _merge_kernel(x_ref, pat_ref, g_ref, *refs, tiles_p):
    branch_refs = refs[:2 * N_BRANCH]
    wg0_ref, wg1_ref, wbr_ref, wo_ref, o_ref, wg_res, wbr_res, wo_res, n_scr, m_scr = refs[2 * N_BRANCH:]
    s = pl.program_id(0)
    prompt = jnp.maximum(s - (N_BRANCH - 1), 0) < tiles_p
    x = x_ref[...]

    def normed():
        return _norm_mod(x, g_ref[...], pat_ref[3:4], pat_ref[4:5]).astype(BF16)

    def branch_in(br):
        return jnp.where(prompt, branch_refs[2 * br][...], branch_refs[2 * br + 1][...])

    def finish(merged):
        o_ref[...] = x + pat_ref[5:6] * _dot(merged.astype(BF16), wo_res[...])

    @pl.when(s < N_BRANCH)
    def _():
        @pl.when(s == 0)
        def _():
            n_scr[...] = normed()
            m_scr[...] = jnp.zeros_like(m_scr)
            wo_res[...] = wo_ref[...].astype(BF16)

        wg = jnp.concatenate([wg0_ref[...], wg1_ref[...]], axis=1).astype(BF16)
        wbr = wbr_ref[...].astype(BF16)
        wg_res[s] = wg
        wbr_res[s] = wbr
        gate = jax.nn.sigmoid(_dot(n_scr[...], wg))
        for br in range(N_BRANCH):
            @pl.when(s == br)
            def _(br=br):
                m_scr[...] += gate * _dot(branch_in(br), wbr)

        @pl.when(s == N_BRANCH - 1)
        def _():
            finish(m_scr[...])

    @pl.when(s >= N_BRANCH)
    def _():
        n = normed()
        merged = None
        for br in range(N_BRANCH):
            part = jax.nn.sigmoid(_dot(n, wg_res[br])) * _dot(branch_in(br), wbr_res[br])
            merged = part if merged is None else merged + part
        finish(merged)


def _merge(rows, x, pat, norm_g, branches, w_in, w_branch, w_out, layer):
    tiles_p = rows.tiles_p
    half = D_MODEL // 2
    gate0 = N_PROJ * D_BRANCH // half
    tile_of = lambda s: jnp.maximum(s - (N_BRANCH - 1), 0)
    br_of = lambda s: jnp.minimum(s, N_BRANCH - 1)
    p_spec = pl.BlockSpec((ROW_TILE, D_BRANCH), lambda s: (jnp.minimum(tile_of(s), tiles_p - 1), 0))
    s_spec = pl.BlockSpec((ROW_TILE, D_BRANCH), lambda s: (jnp.maximum(tile_of(s) - tiles_p, 0), 0))
    return pl.pallas_call(
        functools.partial(_merge_kernel, tiles_p=tiles_p),
        grid=(N_BRANCH - 1 + rows.n_tok // ROW_TILE,),
        in_specs=[
            pl.BlockSpec((ROW_TILE, D_MODEL), lambda s: (tile_of(s), 0)),
            pl.BlockSpec((None, N_MOD, D_MODEL), lambda s: (rows.mod_row(tile_of(s)), 0, 0)),
            pl.BlockSpec((1, D_MODEL), lambda s: (0, 0)),
            p_spec, s_spec, p_spec, s_spec, p_spec, s_spec,
            pl.BlockSpec((None, D_MODEL, half), lambda s: (layer, 0, gate0 + 2 * br_of(s))),
            pl.BlockSpec((None, D_MODEL, half), lambda s: (layer, 0, gate0 + 2 * br_of(s) + 1)),
            pl.BlockSpec((None, None, D_BRANCH, D_MODEL), lambda s: (layer, br_of(s), 0, 0)),
            _resident((None, D_MODEL, D_MODEL), lambda s: (layer, 0, 0)),
        ],
        out_specs=pl.BlockSpec((ROW_TILE, D_MODEL), lambda s: (tile_of(s), 0)),
        out_shape=jax.ShapeDtypeStruct((rows.n_tok, D_MODEL), F32),
        scratch_shapes=[
            pltpu.VMEM((N_BRANCH, D_MODEL, D_MODEL), BF16),
            pltpu.VMEM((N_BRANCH, D_BRANCH, D_MODEL), BF16),
            pltpu.VMEM((D_MODEL, D_MODEL), BF16),
            pltpu.VMEM((ROW_TILE, D_MODEL), BF16),
            pltpu.VMEM((ROW_TILE, D_MODEL), F32),
        ],
        compiler_params=_cparams(("arbitrary",)),
        name="merge",
    )(x, pat, norm_g, *[y for pair in branches for y in pair], w_in, w_in, w_branch, w_out)


def _rope_tables(seq):
    n_rows = seq // GRID_W
    row = jnp.repeat(jnp.arange(n_rows, dtype=F32), GRID_W)
    col = jnp.tile(jnp.arange(GRID_W, dtype=F32), n_rows)
    n_freq = DH // 4
    inv = ROPE_BASE ** (-jnp.arange(n_freq, dtype=F32) / n_freq)
    ar = row[:, None] * inv
    ac = col[:, None] * inv
    cos = jnp.concatenate([jnp.cos(ar)] * 2 + [jnp.cos(ac)] * 2, axis=1)
    sin = jnp.concatenate([-jnp.sin(ar), jnp.sin(ar), -jnp.sin(ac), jnp.sin(ac)], axis=1)
    return jnp.tile(cos, (1, LANES // DH)), jnp.tile(sin, (1, LANES // DH))


def kernel(x_prompt, x_sample, cache_k, cache_v, state_ssm, c, c_ctx, norm_g, w_mod, b_mod, w_ffn_in, w_ffn_out, w_in, ssm_lam_re, ssm_lam_im, ssm_log_dt, ssm_b_re, ssm_b_im, ssm_c_re, ssm_c_im, ssm_d, w_glu, lam_q1, lam_k1, lam_q2, lam_k2, attn_norm_g, w_pool, pool_scale, w_branch, w_out, final_norm_g):
    n_p, seq_p, _ = x_prompt.shape
    n_s, seq_s, _ = x_sample.shape
    n_past = cache_k.shape[2]
    rows = _Rows(n_p, seq_p, n_s, seq_s)
    tok_p = rows.tok_p

    cvec = jnp.concatenate([c_ctx[None, :], c, jnp.zeros((16 - 1 - n_s, D_MODEL), F32)], axis=0)
    pat = _adaln(cvec, w_mod, b_mod).reshape(DEPTH, 16, N_MOD, D_MODEL)

    w_gl = w_glu.astype(BF16)
    w_pl = w_pool.astype(BF16)

    x = (x_prompt.reshape(tok_p, D_MODEL), x_sample.reshape(rows.tok_s, D_MODEL))
    rope_tabs = _rope_tables(seq_s)
    cache = (cache_k.reshape(n_s, DEPTH, n_past, D_BRANCH), cache_v.reshape(n_s, DEPTH, n_past, D_BRANCH))
    h0_p = jnp.zeros((2, 2, n_p, N_STATE), F32)

    new_cache, new_s = None, []
    for l in range(DEPTH):
        lam_init = 0.8 - 0.6 * math.exp(-0.3 * l)
        lam = (jnp.exp(jnp.sum(lam_q1[l] * lam_k1[l])) - jnp.exp(jnp.sum(lam_q2[l] * lam_k2[l])) + lam_init).reshape(1)
        bb, cc, a_re, a_im = _s5_params(ssm_lam_re[l], ssm_lam_im[l], ssm_log_dt[l], ssm_b_re[l], ssm_b_im[l],
                                        ssm_c_re[l], ssm_c_im[l])
        ssm = dict(bb=bb, cc=cc, a_re=a_re, a_im=a_im, d_skip=ssm_d[l][None, :], w_glu=w_gl[l])
        ng = norm_g[l]
        attn_g = attn_norm_g[l][None, :]
        scale_c = pool_scale[l][None, :]
        h0_s = state_ssm[:, l].reshape(n_s, 2, 2, N_STATE).transpose(1, 2, 0, 3)

        x = _ffn(rows, x, pat[l], ng[0:1], w_ffn_in, w_ffn_out, l, 0, 0)
        u_p, u_s, q, k_s, v_s, k_new, v_new, z = _inproj(rows, x, pat[l], ng[1:2], w_in, l, rope_tabs, new_cache)
        new_cache = (k_new, v_new)

        ya_p, h_fin = _s5(u_p, ssm, h0_p, n_p)
        ya_s, _ = _s5(u_s, ssm, h0_s, n_s)
        yb = (_attention(q, k_new, v_new, lam, attn_g, 0, n_p, seq_p, lam_init, l),
              _attention(q, k_s, v_s, lam, attn_g, tok_p, n_s, seq_s, lam_init, l, cache))
        yc = (_pool(z, w_pl[l], scale_c, 0, n_p, seq_p), _pool(z, w_pl[l], scale_c, tok_p, n_s, seq_s))

        x = _merge(rows, x, pat[l], ng[1:2], ((ya_p, ya_s), yb, yc), w_in, w_branch, w_out, l)
        last = l == DEPTH - 1
        x = _ffn(rows, x, pat[l], ng[2:3], w_ffn_in, w_ffn_out, l, 1, 6, final_norm_g[None, :] if last else None)
        new_s.append(h_fin.transpose(2, 0, 1, 3).reshape(n_p, 2, 2, S5_GROUPS, S5_STATE))

    y_p, y_s = x
    return (y_p.reshape(n_p, seq_p, D_MODEL), y_s.reshape(n_s, seq_s, D_MODEL),
            new_cache[0].reshape(n_p, DEPTH, seq_p, N_HEADS, 2, DH), new_cache[1].reshape(n_p, DEPTH, seq_p, N_HEADS, DV),
            jnp.stack(new_s, axis=1))
```

```python
import functools
import math

import jax
import jax.numpy as jnp
from jax import lax
from jax.experimental import pallas as pl
from jax.experimental.pallas import tpu as pltpu

F32 = jnp.float32
BF16 = jnp.bfloat16

D_MODEL = 1024
DEPTH = 4
GRID_W = 64
D_BRANCH = 512
S5_GROUP = 16
S5_GROUPS = 32
S5_STATE = 64
N_STATE = S5_GROUPS * S5_STATE
DH = 64
N_HEADS = 4
DV = 128
POOL_WINDOWS = (2, 4, 8, 16)
POOL_GROUP = 128
POOL_PAD = 16
D_FF = 2816
N_MOD = 9
N_BRANCH = 3
N_PROJ = 5
ROPE_BASE = 10000.0
EPS = 1e-6

LANES = 128
N_SLAB = D_BRANCH // LANES
SLAB_STATE = N_STATE // N_SLAB
ROW_TILE = 512
FF_CHUNK = 256
S5_CHUNK_ROWS = 512
Q_TILE = 512
VMEM_LIMIT = 56 * 1024 * 1024


def _cparams(sem):
    return pltpu.CompilerParams(dimension_semantics=sem, vmem_limit_bytes=VMEM_LIMIT)


def _resident(block_shape, index_map):
    return pl.BlockSpec(block_shape, index_map, pipeline_mode=pl.Buffered(1))


def _norm_mod(x, g, shift, scale):
    ms = jnp.mean(x * x, axis=-1, keepdims=True)
    y = (x * lax.rsqrt(ms + EPS)) * g
    return y * (1.0 + scale) + shift


def _dot(a, b):
    return jnp.dot(a, b, preferred_element_type=F32)


class _Rows:
    def __init__(self, n_p, seq_p, n_s, seq_s):
        self.n_p, self.seq_p, self.n_s, self.seq_s = n_p, seq_p, n_s, seq_s
        self.tok_p = n_p * seq_p
        self.tok_s = n_s * seq_s
        self.n_tok = self.tok_p + self.tok_s
        assert self.tok_p % ROW_TILE == 0 and seq_s % ROW_TILE == 0
        self.tiles_p = self.tok_p // ROW_TILE

    def mod_row(self, i):
        return jnp.where(i < self.tiles_p, 0, 1 + (i - self.tiles_p) // (self.seq_s // ROW_TILE))


def _adaln_kernel(c_ref, w_ref, b_ref, o_ref):
    c = c_ref[...]
    s = (c * jax.nn.sigmoid(c)).astype(BF16)
    o_ref[...] = _dot(s, w_ref[...].astype(BF16)) + b_ref[...]


def _adaln(cvec, w_mod, b_mod):
    tn = 1024
    n_rows = cvec.shape[0]
    return pl.pallas_call(
        _adaln_kernel,
        grid=(DEPTH, N_MOD * D_MODEL // tn),
        in_specs=[
            pl.BlockSpec((n_rows, D_MODEL), lambda l, j: (0, 0)),
            pl.BlockSpec((None, D_MODEL, tn), lambda l, j: (l, 0, j)),
            pl.BlockSpec((None, 1, tn), lambda l, j: (l, 0, j)),
        ],
        out_specs=pl.BlockSpec((None, n_rows, tn), lambda l, j: (l, 0, j)),
        out_shape=jax.ShapeDtypeStruct((DEPTH, n_rows, N_MOD * D_MODEL), F32),
        compiler_params=_cparams(("parallel", "parallel")),
        name="adaln",
    )(cvec, w_mod, b_mod.reshape(DEPTH, 1, N_MOD * D_MODEL))


def _ffn_kernel(*refs, k0, tiles_p, n_chunks, split_in, final):
    refs = list(refs)
    s = pl.program_id(0)
    tile = jnp.maximum(s - (n_chunks - 1), 0)
    prompt = tile < tiles_p
    if split_in:
        xp_ref, xs_ref = refs[:2]
        refs = refs[2:]
        x = jnp.where(prompt, xp_ref[...], xs_ref[...])
    else:
        x = refs.pop(0)[...]
    pat_ref, g_ref, wa_ref, wb_ref, wo_ref = refs[:5]
    refs = refs[5:]
    wa_res, wb_res, wo_res, n_scr, acc_scr, act_scr = refs[-6:]
    refs = refs[:-6]

    def normed():
        return _norm_mod(x, g_ref[...], pat_ref[k0:k0 + 1], pat_ref[k0 + 1:k0 + 2]).astype(BF16)

    def finish(y):
        out = x + (0.5 * pat_ref[k0 + 2:k0 + 3]) * y
        if not final:
            refs[0][...] = out
            return
        fg_ref, op_ref, os_ref = refs
        ms = jnp.mean(out * out, axis=-1, keepdims=True)
        out = (out * lax.rsqrt(ms + EPS)) * fg_ref[...]

        @pl.when(prompt)
        def _():
            op_ref[...] = out

        @pl.when(jnp.logical_not(prompt))
        def _():
            os_ref[...] = out

    @pl.when(s < n_chunks)
    def _():
        @pl.when(s == 0)
        def _():
            n_scr[...] = normed()
            acc_scr[...] = jnp.zeros_like(acc_scr)

        wa = wa_ref[...].astype(BF16)
        wb = wb_ref[...].astype(BF16)
        wo = wo_ref[...].astype(BF16)
        wa_res[s] = wa
        wb_res[s] = wb
        wo_res[pl.ds(pl.multiple_of(s * FF_CHUNK, FF_CHUNK), FF_CHUNK), :] = wo
        n = n_scr[...]
        a = _dot(n, wa)
        b = _dot(n, wb)
        acc_scr[...] += _dot((a * jax.nn.sigmoid(a) * b).astype(BF16), wo)

        @pl.when(s == n_chunks - 1)
        def _():
            finish(acc_scr[...])

    @pl.when(s >= n_chunks)
    def _():
        n = normed()
        for j in range(n_chunks):
            a = _dot(n, wa_res[j])
            b = _dot(n, wb_res[j])
            act_scr[:, j * FF_CHUNK:(j + 1) * FF_CHUNK] = (a * jax.nn.sigmoid(a) * b).astype(BF16)
        finish(_dot(act_scr[...], wo_res[...]))


def _ffn(rows, x, pat, norm_g, w_ffn_in, w_ffn_out, layer, sub, k0, final_g=None):
    split_in = isinstance(x, tuple)
    final = final_g is not None
    n_chunks = D_FF // FF_CHUNK
    tiles_p = rows.tiles_p
    tile_of = lambda s: jnp.maximum(s - (n_chunks - 1), 0)
    chunk_of = lambda s: jnp.minimum(s, n_chunks - 1)
    tile = (ROW_TILE, D_MODEL)
    p_spec = pl.BlockSpec(tile, lambda s: (jnp.minimum(tile_of(s), tiles_p - 1), 0))
    s_spec = pl.BlockSpec(tile, lambda s: (jnp.maximum(tile_of(s) - tiles_p, 0), 0))
    all_spec = pl.BlockSpec(tile, lambda s: (tile_of(s), 0))
    in_specs = ([p_spec, s_spec] if split_in else [all_spec]) + [
        pl.BlockSpec((None, N_MOD, D_MODEL), lambda s: (rows.mod_row(tile_of(s)), 0, 0)),
        pl.BlockSpec((1, D_MODEL), lambda s: (0, 0)),
        pl.BlockSpec((None, None, D_MODEL, FF_CHUNK), lambda s: (layer, sub, 0, chunk_of(s))),
        pl.BlockSpec((None, None, D_MODEL, FF_CHUNK), lambda s: (layer, sub, 0, n_chunks + chunk_of(s))),
        pl.BlockSpec((None, None, FF_CHUNK, D_MODEL), lambda s: (layer, sub, chunk_of(s), 0)),
    ]
    args = (list(x) if split_in else [x]) + [pat, norm_g, w_ffn_in, w_ffn_in, w_ffn_out]
    if final:
        in_specs.append(pl.BlockSpec((1, D_MODEL), lambda s: (0, 0)))
        args.append(final_g)
        out_specs = [p_spec, s_spec]
        out_shape = [jax.ShapeDtypeStruct((rows.tok_p, D_MODEL), F32), jax.ShapeDtypeStruct((rows.tok_s, D_MODEL), F32)]
    else:
        out_specs = all_spec
        out_shape = jax.ShapeDtypeStruct((rows.n_tok, D_MODEL), F32)
    return pl.pallas_call(
        functools.partial(_ffn_kernel, k0=k0, tiles_p=tiles_p, n_chunks=n_chunks, split_in=split_in, final=final),
        grid=(n_chunks - 1 + rows.n_tok // ROW_TILE,),
        in_specs=in_specs,
        out_specs=out_specs,
        out_shape=out_shape,
        scratch_shapes=[
            pltpu.VMEM((n_chunks, D_MODEL, FF_CHUNK), BF16),
            pltpu.VMEM((n_chunks, D_MODEL, FF_CHUNK), BF16),
            pltpu.VMEM((D_FF, D_MODEL), BF16),
            pltpu.VMEM((ROW_TILE, D_MODEL), BF16),
            pltpu.VMEM((ROW_TILE, D_MODEL), F32),
            pltpu.VMEM((ROW_TILE, D_FF), BF16),
        ],
        compiler_params=_cparams(("arbitrary",)),
        name="ffn",
    )(*args)


def _rope(x, cos, sin):
    lane = lax.broadcasted_iota(jnp.int32, (x.shape[0], LANES), 1)
    first_half = (lane % 32) < 16
    out = []
    for c in range(x.shape[1] // LANES):
        xc = x[:, c * LANES:(c + 1) * LANES]
        partner = jnp.where(first_half, pltpu.roll(xc, LANES - 16, 1), pltpu.roll(xc, 16, 1))
        out.append(xc * cos + partner * sin)
    return jnp.concatenate(out, axis=1)


def _inproj_kernel(*refs, tiles_p, n_alias):
    x_ref, pat_ref, g_ref, w_ref, cos_ref, sin_ref = refs[:6]
    up_ref, us_ref, q_ref, ks_ref, vs_ref, kc_ref, vc_ref, z_ref, w_res, n_scr = refs[6 + n_alias:]
    s = pl.program_id(0)
    latent = jnp.maximum(s - (N_PROJ - 1), 0) >= tiles_p
    scale = DH ** -0.5 * math.log2(math.e)

    def normed():
        return _norm_mod(x_ref[...], g_ref[...], pat_ref[3:4], pat_ref[4:5]).astype(BF16)

    def store_prompt(c, y):
        if c == 0:
            up_ref[...] = y
        elif c == 1:
            q_ref[...] = (y * scale).astype(BF16)
        elif c == 2:
            kc_ref[...] = y.reshape(kc_ref.shape)
        elif c == 3:
            vc_ref[...] = y.reshape(vc_ref.shape)
        else:
            z_ref[...] = y

    @pl.when(s < N_PROJ)
    def _():
        @pl.when(s == 0)
        def _():
            n_scr[...] = normed()

        w = w_ref[...].astype(BF16)
        w_res[s] = w
        y = _dot(n_scr[...], w)
        for c in range(N_PROJ):
            @pl.when(s == c)
            def _(c=c):
                store_prompt(c, y)

    @pl.when(s >= N_PROJ)
    def _():
        n = normed()
        col = lambda c: _dot(n, w_res[c])

        @pl.when(latent)
        def _():
            q_ref[...] = (_rope(col(1), cos_ref[...], sin_ref[...]) * scale).astype(BF16)
            ks_ref[...] = _rope(col(2), cos_ref[...], sin_ref[...]).astype(BF16)
            us_ref[...] = col(0)
            vs_ref[...] = col(3).astype(BF16)
            z_ref[...] = col(4)

        @pl.when(jnp.logical_not(latent))
        def _():
            for c in range(N_PROJ):
                store_prompt(c, col(c))


def _inproj(rows, x, pat, norm_g, w_in, layer, rope_tabs, cache_out):
    tiles_p = rows.tiles_p
    seqs_per_tile = ROW_TILE // rows.seq_p
    tabs_per_seq = rows.seq_s // ROW_TILE
    tile_of = lambda s: jnp.maximum(s - (N_PROJ - 1), 0)
    tab_spec = pl.BlockSpec((ROW_TILE, LANES), lambda s: (jnp.maximum(tile_of(s) - tiles_p, 0) % tabs_per_seq, 0))
    tile = (ROW_TILE, D_BRANCH)
    p_spec = pl.BlockSpec(tile, lambda s: (jnp.minimum(tile_of(s), tiles_p - 1), 0))
    s_spec = pl.BlockSpec(tile, lambda s: (jnp.maximum(tile_of(s) - tiles_p, 0), 0))
    all_spec = pl.BlockSpec(tile, lambda s: (tile_of(s), 0))
    cache_spec = pl.BlockSpec((seqs_per_tile, None, rows.seq_p, D_BRANCH),
                              lambda s: (jnp.minimum(tile_of(s), tiles_p - 1), layer, 0, 0))
    cache_shape = jax.ShapeDtypeStruct((rows.n_p, DEPTH, rows.seq_p, D_BRANCH), F32)
    in_specs = [
        pl.BlockSpec((ROW_TILE, D_MODEL), lambda s: (tile_of(s), 0)),
        pl.BlockSpec((None, N_MOD, D_MODEL), lambda s: (rows.mod_row(tile_of(s)), 0, 0)),
        pl.BlockSpec((1, D_MODEL), lambda s: (0, 0)),
        pl.BlockSpec((None, D_MODEL, D_BRANCH), lambda s: (layer, 0, jnp.minimum(s, N_PROJ - 1))),
        tab_spec, tab_spec,
    ]
    args = [x, pat, norm_g, w_in, *rope_tabs]
    aliases = {}
    if cache_out is not None:
        in_specs += [pl.BlockSpec(memory_space=pl.ANY)] * 2
        aliases = {len(args): 5, len(args) + 1: 6}
        args += list(cache_out)
    return pl.pallas_call(
        functools.partial(_inproj_kernel, tiles_p=tiles_p, n_alias=len(aliases)),
        grid=(N_PROJ - 1 + rows.n_tok // ROW_TILE,),
        in_specs=in_specs,
        out_specs=[p_spec, s_spec, all_spec, s_spec, s_spec, cache_spec, cache_spec, all_spec],
        out_shape=[
            jax.ShapeDtypeStruct((rows.tok_p, D_BRANCH), F32),
            jax.ShapeDtypeStruct((rows.tok_s, D_BRANCH), F32),
            jax.ShapeDtypeStruct((rows.n_tok, D_BRANCH), BF16),
            jax.ShapeDtypeStruct((rows.tok_s, D_BRANCH), BF16),
            jax.ShapeDtypeStruct((rows.tok_s, D_BRANCH), BF16),
            cache_shape, cache_shape,
            jax.ShapeDtypeStruct((rows.n_tok, D_BRANCH), F32),
        ],
        scratch_shapes=[pltpu.VMEM((N_PROJ, D_MODEL, D_BRANCH), BF16), pltpu.VMEM((ROW_TILE, D_MODEL), BF16)],
        input_output_aliases=aliases,
        compiler_params=_cparams(("arbitrary",)),
        name="inproj",
    )(*args)


def _s5_kernel(*refs, n_seq, n_steps, backward):
    if backward:
        u_ref, part_ref, w1_ref, w2_ref, w3_ref, ar_ref, ai_ref, h0_ref, wglu_ref, ya_ref, hfin_ref, xs_scr, h_scr = refs
    else:
        u_ref, dsk_ref, w1_ref, w2_ref, w3_ref, ar_ref, ai_ref, h0_ref, part_ref, hfin_ref, xs_scr, h_scr = refs
    rows = n_seq * n_steps
    pairs = n_steps // 2
    half = pairs * n_seq

    @pl.when(pl.program_id(0) == 0)
    def _():
        h_scr[...] = h0_ref[...]

    u = jnp.swapaxes(u_ref[...], 0, 1).reshape(rows, D_BRANCH)
    u4 = u.reshape(pairs, 2, n_seq, D_BRANCH)
    u_even = u4[:, 0].reshape(half, D_BRANCH).astype(BF16)
    u_odd = u4[:, 1].reshape(half, D_BRANCH).astype(BF16)
    u_first = u_odd if backward else u_even
    base = 0 if backward else n_seq
    incoming = half if backward else 0
    order = range(pairs - 1, -1, -1) if backward else range(pairs)

    def project(j):
        cols = slice(j * LANES, (j + 1) * LANES)
        lhs = jnp.concatenate([u_even[:, cols], u_odd[:, cols]], axis=1)
        xs_scr[j, base:base + half, :] = _dot(lhs, w1_ref[j])

    def scan(j):
        lanes = slice(j * SLAB_STATE, (j + 1) * SLAB_STATE)
        ar = jnp.broadcast_to(ar_ref[:, lanes], (n_seq, SLAB_STATE))
        ai = jnp.broadcast_to(ai_ref[:, lanes], (n_seq, SLAB_STATE))
        hr = h_scr[0, :, lanes]
        hi = h_scr[1, :, lanes]
        xs_scr[j, incoming:incoming + n_seq, 0:SLAB_STATE] = hr
        xs_scr[j, incoming:incoming + n_seq, SLAB_STATE:2 * SLAB_STATE] = hi
        for q in order:
            r = slice(base + q * n_seq, base + (q + 1) * n_seq)
            xr = xs_scr[j, r, 0:SLAB_STATE]
            xi = xs_scr[j, r, SLAB_STATE:2 * SLAB_STATE]
            hr, hi = ar * hr - ai * hi + xr, ar * hi + ai * hr + xi
            xs_scr[j, r, 0:SLAB_STATE] = hr
            xs_scr[j, r, SLAB_STATE:2 * SLAB_STATE] = hi
        h_scr[0, :, lanes] = hr
        h_scr[1, :, lanes] = hi

    def read_out(j):
        z = _dot(xs_scr[j].astype(BF16), w2_ref[j])
        direct = _dot(u_first[:, j * LANES:(j + 1) * LANES], w3_ref[j])
        lo = z[0:half]
        hi = z[n_seq:n_seq + half]
        if backward:
            even, odd = lo[:, 0:LANES], hi[:, LANES:2 * LANES] + direct
        else:
            even, odd = lo[:, LANES:2 * LANES] + direct, hi[:, 0:LANES]
        both = jnp.stack([even.reshape(pairs, n_seq, LANES), odd.reshape(pairs, n_seq, LANES)], axis=1)
        return both.reshape(rows, LANES)

    ys = []
    project(0)
    for j in range(N_SLAB):
        if j + 1 < N_SLAB:
            project(j + 1)
        scan(j)
        ys.append(read_out(j))
    y = jnp.concatenate(ys, axis=1)

    if backward:
        g = jax.nn.gelu(part_ref[...] + y)
        ya = g * jax.nn.sigmoid(_dot(g.astype(BF16), wglu_ref[...]))
        ya_ref[...] = jnp.swapaxes(ya.reshape(n_steps, n_seq, D_BRANCH), 0, 1).astype(BF16)
    else:
        part_ref[...] = y + dsk_ref[...] * u

    @pl.when(pl.program_id(0) == pl.num_programs(0) - 1)
    def _():
        hfin_ref[...] = h_scr[...]


def _s5(u, ssm, h0, n_seq):
    n_tok = u.shape[0]
    seq = n_tok // n_seq
    n_steps = S5_CHUNK_ROWS // n_seq
    n_chunks = seq // n_steps
    u3 = u.reshape(n_seq, seq, D_BRANCH)

    def specs(backward):
        d = int(backward)
        chunk = (lambda s: n_chunks - 1 - s) if backward else (lambda s: s)
        seq_block = pl.BlockSpec((n_seq, n_steps, D_BRANCH), lambda s: (0, chunk(s), 0))
        part_block = pl.BlockSpec((S5_CHUNK_ROWS, D_BRANCH), lambda s: (chunk(s), 0))
        params = [
            pl.BlockSpec((None, N_SLAB, 2 * LANES, 2 * SLAB_STATE), lambda s: (d, 0, 0, 0)),
            pl.BlockSpec((None, N_SLAB, 2 * SLAB_STATE, 2 * LANES), lambda s: (d, 0, 0, 0)),
            pl.BlockSpec((None, N_SLAB, LANES, LANES), lambda s: (d, 0, 0, 0)),
            pl.BlockSpec((None, 1, N_STATE), lambda s: (d, 0, 0)),
            pl.BlockSpec((None, 1, N_STATE), lambda s: (d, 0, 0)),
            pl.BlockSpec((None, 2, n_seq, N_STATE), lambda s: (d, 0, 0, 0)),
        ]
        return seq_block, part_block, params

    param_args = (ssm["w1"], ssm["w2"], ssm["w3"], ssm["a2_re"], ssm["a2_im"], h0)
    fin_spec = pl.BlockSpec((2, n_seq, N_STATE), lambda s: (0, 0, 0))
    fin_shape = jax.ShapeDtypeStruct((2, n_seq, N_STATE), F32)
    state_rows = S5_CHUNK_ROWS // 2 + n_seq
    scratch = [pltpu.VMEM((N_SLAB, state_rows, 2 * SLAB_STATE), F32), pltpu.VMEM((2, n_seq, N_STATE), F32)]

    seq_block, part_block, params = specs(False)
    part, fin_f = pl.pallas_call(
        functools.partial(_s5_kernel, n_seq=n_seq, n_steps=n_steps, backward=False),
        grid=(n_chunks,),
        in_specs=[seq_block, pl.BlockSpec((1, D_BRANCH), lambda s: (0, 0))] + params,
        out_specs=[part_block, fin_spec],
        out_shape=[jax.ShapeDtypeStruct((n_tok, D_BRANCH), F32), fin_shape],
        scratch_shapes=scratch,
        compiler_params=_cparams(("arbitrary",)),
        name="s5_fwd",
    )(u3, ssm["d_skip"], *param_args)

    seq_block, part_block, params = specs(True)
    ya, fin_b = pl.pallas_call(
        functools.partial(_s5_kernel, n_seq=n_seq, n_steps=n_steps, backward=True),
        grid=(n_chunks,),
        in_specs=[seq_block, part_block] + params + [pl.BlockSpec((D_BRANCH, D_BRANCH), lambda s: (0, 0))],
        out_specs=[seq_block, fin_spec],
        out_shape=[jax.ShapeDtypeStruct((n_seq, seq, D_BRANCH), BF16), fin_shape],
        scratch_shapes=scratch,
        compiler_params=_cparams(("arbitrary",)),
        name="s5_bwd",
    )(u3, part, *param_args, ssm["w_glu"])
    return ya.reshape(n_tok, D_BRANCH), jnp.stack([fin_f, fin_b], axis=0)


def _s5_params(lam_re, lam_im, log_dt, b_re, b_im, c_re, c_im):
    dt = jnp.exp(log_dt)[..., None]
    mag = jnp.exp(lam_re * dt)
    abr = mag * jnp.cos(lam_im * dt)
    abi = mag * jnp.sin(lam_im * dt)
    den = lam_re * lam_re + lam_im * lam_im
    nr = abr - 1.0
    kr = (nr * lam_re + abi * lam_im) / den
    ki = (abi * lam_re - nr * lam_im) / den
    bbr = kr[..., None] * b_re - ki[..., None] * b_im
    bbi = kr[..., None] * b_im + ki[..., None] * b_re
    per_slab = S5_GROUPS // N_SLAB
    eye = jnp.eye(per_slab, dtype=F32)

    def pack_in(w):
        w = w.reshape(2, N_SLAB, per_slab, S5_STATE, S5_GROUP)
        return jnp.einsum("djgpc,gh->djgchp", w, eye).reshape(2, N_SLAB, LANES, SLAB_STATE)

    def pack_out(w):
        w = w.reshape(2, N_SLAB, per_slab, S5_GROUP, S5_STATE)
        return jnp.einsum("djgcp,gh->djgphc", w, eye).reshape(2, N_SLAB, SLAB_STATE, LANES)

    abbr = abr[..., None] * bbr - abi[..., None] * bbi
    abbi = abr[..., None] * bbi + abi[..., None] * bbr
    car = c_re * abr[:, :, None, :] - c_im * abi[:, :, None, :]
    cai = c_re * abi[:, :, None, :] + c_im * abr[:, :, None, :]
    b_pack = jnp.concatenate([pack_in(bbr), pack_in(bbi)], axis=-1)
    ab_pack = jnp.concatenate([pack_in(abbr), pack_in(abbi)], axis=-1)
    w1 = jnp.stack([jnp.concatenate([ab_pack[0], b_pack[0]], axis=-2),
                    jnp.concatenate([b_pack[1], ab_pack[1]], axis=-2)]).astype(BF16)
    c_pack = jnp.concatenate([pack_out(c_re), -pack_out(c_im)], axis=-2)
    ca_pack = jnp.concatenate([pack_out(car), -pack_out(cai)], axis=-2)
    w2 = jnp.concatenate([c_pack, ca_pack], axis=-1).astype(BF16)
    direct = jnp.einsum("dgop,dgpi->dgoi", c_re, bbr) - jnp.einsum("dgop,dgpi->dgoi", c_im, bbi)
    direct = direct.reshape(2, N_SLAB, per_slab, S5_GROUP, S5_GROUP)
    w3 = jnp.einsum("djgoi,gh->djgiho", direct, eye).reshape(2, N_SLAB, LANES, LANES).astype(BF16)
    a2r = abr * abr - abi * abi
    a2i = 2.0 * abr * abi
    return w1, w2, w3, a2r.reshape(2, 1, N_STATE), a2i.reshape(2, 1, N_STATE)


def _attn_kernel(*refs, n_ctx, lam_init):
    if n_ctx:
        lam_ref, q_ref, k_ref, v_ref, ck_ref, cv_ref, g_ref, o_ref, k_scr, v_scr = refs
    else:
        lam_ref, q_ref, k_ref, v_ref, g_ref, o_ref, k_scr, v_scr = refs
    n_own = k_ref.shape[0]

    @pl.when(pl.program_id(1) == 0)
    def _():
        k_scr[0:n_own, :] = k_ref[...].astype(BF16)
        v_scr[0:n_own, :] = v_ref[...].astype(BF16)
        if n_ctx:
            k_scr[n_own:n_own + n_ctx, :] = ck_ref[...].astype(BF16)
            v_scr[n_own:n_own + n_ctx, :] = cv_ref[...].astype(BF16)

    lam = lam_ref[0]
    q = q_ref[...]
    lane = lax.broadcasted_iota(jnp.int32, (q.shape[0], LANES), 1)

    def scores(h):
        lanes = slice(h * LANES, (h + 1) * LANES)
        qh = q[:, lanes]
        kh = k_scr[:, lanes]
        out = []
        for m in range(2):
            qm = jnp.where((lane < DH) if m == 0 else (lane >= DH), qh, jnp.zeros_like(qh))
            out.append(lax.dot_general(qm, kh, (((1,), (1,)), ((), ())), preferred_element_type=F32))
        return out

    sc = scores(0)
    for h in range(N_HEADS):
        lanes = slice(h * LANES, (h + 1) * LANES)
        nxt = scores(h + 1) if h + 1 < N_HEADS else None
        e1, e2 = [jnp.exp2(s_m - jnp.max(s_m, axis=-1, keepdims=True)) for s_m in sc]
        l1 = jnp.sum(e1, axis=-1, keepdims=True)
        l2 = jnp.sum(e2, axis=-1, keepdims=True)
        a = (e1 - e2 * (lam * l1 / l2)).astype(BF16)
        o = _dot(a, v_scr[:, lanes]) * (1.0 / l1)
        ms = jnp.mean(o * o, axis=-1, keepdims=True)
        o_ref[:, lanes] = (((o * lax.rsqrt(ms + EPS)) * g_ref[...]) * (1.0 - lam_init)).astype(BF16)
        sc = nxt


def _attention(q, k, v, lam, attn_g, row0, n_seq, seq, lam_init, layer, cache=None):
    tq = min(Q_TILE, seq)
    q_tiles = seq // tq
    q0 = row0 // tq
    n_ctx = 0 if cache is None else cache[0].shape[2]
    if k.ndim == 4:
        own = pl.BlockSpec((None, None, seq, D_BRANCH), lambda b, i: (b, layer, 0, 0))
    else:
        own = pl.BlockSpec((seq, D_BRANCH), lambda b, i: (b, 0))
    in_specs = [
        pl.BlockSpec(memory_space=pltpu.SMEM),
        pl.BlockSpec((tq, D_BRANCH), lambda b, i: (q0 + b * q_tiles + i, 0)),
        own, own,
    ]
    args = [lam, q, k, v]
    if n_ctx:
        in_specs += [pl.BlockSpec((None, None, n_ctx, D_BRANCH), lambda b, i: (b, layer, 0, 0))] * 2
        args += list(cache)
    in_specs.append(pl.BlockSpec((1, DV), lambda b, i: (0, 0)))
    args.append(attn_g)
    return pl.pallas_call(
        functools.partial(_attn_kernel, n_ctx=n_ctx, lam_init=lam_init),
        grid=(n_seq, q_tiles),
        in_specs=in_specs,
        out_specs=pl.BlockSpec((tq, D_BRANCH), lambda b, i: (b * q_tiles + i, 0)),
        out_shape=jax.ShapeDtypeStruct((n_seq * seq, D_BRANCH), BF16),
        scratch_shapes=[pltpu.VMEM((seq + n_ctx, D_BRANCH), BF16)] * 2,
        compiler_params=_cparams(("parallel", "arbitrary")),
        name="attn",
    )(*args)


def _pool_kernel(z_ref, w_ref, sc_ref, o_ref, pad_scr):
    seq = z_ref.shape[0]
    padded = seq + 2 * POOL_PAD
    z = z_ref[...]
    zeros = jnp.zeros((POOL_PAD, D_BRANCH), F32)
    pad_scr[0:POOL_PAD, :] = zeros
    pad_scr[POOL_PAD:POOL_PAD + seq, :] = z
    pad_scr[POOL_PAD + seq:padded, :] = zeros
    t = lax.broadcasted_iota(jnp.int32, (seq, POOL_GROUP), 0)
    outs = []
    for gi, w in enumerate(POOL_WINDOWS):
        lanes = slice(gi * POOL_GROUP, (gi + 1) * POOL_GROUP)
        run = pad_scr[:, lanes]
        span = 1
        while span < w:
            run = run + pltpu.roll(run, padded - span, 0)
            span *= 2
        tot = pltpu.roll(run, w // 2, 0)[POOL_PAD:POOL_PAD + seq]
        cnt = jnp.minimum(t + w // 2, seq) - jnp.maximum(t - w // 2, 0)
        pooled = tot / cnt.astype(F32) - z[:, lanes]
        outs.append(_dot(pooled.astype(BF16), w_ref[gi]))
    o_ref[...] = (jnp.concatenate(outs, axis=1) * sc_ref[...]).astype(BF16)


def _pool(z, w_pool, pool_scale, row0, n_seq, seq):
    s0 = row0 // seq
    return pl.pallas_call(
        _pool_kernel,
        grid=(n_seq,),
        in_specs=[
            pl.BlockSpec((seq, D_BRANCH), lambda b: (s0 + b, 0)),
            pl.BlockSpec((len(POOL_WINDOWS), POOL_GROUP, POOL_GROUP), lambda b: (0, 0, 0)),
            pl.BlockSpec((1, D_BRANCH), lambda b: (0, 0)),
        ],
        out_specs=pl.BlockSpec((seq, D_BRANCH), lambda b: (b, 0)),
        out_shape=jax.ShapeDtypeStruct((n_seq * seq, D_BRANCH), BF16),
        scratch_shapes=[pltpu.VMEM((seq + 2 * POOL_PAD, D_BRANCH), F32)],
        compiler_params=_cparams(("parallel",)),
        name="pool",
    )(z, w_pool, pool_scale)


def _merge_kernel(x_ref, pat_ref, g_ref, *refs, tiles_p):
    branch_refs = refs[:2 * N_BRANCH]
    wg0_ref, wg1_ref, wbr_ref, wo_ref, o_ref, wg_res, wbr_res, wo_res, n_scr, m_scr = refs[2 * N_BRANCH:]
    s = pl.program_id(0)
    prompt = jnp.maximum(s - (N_BRANCH - 1), 0) < tiles_p
    x = x_ref[...]

    def normed():
        return _norm_mod(x, g_ref[...], pat_ref[3:4], pat_ref[4:5]).astype(BF16)

    def branch_in(br):
        return jnp.where(prompt, branch_refs[2 * br][...], branch_refs[2 * br + 1][...])

    def finish(merged):
        o_ref[...] = x + pat_ref[5:6] * _dot(merged.astype(BF16), wo_res[...])

    @pl.when(s < N_BRANCH)
    def _():
        @pl.when(s == 0)
        def _():
            n_scr[...] = normed()
            m_scr[...] = jnp.zeros_like(m_scr)
            wo_res[...] = wo_ref[...].astype(BF16)

        wg = jnp.concatenate([wg0_ref[...], wg1_ref[...]], axis=1).astype(BF16)
        wbr = wbr_ref[...].astype(BF16)
        wg_res[s] = wg
        wbr_res[s] = wbr
        gate = jax.nn.sigmoid(_dot(n_scr[...], wg))
        for br in range(N_BRANCH):
            @pl.when(s == br)
            def _(br=br):
                m_scr[...] += gate * _dot(branch_in(br), wbr)

        @pl.when(s == N_BRANCH - 1)
        def _():
            finish(m_scr[...])

    @pl.when(s >= N_BRANCH)
    def _():
        n = normed()
        merged = None
        for br in range(N_BRANCH):
            part = jax.nn.sigmoid(_dot(n, wg_res[br])) * _dot(branch_in(br), wbr_res[br])
            merged = part if merged is None else merged + part
        finish(merged)


def _merge(rows, x, pat, norm_g, branches, w_in, w_branch, w_out, layer):
    tiles_p = rows.tiles_p
    half = D_MODEL // 2
    gate0 = N_PROJ * D_BRANCH // half
    tile_of = lambda s: jnp.maximum(s - (N_BRANCH - 1), 0)
    br_of = lambda s: jnp.minimum(s, N_BRANCH - 1)
    p_spec = pl.BlockSpec((ROW_TILE, D_BRANCH), lambda s: (jnp.minimum(tile_of(s), tiles_p - 1), 0))
    s_spec = pl.BlockSpec((ROW_TILE, D_BRANCH), lambda s: (jnp.maximum(tile_of(s) - tiles_p, 0), 0))
    return pl.pallas_call(
        functools.partial(_merge_kernel, tiles_p=tiles_p),
        grid=(N_BRANCH - 1 + rows.n_tok // ROW_TILE,),
        in_specs=[
            pl.BlockSpec((ROW_TILE, D_MODEL), lambda s: (tile_of(s), 0)),
            pl.BlockSpec((None, N_MOD, D_MODEL), lambda s: (rows.mod_row(tile_of(s)), 0, 0)),
            pl.BlockSpec((1, D_MODEL), lambda s: (0, 0)),
            p_spec, s_spec, p_spec, s_spec, p_spec, s_spec,
            pl.BlockSpec((None, D_MODEL, half), lambda s: (layer, 0, gate0 + 2 * br_of(s))),
            pl.BlockSpec((None, D_MODEL, half), lambda s: (layer, 0, gate0 + 2 * br_of(s) + 1)),
            pl.BlockSpec((None, None, D_BRANCH, D_MODEL), lambda s: (layer, br_of(s), 0, 0)),
            _resident((None, D_MODEL, D_MODEL), lambda s: (layer, 0, 0)),
        ],
        out_specs=pl.BlockSpec((ROW_TILE, D_MODEL), lambda s: (tile_of(s), 0)),
        out_shape=jax.ShapeDtypeStruct((rows.n_tok, D_MODEL), F32),
        scratch_shapes=[
            pltpu.VMEM((N_BRANCH, D_MODEL, D_MODEL), BF16),
            pltpu.VMEM((N_BRANCH, D_BRANCH, D_MODEL), BF16),
            pltpu.VMEM((D_MODEL, D_MODEL), BF16),
            pltpu.VMEM((ROW_TILE, D_MODEL), BF16),
            pltpu.VMEM((ROW_TILE, D_MODEL), F32),
        ],
        compiler_params=_cparams(("arbitrary",)),
        name="merge",
    )(x, pat, norm_g, *[y for pair in branches for y in pair], w_in, w_in, w_branch, w_out)


def _rope_tables(seq):
    n_rows = seq // GRID_W
    row = jnp.repeat(jnp.arange(n_rows, dtype=F32), GRID_W)
    col = jnp.tile(jnp.arange(GRID_W, dtype=F32), n_rows)
    n_freq = DH // 4
    inv = ROPE_BASE ** (-jnp.arange(n_freq, dtype=F32) / n_freq)
    ar = row[:, None] * inv
    ac = col[:, None] * inv
    cos = jnp.concatenate([jnp.cos(ar)] * 2 + [jnp.cos(ac)] * 2, axis=1)
    sin = jnp.concatenate([-jnp.sin(ar), jnp.sin(ar), -jnp.sin(ac), jnp.sin(ac)], axis=1)
    return jnp.tile(cos, (1, LANES // DH)), jnp.tile(sin, (1, LANES // DH))


def kernel(x_prompt, x_sample, cache_k, cache_v, state_ssm, c, c_ctx, norm_g, w_mod, b_mod, w_ffn_in, w_ffn_out, w_in, ssm_lam_re, ssm_lam_im, ssm_log_dt, ssm_b_re, ssm_b_im, ssm_c_re, ssm_c_im, ssm_d, w_glu, lam_q1, lam_k1, lam_q2, lam_k2, attn_norm_g, w_pool, pool_scale, w_branch, w_out, final_norm_g):
    n_p, seq_p, _ = x_prompt.shape
    n_s, seq_s, _ = x_sample.shape
    n_past = cache_k.shape[2]
    rows = _Rows(n_p, seq_p, n_s, seq_s)
    tok_p = rows.tok_p

    cvec = jnp.concatenate([c_ctx[None, :], c, jnp.zeros((16 - 1 - n_s, D_MODEL), F32)], axis=0)
    pat = _adaln(cvec, w_mod, b_mod).reshape(DEPTH, 16, N_MOD, D_MODEL)

    w_gl = w_glu.astype(BF16)
    w_pl = w_pool.astype(BF16)

    x = (x_prompt.reshape(tok_p, D_MODEL), x_sample.reshape(rows.tok_s, D_MODEL))
    rope_tabs = _rope_tables(seq_s)
    cache = (cache_k.reshape(n_s, DEPTH, n_past, D_BRANCH), cache_v.reshape(n_s, DEPTH, n_past, D_BRANCH))
    h0_p = jnp.zeros((2, 2, n_p, N_STATE), F32)

    new_cache, new_s = None, []
    for l in range(DEPTH):
        lam_init = 0.8 - 0.6 * math.exp(-0.3 * l)
        lam = (jnp.exp(jnp.sum(lam_q1[l] * lam_k1[l])) - jnp.exp(jnp.sum(lam_q2[l] * lam_k2[l])) + lam_init).reshape(1)
        w1, w2, w3, a2_re, a2_im = _s5_params(ssm_lam_re[l], ssm_lam_im[l], ssm_log_dt[l], ssm_b_re[l], ssm_b_im[l],
                                              ssm_c_re[l], ssm_c_im[l])
        ssm = dict(w1=w1, w2=w2, w3=w3, a2_re=a2_re, a2_im=a2_im, d_skip=ssm_d[l][None, :], w_glu=w_gl[l])
        ng = norm_g[l]
        attn_g = attn_norm_g[l][None, :]
        scale_c = pool_scale[l][None, :]
        h0_s = state_ssm[:, l].reshape(n_s, 2, 2, N_STATE).transpose(1, 2, 0, 3)

        x = _ffn(rows, x, pat[l], ng[0:1], w_ffn_in, w_ffn_out, l, 0, 0)
        u_p, u_s, q, k_s, v_s, k_new, v_new, z = _inproj(rows, x, pat[l], ng[1:2], w_in, l, rope_tabs, new_cache)
        new_cache = (k_new, v_new)

        ya_p, h_fin = _s5(u_p, ssm, h0_p, n_p)
        ya_s, _ = _s5(u_s, ssm, h0_s, n_s)
        yb = (_attention(q, k_new, v_new, lam, attn_g, 0, n_p, seq_p, lam_init, l),
              _attention(q, k_s, v_s, lam, attn_g, tok_p, n_s, seq_s, lam_init, l, cache))
        yc = (_pool(z, w_pl[l], scale_c, 0, n_p, seq_p), _pool(z, w_pl[l], scale_c, tok_p, n_s, seq_s))

        x = _merge(rows, x, pat[l], ng[1:2], ((ya_p, ya_s), yb, yc), w_in, w_branch, w_out, l)
        last = l == DEPTH - 1
        x = _ffn(rows, x, pat[l], ng[2:3], w_ffn_in, w_ffn_out, l, 1, 6, final_norm_g[None, :] if last else None)
        new_s.append(h_fin.transpose(2, 0, 1, 3).reshape(n_p, 2, 2, S5_GROUPS, S5_STATE))

    y_p, y_s = x
    return (y_p.reshape(n_p, seq_p, D_MODEL), y_s.reshape(n_s, seq_s, D_MODEL),
            new_cache[0].reshape(n_p, DEPTH, seq_p, N_HEADS, 2, DH), new_cache[1].reshape(n_p, DEPTH, seq_p, N_HEADS, DV),
            jnp.stack(new_s, axis=1))
```

```python
import functools
import math

import jax
import jax.numpy as jnp
from jax import lax
from jax.experimental import pallas as pl
from jax.experimental.pallas import tpu as pltpu

F32 = jnp.float32
BF16 = jnp.bfloat16

D_MODEL = 1024
DEPTH = 4
GRID_W = 64
D_BRANCH = 512
S5_GROUP = 16
S5_GROUPS = 32
S5_STATE = 64
N_STATE = S5_GROUPS * S5_STATE
DH = 64
N_HEADS = 4
DV = 128
POOL_WINDOWS = (2, 4, 8, 16)
POOL_GROUP = 128
POOL_PAD = 16
D_FF = 2816
N_MOD = 9
N_BRANCH = 3
N_PROJ = 5
ROPE_BASE = 10000.0
EPS = 1e-6

LANES = 128
N_SLAB = D_BRANCH // LANES
SLAB_STATE = N_STATE // N_SLAB
ROW_TILE = 512
FF_CHUNK = 256
S5_CHUNK_ROWS = 1024
Q_TILE = 512
VMEM_LIMIT = 56 * 1024 * 1024


def _cparams(sem):
    return pltpu.CompilerParams(dimension_semantics=sem, vmem_limit_bytes=VMEM_LIMIT)


def _resident(block_shape, index_map):
    return pl.BlockSpec(block_shape, index_map, pipeline_mode=pl.Buffered(1))


def _norm_mod(x, g, shift, scale):
    ms = jnp.mean(x * x, axis=-1, keepdims=True)
    y = (x * lax.rsqrt(ms + EPS)) * g
    return y * (1.0 + scale) + shift


def _dot(a, b):
    return jnp.dot(a, b, preferred_element_type=F32)


class _Rows:
    def __init__(self, n_p, seq_p, n_s, seq_s):
        self.n_p, self.seq_p, self.n_s, self.seq_s = n_p, seq_p, n_s, seq_s
        self.tok_p = n_p * seq_p
        self.tok_s = n_s * seq_s
        self.n_tok = self.tok_p + self.tok_s
        assert self.tok_p % ROW_TILE == 0 and seq_s % ROW_TILE == 0
        self.tiles_p = self.tok_p // ROW_TILE

    def mod_row(self, i):
        return jnp.where(i < self.tiles_p, 0, 1 + (i - self.tiles_p) // (self.seq_s // ROW_TILE))


def _adaln_kernel(c_ref, w_ref, b_ref, o_ref):
    c = c_ref[...]
    s = (c * jax.nn.sigmoid(c)).astype(BF16)
    o_ref[...] = _dot(s, w_ref[...].astype(BF16)) + b_ref[...]


def _adaln(cvec, w_mod, b_mod):
    tn = 1024
    n_rows = cvec.shape[0]
    return pl.pallas_call(
        _adaln_kernel,
        grid=(DEPTH, N_MOD * D_MODEL // tn),
        in_specs=[
            pl.BlockSpec((n_rows, D_MODEL), lambda l, j: (0, 0)),
            pl.BlockSpec((None, D_MODEL, tn), lambda l, j: (l, 0, j)),
            pl.BlockSpec((None, 1, tn), lambda l, j: (l, 0, j)),
        ],
        out_specs=pl.BlockSpec((None, n_rows, tn), lambda l, j: (l, 0, j)),
        out_shape=jax.ShapeDtypeStruct((DEPTH, n_rows, N_MOD * D_MODEL), F32),
        compiler_params=_cparams(("parallel", "parallel")),
        name="adaln",
    )(cvec, w_mod, b_mod.reshape(DEPTH, 1, N_MOD * D_MODEL))


def _ffn_kernel(*refs, k0, tiles_p, n_chunks, split_in, final):
    refs = list(refs)
    s = pl.program_id(0)
    tile = jnp.maximum(s - (n_chunks - 1), 0)
    prompt = tile < tiles_p
    if split_in:
        xp_ref, xs_ref = refs[:2]
        refs = refs[2:]
        x = jnp.where(prompt, xp_ref[...], xs_ref[...])
    else:
        x = refs.pop(0)[...]
    pat_ref, g_ref, wa_ref, wb_ref, wo_ref = refs[:5]
    refs = refs[5:]
    wa_res, wb_res, wo_res, n_scr, acc_scr, act_scr = refs[-6:]
    refs = refs[:-6]

    def normed():
        return _norm_mod(x, g_ref[...], pat_ref[k0:k0 + 1], pat_ref[k0 + 1:k0 + 2]).astype(BF16)

    def finish(y):
        out = x + (0.5 * pat_ref[k0 + 2:k0 + 3]) * y
        if not final:
            refs[0][...] = out
            return
        fg_ref, op_ref, os_ref = refs
        ms = jnp.mean(out * out, axis=-1, keepdims=True)
        out = (out * lax.rsqrt(ms + EPS)) * fg_ref[...]

        @pl.when(prompt)
        def _():
            op_ref[...] = out

        @pl.when(jnp.logical_not(prompt))
        def _():
            os_ref[...] = out

    @pl.when(s < n_chunks)
    def _():
        @pl.when(s == 0)
        def _():
            n_scr[...] = normed()
            acc_scr[...] = jnp.zeros_like(acc_scr)

        wa = wa_ref[...].astype(BF16)
        wb = wb_ref[...].astype(BF16)
        wo = wo_ref[...].astype(BF16)
        wa_res[s] = wa
        wb_res[s] = wb
        wo_res[pl.ds(pl.multiple_of(s * FF_CHUNK, FF_CHUNK), FF_CHUNK), :] = wo
        n = n_scr[...]
        a = _dot(n, wa)
        b = _dot(n, wb)
        acc_scr[...] += _dot((a * jax.nn.sigmoid(a) * b).astype(BF16), wo)

        @pl.when(s == n_chunks - 1)
        def _():
            finish(acc_scr[...])

    @pl.when(s >= n_chunks)
    def _():
        n = normed()
        for j in range(n_chunks):
            a = _dot(n, wa_res[j])
            b = _dot(n, wb_res[j])
            act_scr[:, j * FF_CHUNK:(j + 1) * FF_CHUNK] = (a * jax.nn.sigmoid(a) * b).astype(BF16)
        finish(_dot(act_scr[...], wo_res[...]))


def _ffn(rows, x, pat, norm_g, w_ffn_in, w_ffn_out, layer, sub, k0, final_g=None):
    split_in = isinstance(x, tuple)
    final = final_g is not None
    n_chunks = D_FF // FF_CHUNK
    tiles_p = rows.tiles_p
    tile_of = lambda s: jnp.maximum(s - (n_chunks - 1), 0)
    chunk_of = lambda s: jnp.minimum(s, n_chunks - 1)
    tile = (ROW_TILE, D_MODEL)
    p_spec = pl.BlockSpec(tile, lambda s: (jnp.minimum(tile_of(s), tiles_p - 1), 0))
    s_spec = pl.BlockSpec(tile, lambda s: (jnp.maximum(tile_of(s) - tiles_p, 0), 0))
    all_spec = pl.BlockSpec(tile, lambda s: (tile_of(s), 0))
    in_specs = ([p_spec, s_spec] if split_in else [all_spec]) + [
        pl.BlockSpec((None, None, N_MOD, D_MODEL), lambda s: (layer, rows.mod_row(tile_of(s)), 0, 0)),
        pl.BlockSpec((None, None, 1, D_MODEL), lambda s: (layer, 2 * sub, 0, 0)),
        pl.BlockSpec((None, None, D_MODEL, FF_CHUNK), lambda s: (layer, sub, 0, chunk_of(s))),
        pl.BlockSpec((None, None, D_MODEL, FF_CHUNK), lambda s: (layer, sub, 0, n_chunks + chunk_of(s))),
        pl.BlockSpec((None, None, FF_CHUNK, D_MODEL), lambda s: (layer, sub, chunk_of(s), 0)),
    ]
    args = (list(x) if split_in else [x]) + [pat, norm_g, w_ffn_in, w_ffn_in, w_ffn_out]
    if final:
        in_specs.append(pl.BlockSpec((1, D_MODEL), lambda s: (0, 0)))
        args.append(final_g)
        out_specs = [p_spec, s_spec]
        out_shape = [jax.ShapeDtypeStruct((rows.tok_p, D_MODEL), F32), jax.ShapeDtypeStruct((rows.tok_s, D_MODEL), F32)]
    else:
        out_specs = all_spec
        out_shape = jax.ShapeDtypeStruct((rows.n_tok, D_MODEL), F32)
    return pl.pallas_call(
        functools.partial(_ffn_kernel, k0=k0, tiles_p=tiles_p, n_chunks=n_chunks, split_in=split_in, final=final),
        grid=(n_chunks - 1 + rows.n_tok // ROW_TILE,),
        in_specs=in_specs,
        out_specs=out_specs,
        out_shape=out_shape,
        scratch_shapes=[
            pltpu.VMEM((n_chunks, D_MODEL, FF_CHUNK), BF16),
            pltpu.VMEM((n_chunks, D_MODEL, FF_CHUNK), BF16),
            pltpu.VMEM((D_FF, D_MODEL), BF16),
            pltpu.VMEM((ROW_TILE, D_MODEL), BF16),
            pltpu.VMEM((ROW_TILE, D_MODEL), F32),
            pltpu.VMEM((ROW_TILE, D_FF), BF16),
        ],
        compiler_params=_cparams(("arbitrary",)),
        name="ffn",
    )(*args)


def _rope(x, cos, sin):
    lane = lax.broadcasted_iota(jnp.int32, (x.shape[0], LANES), 1)
    first_half = (lane % 32) < 16
    out = []
    for c in range(x.shape[1] // LANES):
        xc = x[:, c * LANES:(c + 1) * LANES]
        partner = jnp.where(first_half, pltpu.roll(xc, LANES - 16, 1), pltpu.roll(xc, 16, 1))
        out.append(xc * cos + partner * sin)
    return jnp.concatenate(out, axis=1)


def _inproj_kernel(*refs, tiles_p, n_alias):
    x_ref, pat_ref, g_ref, w_ref, cos_ref, sin_ref = refs[:6]
    up_ref, us_ref, q_ref, ks_ref, vs_ref, kc_ref, vc_ref, z_ref, w_res, n_scr = refs[6 + n_alias:]
    s = pl.program_id(0)
    latent = jnp.maximum(s - (N_PROJ - 1), 0) >= tiles_p
    scale = DH ** -0.5 * math.log2(math.e)

    def normed():
        return _norm_mod(x_ref[...], g_ref[...], pat_ref[3:4], pat_ref[4:5]).astype(BF16)

    def store_prompt(c, y):
        if c == 0:
            up_ref[...] = y
        elif c == 1:
            q_ref[...] = (y * scale).astype(BF16)
        elif c == 2:
            kc_ref[...] = y.reshape(kc_ref.shape)
        elif c == 3:
            vc_ref[...] = y.reshape(vc_ref.shape)
        else:
            z_ref[...] = y

    @pl.when(s < N_PROJ)
    def _():
        @pl.when(s == 0)
        def _():
            n_scr[...] = normed()

        w = w_ref[...].astype(BF16)
        w_res[s] = w
        y = _dot(n_scr[...], w)
        for c in range(N_PROJ):
            @pl.when(s == c)
            def _(c=c):
                store_prompt(c, y)

    @pl.when(s >= N_PROJ)
    def _():
        n = normed()
        col = lambda c: _dot(n, w_res[c])

        @pl.when(latent)
        def _():
            q_ref[...] = (_rope(col(1), cos_ref[...], sin_ref[...]) * scale).astype(BF16)
            ks_ref[...] = _rope(col(2), cos_ref[...], sin_ref[...]).astype(BF16)
            us_ref[...] = col(0)
            vs_ref[...] = col(3).astype(BF16)
            z_ref[...] = col(4)

        @pl.when(jnp.logical_not(latent))
        def _():
            for c in range(N_PROJ):
                store_prompt(c, col(c))


def _inproj(rows, x, pat, norm_g, w_in, layer, rope_tabs, cache_out):
    tiles_p = rows.tiles_p
    seqs_per_tile = ROW_TILE // rows.seq_p
    tabs_per_seq = rows.seq_s // ROW_TILE
    tile_of = lambda s: jnp.maximum(s - (N_PROJ - 1), 0)
    tab_spec = pl.BlockSpec((ROW_TILE, LANES), lambda s: (jnp.maximum(tile_of(s) - tiles_p, 0) % tabs_per_seq, 0))
    tile = (ROW_TILE, D_BRANCH)
    p_spec = pl.BlockSpec(tile, lambda s: (jnp.minimum(tile_of(s), tiles_p - 1), 0))
    s_spec = pl.BlockSpec(tile, lambda s: (jnp.maximum(tile_of(s) - tiles_p, 0), 0))
    all_spec = pl.BlockSpec(tile, lambda s: (tile_of(s), 0))
    cache_spec = pl.BlockSpec((seqs_per_tile, None, rows.seq_p, D_BRANCH),
                              lambda s: (jnp.minimum(tile_of(s), tiles_p - 1), layer, 0, 0))
    cache_shape = jax.ShapeDtypeStruct((rows.n_p, DEPTH, rows.seq_p, D_BRANCH), F32)
    in_specs = [
        pl.BlockSpec((ROW_TILE, D_MODEL), lambda s: (tile_of(s), 0)),
        pl.BlockSpec((None, None, N_MOD, D_MODEL), lambda s: (layer, rows.mod_row(tile_of(s)), 0, 0)),
        pl.BlockSpec((None, None, 1, D_MODEL), lambda s: (layer, 1, 0, 0)),
        pl.BlockSpec((None, D_MODEL, D_BRANCH), lambda s: (layer, 0, jnp.minimum(s, N_PROJ - 1))),
        tab_spec, tab_spec,
    ]
    args = [x, pat, norm_g, w_in, *rope_tabs]
    aliases = {}
    if cache_out is not None:
        in_specs += [pl.BlockSpec(memory_space=pl.ANY)] * 2
        aliases = {len(args): 5, len(args) + 1: 6}
        args += list(cache_out)
    return pl.pallas_call(
        functools.partial(_inproj_kernel, tiles_p=tiles_p, n_alias=len(aliases)),
        grid=(N_PROJ - 1 + rows.n_tok // ROW_TILE,),
        in_specs=in_specs,
        out_specs=[p_spec, s_spec, all_spec, s_spec, s_spec, cache_spec, cache_spec, all_spec],
        out_shape=[
            jax.ShapeDtypeStruct((rows.tok_p, D_BRANCH), F32),
            jax.ShapeDtypeStruct((rows.tok_s, D_BRANCH), F32),
            jax.ShapeDtypeStruct((rows.n_tok, D_BRANCH), BF16),
            jax.ShapeDtypeStruct((rows.tok_s, D_BRANCH), BF16),
            jax.ShapeDtypeStruct((rows.tok_s, D_BRANCH), BF16),
            cache_shape, cache_shape,
            jax.ShapeDtypeStruct((rows.n_tok, D_BRANCH), F32),
        ],
        scratch_shapes=[pltpu.VMEM((N_PROJ, D_MODEL, D_BRANCH), BF16), pltpu.VMEM((ROW_TILE, D_MODEL), BF16)],
        input_output_aliases=aliases,
        compiler_params=_cparams(("arbitrary",)),
        name="inproj",
    )(*args)


def _s5_kernel(*refs, n_seq, n_steps, backward):
    if backward:
        u_ref, part_ref, w1_ref, w2_ref, w3_ref, ar_ref, ai_ref, h0_ref, wglu_ref, ya_ref, hfin_ref, xs_scr, h_scr = refs
    else:
        u_ref, dsk_ref, w1_ref, w2_ref, w3_ref, ar_ref, ai_ref, h0_ref, part_ref, hfin_ref, xs_scr, h_scr = refs
    rows = n_seq * n_steps
    pairs = n_steps // 2
    half = pairs * n_seq

    @pl.when(pl.program_id(0) == 0)
    def _():
        h_scr[...] = h0_ref[...]

    u = jnp.swapaxes(u_ref[...], 0, 1).reshape(rows, D_BRANCH)
    u4 = u.reshape(pairs, 2, n_seq, D_BRANCH)
    u_even = u4[:, 0].reshape(half, D_BRANCH).astype(BF16)
    u_odd = u4[:, 1].reshape(half, D_BRANCH).astype(BF16)
    u_first = u_odd if backward else u_even
    base = 0 if backward else n_seq
    incoming = half if backward else 0
    order = range(pairs - 1, -1, -1) if backward else range(pairs)

    def project(j):
        cols = slice(j * LANES, (j + 1) * LANES)
        lhs = jnp.concatenate([u_even[:, cols], u_odd[:, cols]], axis=1)
        xs_scr[j, base:base + half, :] = _dot(lhs, w1_ref[j])

    def scan(j):
        lanes = slice(j * SLAB_STATE, (j + 1) * SLAB_STATE)
        ar = jnp.broadcast_to(ar_ref[:, lanes], (n_seq, SLAB_STATE))
        ai = jnp.broadcast_to(ai_ref[:, lanes], (n_seq, SLAB_STATE))
        hr = h_scr[0, :, lanes]
        hi = h_scr[1, :, lanes]
        xs_scr[j, incoming:incoming + n_seq, 0:SLAB_STATE] = hr
        xs_scr[j, incoming:incoming + n_seq, SLAB_STATE:2 * SLAB_STATE] = hi
        for q in order:
            r = slice(base + q * n_seq, base + (q + 1) * n_seq)
            xr = xs_scr[j, r, 0:SLAB_STATE]
            xi = xs_scr[j, r, SLAB_STATE:2 * SLAB_STATE]
            hr, hi = ar * hr - ai * hi + xr, ar * hi + ai * hr + xi
            xs_scr[j, r, 0:SLAB_STATE] = hr
            xs_scr[j, r, SLAB_STATE:2 * SLAB_STATE] = hi
        h_scr[0, :, lanes] = hr
        h_scr[1, :, lanes] = hi

    def read_out(j):
        z = _dot(xs_scr[j].astype(BF16), w2_ref[j])
        direct = _dot(u_first[:, j * LANES:(j + 1) * LANES], w3_ref[j])
        lo = z[0:half]
        hi = z[n_seq:n_seq + half]
        if backward:
            even, odd = lo[:, 0:LANES], hi[:, LANES:2 * LANES] + direct
        else:
            even, odd = lo[:, LANES:2 * LANES] + direct, hi[:, 0:LANES]
        both = jnp.stack([even.reshape(pairs, n_seq, LANES), odd.reshape(pairs, n_seq, LANES)], axis=1)
        return both.reshape(rows, LANES)

    ys = []
    project(0)
    for j in range(N_SLAB):
        if j + 1 < N_SLAB:
            project(j + 1)
        scan(j)
        ys.append(read_out(j))
    y = jnp.concatenate(ys, axis=1)

    if backward:
        g = jax.nn.gelu(part_ref[...] + y)
        ya = g * jax.nn.sigmoid(_dot(g.astype(BF16), wglu_ref[...]))
        ya_ref[...] = jnp.swapaxes(ya.reshape(n_steps, n_seq, D_BRANCH), 0, 1).astype(BF16)
    else:
        part_ref[...] = y + dsk_ref[...] * u

    @pl.when(pl.program_id(0) == pl.num_programs(0) - 1)
    def _():
        hfin_ref[...] = h_scr[...]


def _s5(u, ssm, h0, h0_row, n_seq, layer):
    n_tok = u.shape[0]
    seq = n_tok // n_seq
    n_steps = S5_CHUNK_ROWS // n_seq
    n_chunks = seq // n_steps
    u3 = u.reshape(n_seq, seq, D_BRANCH)

    def specs(backward):
        d = int(backward)
        chunk = (lambda s: n_chunks - 1 - s) if backward else (lambda s: s)
        seq_block = pl.BlockSpec((n_seq, n_steps, D_BRANCH), lambda s: (0, chunk(s), 0))
        part_block = pl.BlockSpec((S5_CHUNK_ROWS, D_BRANCH), lambda s: (chunk(s), 0))
        params = [
            pl.BlockSpec((None, None, N_SLAB, 2 * LANES, 2 * SLAB_STATE), lambda s: (layer, d, 0, 0, 0)),
            pl.BlockSpec((None, None, N_SLAB, 2 * SLAB_STATE, 2 * LANES), lambda s: (layer, d, 0, 0, 0)),
            pl.BlockSpec((None, None, N_SLAB, LANES, LANES), lambda s: (layer, d, 0, 0, 0)),
            pl.BlockSpec((None, None, 1, N_STATE), lambda s: (layer, d, 0, 0)),
            pl.BlockSpec((None, None, 1, N_STATE), lambda s: (layer, d, 0, 0)),
            pl.BlockSpec((None, None, 2, n_seq, N_STATE), lambda s: (h0_row, d, 0, 0, 0)),
        ]
        return seq_block, part_block, params

    param_args = (ssm["w1"], ssm["w2"], ssm["w3"], ssm["a2_re"], ssm["a2_im"], h0)
    fin_spec = pl.BlockSpec((2, n_seq, N_STATE), lambda s: (0, 0, 0))
    fin_shape = jax.ShapeDtypeStruct((2, n_seq, N_STATE), F32)
    state_rows = S5_CHUNK_ROWS // 2 + n_seq
    scratch = [pltpu.VMEM((N_SLAB, state_rows, 2 * SLAB_STATE), F32), pltpu.VMEM((2, n_seq, N_STATE), F32)]

    seq_block, part_block, params = specs(False)
    part, fin_f = pl.pallas_call(
        functools.partial(_s5_kernel, n_seq=n_seq, n_steps=n_steps, backward=False),
        grid=(n_chunks,),
        in_specs=[seq_block, pl.BlockSpec((None, 1, D_BRANCH), lambda s: (layer, 0, 0))] + params,
        out_specs=[part_block, fin_spec],
        out_shape=[jax.ShapeDtypeStruct((n_tok, D_BRANCH), F32), fin_shape],
        scratch_shapes=scratch,
        compiler_params=_cparams(("arbitrary",)),
        name="s5_fwd",
    )(u3, ssm["d_skip"], *param_args)

    seq_block, part_block, params = specs(True)
    ya, fin_b = pl.pallas_call(
        functools.partial(_s5_kernel, n_seq=n_seq, n_steps=n_steps, backward=True),
        grid=(n_chunks,),
        in_specs=[seq_block, part_block] + params + [pl.BlockSpec((None, D_BRANCH, D_BRANCH), lambda s: (layer, 0, 0))],
        out_specs=[seq_block, fin_spec],
        out_shape=[jax.ShapeDtypeStruct((n_seq, seq, D_BRANCH), BF16), fin_shape],
        scratch_shapes=scratch,
        compiler_params=_cparams(("arbitrary",)),
        name="s5_bwd",
    )(u3, part, *param_args, ssm["w_glu"])
    return ya.reshape(n_tok, D_BRANCH), jnp.stack([fin_f, fin_b], axis=0)


def _s5_params(lam_re, lam_im, log_dt, b_re, b_im, c_re, c_im):
    dt = jnp.exp(log_dt)[..., None]
    mag = jnp.exp(lam_re * dt)
    abr = mag * jnp.cos(lam_im * dt)
    abi = mag * jnp.sin(lam_im * dt)
    den = lam_re * lam_re + lam_im * lam_im
    nr = abr - 1.0
    kr = (nr * lam_re + abi * lam_im) / den
    ki = (abi * lam_re - nr * lam_im) / den
    bbr = kr[..., None] * b_re - ki[..., None] * b_im
    bbi = kr[..., None] * b_im + ki[..., None] * b_re
    per_slab = S5_GROUPS // N_SLAB
    eye = jnp.eye(per_slab, dtype=F32)

    def pack_in(w):
        w = w.reshape(2, N_SLAB, per_slab, S5_STATE, S5_GROUP)
        return jnp.einsum("djgpc,gh->djgchp", w, eye).reshape(2, N_SLAB, LANES, SLAB_STATE)

    def pack_out(w):
        w = w.reshape(2, N_SLAB, per_slab, S5_GROUP, S5_STATE)
        return jnp.einsum("djgcp,gh->djgphc", w, eye).reshape(2, N_SLAB, SLAB_STATE, LANES)

    abbr = abr[..., None] * bbr - abi[..., None] * bbi
    abbi = abr[..., None] * bbi + abi[..., None] * bbr
    car = c_re * abr[:, :, None, :] - c_im * abi[:, :, None, :]
    cai = c_re * abi[:, :, None, :] + c_im * abr[:, :, None, :]
    b_pack = jnp.concatenate([pack_in(bbr), pack_in(bbi)], axis=-1)
    ab_pack = jnp.concatenate([pack_in(abbr), pack_in(abbi)], axis=-1)
    w1 = jnp.stack([jnp.concatenate([ab_pack[0], b_pack[0]], axis=-2),
                    jnp.concatenate([b_pack[1], ab_pack[1]], axis=-2)]).astype(BF16)
    c_pack = jnp.concatenate([pack_out(c_re), -pack_out(c_im)], axis=-2)
    ca_pack = jnp.concatenate([pack_out(car), -pack_out(cai)], axis=-2)
    w2 = jnp.concatenate([c_pack, ca_pack], axis=-1).astype(BF16)
    direct = jnp.einsum("dgop,dgpi->dgoi", c_re, bbr) - jnp.einsum("dgop,dgpi->dgoi", c_im, bbi)
    direct = direct.reshape(2, N_SLAB, per_slab, S5_GROUP, S5_GROUP)
    w3 = jnp.einsum("djgoi,gh->djgiho", direct, eye).reshape(2, N_SLAB, LANES, LANES).astype(BF16)
    a2r = abr * abr - abi * abi
    a2i = 2.0 * abr * abi
    return dict(w1=w1, w2=w2, w3=w3, a2_re=a2r.reshape(2, 1, N_STATE), a2_im=a2i.reshape(2, 1, N_STATE))


def _attn_kernel(*refs, n_ctx, lam_init, layer):
    if n_ctx:
        lam_ref, q_ref, k_ref, v_ref, ck_ref, cv_ref, g_ref, o_ref, k_scr, v_scr = refs
    else:
        lam_ref, q_ref, k_ref, v_ref, g_ref, o_ref, k_scr, v_scr = refs
    n_own = k_ref.shape[0]

    @pl.when(pl.program_id(1) == 0)
    def _():
        k_scr[0:n_own, :] = k_ref[...].astype(BF16)
        v_scr[0:n_own, :] = v_ref[...].astype(BF16)
        if n_ctx:
            k_scr[n_own:n_own + n_ctx, :] = ck_ref[...].astype(BF16)
            v_scr[n_own:n_own + n_ctx, :] = cv_ref[...].astype(BF16)

    lam = lam_ref[layer]
    q = q_ref[...]
    lane = lax.broadcasted_iota(jnp.int32, (q.shape[0], LANES), 1)

    def scores(h):
        lanes = slice(h * LANES, (h + 1) * LANES)
        qh = q[:, lanes]
        kh = k_scr[:, lanes]
        out = []
        for m in range(2):
            qm = jnp.where((lane < DH) if m == 0 else (lane >= DH), qh, jnp.zeros_like(qh))
            out.append(lax.dot_general(qm, kh, (((1,), (1,)), ((), ())), preferred_element_type=F32))
        return out

    sc = scores(0)
    for h in range(N_HEADS):
        lanes = slice(h * LANES, (h + 1) * LANES)
        nxt = scores(h + 1) if h + 1 < N_HEADS else None
        e1, e2 = [jnp.exp2(s_m - jnp.max(s_m, axis=-1, keepdims=True)) for s_m in sc]
        l1 = jnp.sum(e1, axis=-1, keepdims=True)
        l2 = jnp.sum(e2, axis=-1, keepdims=True)
        a = (e1 - e2 * (lam * l1 / l2)).astype(BF16)
        o = _dot(a, v_scr[:, lanes]) * (1.0 / l1)
        ms = jnp.mean(o * o, axis=-1, keepdims=True)
        o_ref[:, lanes] = (((o * lax.rsqrt(ms + EPS)) * g_ref[...]) * (1.0 - lam_init)).astype(BF16)
        sc = nxt


def _attention(q, k, v, lam, attn_g, row0, n_seq, seq, lam_init, layer, cache=None):
    tq = min(Q_TILE, seq)
    q_tiles = seq // tq
    q0 = row0 // tq
    n_ctx = 0 if cache is None else cache[0].shape[2]
    if k.ndim == 4:
        own = pl.BlockSpec((None, None, seq, D_BRANCH), lambda b, i: (b, layer, 0, 0))
    else:
        own = pl.BlockSpec((seq, D_BRANCH), lambda b, i: (b, 0))
    in_specs = [
        pl.BlockSpec(memory_space=pltpu.SMEM),
        pl.BlockSpec((tq, D_BRANCH), lambda b, i: (q0 + b * q_tiles + i, 0)),
        own, own,
    ]
    args = [lam, q, k, v]
    if n_ctx:
        in_specs += [pl.BlockSpec((None, None, n_ctx, D_BRANCH), lambda b, i: (b, layer, 0, 0))] * 2
        args += list(cache)
    in_specs.append(pl.BlockSpec((None, 1, DV), lambda b, i: (layer, 0, 0)))
    args.append(attn_g)
    return pl.pallas_call(
        functools.partial(_attn_kernel, n_ctx=n_ctx, lam_init=lam_init, layer=layer),
        grid=(n_seq, q_tiles),
        in_specs=in_specs,
        out_specs=pl.BlockSpec((tq, D_BRANCH), lambda b, i: (b * q_tiles + i, 0)),
        out_shape=jax.ShapeDtypeStruct((n_seq * seq, D_BRANCH), BF16),
        scratch_shapes=[pltpu.VMEM((seq + n_ctx, D_BRANCH), BF16)] * 2,
        compiler_params=_cparams(("parallel", "arbitrary")),
        name="attn",
    )(*args)


def _pool_kernel(z_ref, w_ref, sc_ref, o_ref, pad_scr):
    seq = z_ref.shape[0]
    padded = seq + 2 * POOL_PAD
    z = z_ref[...]
    zeros = jnp.zeros((POOL_PAD, D_BRANCH), F32)
    pad_scr[0:POOL_PAD, :] = zeros
    pad_scr[POOL_PAD:POOL_PAD + seq, :] = z
    pad_scr[POOL_PAD + seq:padded, :] = zeros
    t = lax.broadcasted_iota(jnp.int32, (seq, POOL_GROUP), 0)
    outs = []
    for gi, w in enumerate(POOL_WINDOWS):
        lanes = slice(gi * POOL_GROUP, (gi + 1) * POOL_GROUP)
        run = pad_scr[:, lanes]
        span = 1
        while span < w:
            run = run + pltpu.roll(run, padded - span, 0)
            span *= 2
        tot = pltpu.roll(run, w // 2, 0)[POOL_PAD:POOL_PAD + seq]
        cnt = jnp.minimum(t + w // 2, seq) - jnp.maximum(t - w // 2, 0)
        pooled = tot / cnt.astype(F32) - z[:, lanes]
        outs.append(_dot(pooled.astype(BF16), w_ref[gi]))
    o_ref[...] = (jnp.concatenate(outs, axis=1) * sc_ref[...]).astype(BF16)


def _pool(z, w_pool, pool_scale, row0, n_seq, seq, layer):
    s0 = row0 // seq
    return pl.pallas_call(
        _pool_kernel,
        grid=(n_seq,),
        in_specs=[
            pl.BlockSpec((seq, D_BRANCH), lambda b: (s0 + b, 0)),
            pl.BlockSpec((None, len(POOL_WINDOWS), POOL_GROUP, POOL_GROUP), lambda b: (layer, 0, 0, 0)),
            pl.BlockSpec((None, 1, D_BRANCH), lambda b: (layer, 0, 0)),
        ],
        out_specs=pl.BlockSpec((seq, D_BRANCH), lambda b: (b, 0)),
        out_shape=jax.ShapeDtypeStruct((n_seq * seq, D_BRANCH), BF16),
        scratch_shapes=[pltpu.VMEM((seq + 2 * POOL_PAD, D_BRANCH), F32)],
        compiler_params=_cparams(("parallel",)),
        name="pool",
    )(z, w_pool, pool_scale)


def _merge_kernel(x_ref, pat_ref, g_ref, *refs, tiles_p):
    branch_refs = refs[:2 * N_BRANCH]
    wg0_ref, wg1_ref, wbr_ref, wo_ref, o_ref, wg_res, wbr_res, wo_res, n_scr, m_scr = refs[2 * N_BRANCH:]
    s = pl.program_id(0)
    prompt = jnp.maximum(s - (N_BRANCH - 1), 0) < tiles_p
    x = x_ref[...]

    def normed():
        return _norm_mod(x, g_ref[...], pat_ref[3:4], pat_ref[4:5]).astype(BF16)

    def branch_in(br):
        return jnp.where(prompt, branch_refs[2 * br][...], branch_refs[2 * br + 1][...])

    def finish(merged):
        o_ref[...] = x + pat_ref[5:6] * _dot(merged.astype(BF16), wo_res[...])

    @pl.when(s < N_BRANCH)
    def _():
        @pl.when(s == 0)
        def _():
            n_scr[...] = normed()
            m_scr[...] = jnp.zeros_like(m_scr)
            wo_res[...] = wo_ref[...].astype(BF16)

        wg = jnp.concatenate([wg0_ref[...], wg1_ref[...]], axis=1).astype(BF16)
        wbr = wbr_ref[...].astype(BF16)
        wg_res[s] = wg
        wbr_res[s] = wbr
        gate = jax.nn.sigmoid(_dot(n_scr[...], wg))
        for br in range(N_BRANCH):
            @pl.when(s == br)
            def _(br=br):
                m_scr[...] += gate * _dot(branch_in(br), wbr)

        @pl.when(s == N_BRANCH - 1)
        def _():
            finish(m_scr[...])

    @pl.when(s >= N_BRANCH)
    def _():
        n = normed()
        merged = None
        for br in range(N_BRANCH):
            part = jax.nn.sigmoid(_dot(n, wg_res[br])) * _dot(branch_in(br), wbr_res[br])
            merged = part if merged is None else merged + part
        finish(merged)


def _merge(rows, x, pat, norm_g, branches, w_in, w_branch, w_out, layer):
    tiles_p = rows.tiles_p
    half = D_MODEL // 2
    gate0 = N_PROJ * D_BRANCH // half
    tile_of = lambda s: jnp.maximum(s - (N_BRANCH - 1), 0)
    br_of = lambda s: jnp.minimum(s, N_BRANCH - 1)
    p_spec = pl.BlockSpec((ROW_TILE, D_BRANCH), lambda s: (jnp.minimum(tile_of(s), tiles_p - 1), 0))
    s_spec = pl.BlockSpec((ROW_TILE, D_BRANCH), lambda s: (jnp.maximum(tile_of(s) - tiles_p, 0), 0))
    return pl.pallas_call(
        functools.partial(_merge_kernel, tiles_p=tiles_p),
        grid=(N_BRANCH - 1 + rows.n_tok // ROW_TILE,),
        in_specs=[
            pl.BlockSpec((ROW_TILE, D_MODEL), lambda s: (tile_of(s), 0)),
            pl.BlockSpec((None, None, N_MOD, D_MODEL), lambda s: (layer, rows.mod_row(tile_of(s)), 0, 0)),
            pl.BlockSpec((None, None, 1, D_MODEL), lambda s: (layer, 1, 0, 0)),
            p_spec, s_spec, p_spec, s_spec, p_spec, s_spec,
            pl.BlockSpec((None, D_MODEL, half), lambda s: (layer, 0, gate0 + 2 * br_of(s))),
            pl.BlockSpec((None, D_MODEL, half), lambda s: (layer, 0, gate0 + 2 * br_of(s) + 1)),
            pl.BlockSpec((None, None, D_BRANCH, D_MODEL), lambda s: (layer, br_of(s), 0, 0)),
            _resident((None, D_MODEL, D_MODEL), lambda s: (layer, 0, 0)),
        ],
        out_specs=pl.BlockSpec((ROW_TILE, D_MODEL), lambda s: (tile_of(s), 0)),
        out_shape=jax.ShapeDtypeStruct((rows.n_tok, D_MODEL), F32),
        scratch_shapes=[
            pltpu.VMEM((N_BRANCH, D_MODEL, D_MODEL), BF16),
            pltpu.VMEM((N_BRANCH, D_BRANCH, D_MODEL), BF16),
            pltpu.VMEM((D_MODEL, D_MODEL), BF16),
            pltpu.VMEM((ROW_TILE, D_MODEL), BF16),
            pltpu.VMEM((ROW_TILE, D_MODEL), F32),
        ],
        compiler_params=_cparams(("arbitrary",)),
        name="merge",
    )(x, pat, norm_g, *[y for pair in branches for y in pair], w_in, w_in, w_branch, w_out)


def _rope_tables(seq):
    n_rows = seq // GRID_W
    row = jnp.repeat(jnp.arange(n_rows, dtype=F32), GRID_W)
    col = jnp.tile(jnp.arange(GRID_W, dtype=F32), n_rows)
    n_freq = DH // 4
    inv = ROPE_BASE ** (-jnp.arange(n_freq, dtype=F32) / n_freq)
    ar = row[:, None] * inv
    ac = col[:, None] * inv
    cos = jnp.concatenate([jnp.cos(ar)] * 2 + [jnp.cos(ac)] * 2, axis=1)
    sin = jnp.concatenate([-jnp.sin(ar), jnp.sin(ar), -jnp.sin(ac), jnp.sin(ac)], axis=1)
    return jnp.tile(cos, (1, LANES // DH)), jnp.tile(sin, (1, LANES // DH))


def kernel(x_prompt, x_sample, cache_k, cache_v, state_ssm, c, c_ctx, norm_g, w_mod, b_mod, w_ffn_in, w_ffn_out, w_in, ssm_lam_re, ssm_lam_im, ssm_log_dt, ssm_b_re, ssm_b_im, ssm_c_re, ssm_c_im, ssm_d, w_glu, lam_q1, lam_k1, lam_q2, lam_k2, attn_norm_g, w_pool, pool_scale, w_branch, w_out, final_norm_g):
    n_p, seq_p, _ = x_prompt.shape
    n_s, seq_s, _ = x_sample.shape
    n_past = cache_k.shape[2]
    rows = _Rows(n_p, seq_p, n_s, seq_s)
    tok_p = rows.tok_p

    cvec = jnp.concatenate([c_ctx[None, :], c, jnp.zeros((16 - 1 - n_s, D_MODEL), F32)], axis=0)
    pat = _adaln(cvec, w_mod, b_mod).reshape(DEPTH, 16, N_MOD, D_MODEL)

    w_pl = w_pool.astype(BF16)

    x = (x_prompt.reshape(tok_p, D_MODEL), x_sample.reshape(rows.tok_s, D_MODEL))
    rope_tabs = _rope_tables(seq_s)
    cache = (cache_k.reshape(n_s, DEPTH, n_past, D_BRANCH), cache_v.reshape(n_s, DEPTH, n_past, D_BRANCH))
    h0_p = jnp.zeros((1, 2, 2, n_p, N_STATE), F32)
    h0_s = state_ssm.reshape(n_s, DEPTH, 2, 2, N_STATE).transpose(1, 2, 3, 0, 4)

    ssm = jax.vmap(_s5_params)(ssm_lam_re, ssm_lam_im, ssm_log_dt, ssm_b_re, ssm_b_im, ssm_c_re, ssm_c_im)
    ssm.update(d_skip=ssm_d[:, None, :], w_glu=w_glu.astype(BF16))

    lam_inits = [0.8 - 0.6 * math.exp(-0.3 * l) for l in range(DEPTH)]
    lam = (jnp.exp(jnp.sum(lam_q1 * lam_k1, axis=-1)) - jnp.exp(jnp.sum(lam_q2 * lam_k2, axis=-1))
           + jnp.asarray(lam_inits, F32))
    ng = norm_g.reshape(DEPTH, 3, 1, D_MODEL)
    attn_g = attn_norm_g.reshape(DEPTH, 1, DV)
    scale_c = pool_scale.reshape(DEPTH, 1, D_BRANCH)

    new_cache, new_s = None, []
    for l in range(DEPTH):
        x = _ffn(rows, x, pat, ng, w_ffn_in, w_ffn_out, l, 0, 0)
        u_p, u_s, q, k_s, v_s, k_new, v_new, z = _inproj(rows, x, pat, ng, w_in, l, rope_tabs, new_cache)
        new_cache = (k_new, v_new)

        ya_p, h_fin = _s5(u_p, ssm, h0_p, 0, n_p, l)
        ya_s, _ = _s5(u_s, ssm, h0_s, l, n_s, l)
        yb = (_attention(q, k_new, v_new, lam, attn_g, 0, n_p, seq_p, lam_inits[l], l),
              _attention(q, k_s, v_s, lam, attn_g, tok_p, n_s, seq_s, lam_inits[l], l, cache))
        yc = (_pool(z, w_pl, scale_c, 0, n_p, seq_p, l), _pool(z, w_pl, scale_c, tok_p, n_s, seq_s, l))

        x = _merge(rows, x, pat, ng, ((ya_p, ya_s), yb, yc), w_in, w_branch, w_out, l)
        last = l == DEPTH - 1
        x = _ffn(rows, x, pat, ng, w_ffn_in, w_ffn_out, l, 1, 6, final_norm_g[None, :] if last else None)
        new_s.append(h_fin)

    y_p, y_s = x
    new_state = jnp.stack(new_s, axis=0).transpose(3, 0, 1, 2, 4)
    return (y_p.reshape(n_p, seq_p, D_MODEL), y_s.reshape(n_s, seq_s, D_MODEL),
            new_cache[0].reshape(n_p, DEPTH, seq_p, N_HEADS, 2, DH), new_cache[1].reshape(n_p, DEPTH, seq_p, N_HEADS, DV),
            new_state.reshape(n_p, DEPTH, 2, 2, S5_GROUPS, S5_STATE))
```

```python
import functools
import math

import jax
import jax.numpy as jnp
from jax import lax
from jax.experimental import pallas as pl
from jax.experimental.pallas import tpu as pltpu

F32 = jnp.float32
BF16 = jnp.bfloat16

D_MODEL = 1024
DEPTH = 4
GRID_W = 64
D_BRANCH = 512
S5_GROUP = 16
S5_GROUPS = 32
S5_STATE = 64
N_STATE = S5_GROUPS * S5_STATE
DH = 64
N_HEADS = 4
DV = 128
POOL_WINDOWS = (2, 4, 8, 16)
POOL_GROUP = 128
POOL_PAD = 16
D_FF = 2816
N_MOD = 9
N_BRANCH = 3
N_PROJ = 5
ROPE_BASE = 10000.0
EPS = 1e-6

LANES = 128
N_SLAB = D_BRANCH // LANES
SLAB_STATE = N_STATE // N_SLAB
ROW_TILE = 512
FF_CHUNK = 256
S5_CHUNK_ROWS = 1024
Q_TILE = 512
VMEM_LIMIT = 56 * 1024 * 1024


def _cparams(sem):
    return pltpu.CompilerParams(dimension_semantics=sem, vmem_limit_bytes=VMEM_LIMIT)


def _resident(block_shape, index_map):
    return pl.BlockSpec(block_shape, index_map, pipeline_mode=pl.Buffered(1))


def _norm_mod(x, g, shift, scale):
    ms = jnp.mean(x * x, axis=-1, keepdims=True)
    y = (x * lax.rsqrt(ms + EPS)) * g
    return y * (1.0 + scale) + shift


def _dot(a, b):
    return jnp.dot(a, b, preferred_element_type=F32)


class _Rows:
    def __init__(self, n_p, seq_p, n_s, seq_s):
        self.n_p, self.seq_p, self.n_s, self.seq_s = n_p, seq_p, n_s, seq_s
        self.tok_p = n_p * seq_p
        self.tok_s = n_s * seq_s
        self.n_tok = self.tok_p + self.tok_s
        assert self.tok_p % ROW_TILE == 0 and seq_s % ROW_TILE == 0
        self.tiles_p = self.tok_p // ROW_TILE

    def mod_row(self, i):
        return jnp.where(i < self.tiles_p, 0, 1 + (i - self.tiles_p) // (self.seq_s // ROW_TILE))


def _adaln_kernel(c_ref, w_ref, b_ref, o_ref):
    c = c_ref[...]
    s = (c * jax.nn.sigmoid(c)).astype(BF16)
    o_ref[...] = _dot(s, w_ref[...].astype(BF16)) + b_ref[...]


def _adaln(cvec, w_mod, b_mod):
    tn = 1024
    n_rows = cvec.shape[0]
    return pl.pallas_call(
        _adaln_kernel,
        grid=(DEPTH, N_MOD * D_MODEL // tn),
        in_specs=[
            pl.BlockSpec((n_rows, D_MODEL), lambda l, j: (0, 0)),
            pl.BlockSpec((None, D_MODEL, tn), lambda l, j: (l, 0, j)),
            pl.BlockSpec((None, 1, tn), lambda l, j: (l, 0, j)),
        ],
        out_specs=pl.BlockSpec((None, n_rows, tn), lambda l, j: (l, 0, j)),
        out_shape=jax.ShapeDtypeStruct((DEPTH, n_rows, N_MOD * D_MODEL), F32),
        compiler_params=_cparams(("parallel", "parallel")),
        name="adaln",
    )(cvec, w_mod, b_mod.reshape(DEPTH, 1, N_MOD * D_MODEL))


def _ffn_kernel(*refs, k0, tiles_p, n_chunks, split_in, final):
    refs = list(refs)
    s = pl.program_id(0)
    tile = jnp.maximum(s - (n_chunks - 1), 0)
    prompt = tile < tiles_p
    if split_in:
        xp_ref, xs_ref = refs[:2]
        refs = refs[2:]
        x = jnp.where(prompt, xp_ref[...], xs_ref[...])
    else:
        x = refs.pop(0)[...]
    pat_ref, g_ref, wa_ref, wb_ref, wo_ref = refs[:5]
    refs = refs[5:]
    wa_res, wb_res, wo_res, n_scr, acc_scr, act_scr = refs[-6:]
    refs = refs[:-6]

    def normed():
        return _norm_mod(x, g_ref[...], pat_ref[k0:k0 + 1], pat_ref[k0 + 1:k0 + 2]).astype(BF16)

    def finish(y):
        out = x + (0.5 * pat_ref[k0 + 2:k0 + 3]) * y
        if not final:
            refs[0][...] = out
            return
        fg_ref, op_ref, os_ref = refs
        ms = jnp.mean(out * out, axis=-1, keepdims=True)
        out = (out * lax.rsqrt(ms + EPS)) * fg_ref[...]

        @pl.when(prompt)
        def _():
            op_ref[...] = out

        @pl.when(jnp.logical_not(prompt))
        def _():
            os_ref[...] = out

    @pl.when(s < n_chunks)
    def _():
        @pl.when(s == 0)
        def _():
            n_scr[...] = normed()
            acc_scr[...] = jnp.zeros_like(acc_scr)

        wa = wa_ref[...].astype(BF16)
        wb = wb_ref[...].astype(BF16)
        wo = wo_ref[...].astype(BF16)
        wa_res[s] = wa
        wb_res[s] = wb
        wo_res[pl.ds(pl.multiple_of(s * FF_CHUNK, FF_CHUNK), FF_CHUNK), :] = wo
        n = n_scr[...]
        a = _dot(n, wa)
        b = _dot(n, wb)
        acc_scr[...] += _dot((a * jax.nn.sigmoid(a) * b).astype(BF16), wo)

        @pl.when(s == n_chunks - 1)
        def _():
            finish(acc_scr[...])

    @pl.when(s >= n_chunks)
    def _():
        n = normed()
        for j in range(n_chunks):
            a = _dot(n, wa_res[j])
            b = _dot(n, wb_res[j])
            act_scr[:, j * FF_CHUNK:(j + 1) * FF_CHUNK] = (a * jax.nn.sigmoid(a) * b).astype(BF16)
        finish(_dot(act_scr[...], wo_res[...]))


def _ffn(rows, x, pat, norm_g, w_ffn_in, w_ffn_out, layer, sub, k0, final_g=None):
    split_in = isinstance(x, tuple)
    final = final_g is not None
    n_chunks = D_FF // FF_CHUNK
    tiles_p = rows.tiles_p
    tile_of = lambda s: jnp.maximum(s - (n_chunks - 1), 0)
    chunk_of = lambda s: jnp.minimum(s, n_chunks - 1)
    tile = (ROW_TILE, D_MODEL)
    p_spec = pl.BlockSpec(tile, lambda s: (jnp.minimum(tile_of(s), tiles_p - 1), 0))
    s_spec = pl.BlockSpec(tile, lambda s: (jnp.maximum(tile_of(s) - tiles_p, 0), 0))
    all_spec = pl.BlockSpec(tile, lambda s: (tile_of(s), 0))
    in_specs = ([p_spec, s_spec] if split_in else [all_spec]) + [
        pl.BlockSpec((None, None, N_MOD, D_MODEL), lambda s: (layer, rows.mod_row(tile_of(s)), 0, 0)),
        pl.BlockSpec((None, None, 1, D_MODEL), lambda s: (layer, 2 * sub, 0, 0)),
        pl.BlockSpec((None, None, D_MODEL, FF_CHUNK), lambda s: (layer, sub, 0, chunk_of(s))),
        pl.BlockSpec((None, None, D_MODEL, FF_CHUNK), lambda s: (layer, sub, 0, n_chunks + chunk_of(s))),
        pl.BlockSpec((None, None, FF_CHUNK, D_MODEL), lambda s: (layer, sub, chunk_of(s), 0)),
    ]
    args = (list(x) if split_in else [x]) + [pat, norm_g, w_ffn_in, w_ffn_in, w_ffn_out]
    if final:
        in_specs.append(pl.BlockSpec((1, D_MODEL), lambda s: (0, 0)))
        args.append(final_g)
        out_specs = [p_spec, s_spec]
        out_shape = [jax.ShapeDtypeStruct((rows.tok_p, D_MODEL), F32), jax.ShapeDtypeStruct((rows.tok_s, D_MODEL), F32)]
    else:
        out_specs = all_spec
        out_shape = jax.ShapeDtypeStruct((rows.n_tok, D_MODEL), F32)
    return pl.pallas_call(
        functools.partial(_ffn_kernel, k0=k0, tiles_p=tiles_p, n_chunks=n_chunks, split_in=split_in, final=final),
        grid=(n_chunks - 1 + rows.n_tok // ROW_TILE,),
        in_specs=in_specs,
        out_specs=out_specs,
        out_shape=out_shape,
        scratch_shapes=[
            pltpu.VMEM((n_chunks, D_MODEL, FF_CHUNK), BF16),
            pltpu.VMEM((n_chunks, D_MODEL, FF_CHUNK), BF16),
            pltpu.VMEM((D_FF, D_MODEL), BF16),
            pltpu.VMEM((ROW_TILE, D_MODEL), BF16),
            pltpu.VMEM((ROW_TILE, D_MODEL), F32),
            pltpu.VMEM((ROW_TILE, D_FF), BF16),
        ],
        compiler_params=_cparams(("arbitrary",)),
        name="ffn",
    )(*args)


def _rope(x, cos, sin):
    lane = lax.broadcasted_iota(jnp.int32, (x.shape[0], LANES), 1)
    first_half = (lane % 32) < 16
    out = []
    for c in range(x.shape[1] // LANES):
        xc = x[:, c * LANES:(c + 1) * LANES]
        partner = jnp.where(first_half, pltpu.roll(xc, LANES - 16, 1), pltpu.roll(xc, 16, 1))
        out.append(xc * cos + partner * sin)
    return jnp.concatenate(out, axis=1)


def _inproj_kernel(*refs, tiles_p, n_alias):
    x_ref, pat_ref, g_ref, w_ref, cos_ref, sin_ref = refs[:6]
    up_ref, us_ref, q_ref, ks_ref, vs_ref, kc_ref, vc_ref, z_ref, w_res, n_scr = refs[6 + n_alias:]
    s = pl.program_id(0)
    latent = jnp.maximum(s - (N_PROJ - 1), 0) >= tiles_p
    scale = DH ** -0.5 * math.log2(math.e)

    def normed():
        return _norm_mod(x_ref[...], g_ref[...], pat_ref[3:4], pat_ref[4:5]).astype(BF16)

    def store_prompt(c, y):
        if c == 0:
            up_ref[...] = y
        elif c == 1:
            q_ref[...] = (y * scale).astype(BF16)
        elif c == 2:
            kc_ref[...] = y.reshape(kc_ref.shape)
        elif c == 3:
            vc_ref[...] = y.reshape(vc_ref.shape)
        else:
            z_ref[...] = y

    @pl.when(s < N_PROJ)
    def _():
        @pl.when(s == 0)
        def _():
            n_scr[...] = normed()

        w = w_ref[...].astype(BF16)
        w_res[s] = w
        y = _dot(n_scr[...], w)
        for c in range(N_PROJ):
            @pl.when(s == c)
            def _(c=c):
                store_prompt(c, y)

    @pl.when(s >= N_PROJ)
    def _():
        n = normed()
        col = lambda c: _dot(n, w_res[c])

        @pl.when(latent)
        def _():
            q_ref[...] = (_rope(col(1), cos_ref[...], sin_ref[...]) * scale).astype(BF16)
            ks_ref[...] = _rope(col(2), cos_ref[...], sin_ref[...]).astype(BF16)
            us_ref[...] = col(0)
            vs_ref[...] = col(3).astype(BF16)
            z_ref[...] = col(4)

        @pl.when(jnp.logical_not(latent))
        def _():
            for c in range(N_PROJ):
                store_prompt(c, col(c))


def _inproj(rows, x, pat, norm_g, w_in, layer, rope_tabs, cache_out):
    tiles_p = rows.tiles_p
    seqs_per_tile = ROW_TILE // rows.seq_p
    tabs_per_seq = rows.seq_s // ROW_TILE
    tile_of = lambda s: jnp.maximum(s - (N_PROJ - 1), 0)
    tab_spec = pl.BlockSpec((ROW_TILE, LANES), lambda s: (jnp.maximum(tile_of(s) - tiles_p, 0) % tabs_per_seq, 0))
    tile = (ROW_TILE, D_BRANCH)
    p_spec = pl.BlockSpec(tile, lambda s: (jnp.minimum(tile_of(s), tiles_p - 1), 0))
    s_spec = pl.BlockSpec(tile, lambda s: (jnp.maximum(tile_of(s) - tiles_p, 0), 0))
    all_spec = pl.BlockSpec(tile, lambda s: (tile_of(s), 0))
    cache_spec = pl.BlockSpec((seqs_per_tile, None, rows.seq_p, D_BRANCH),
                              lambda s: (jnp.minimum(tile_of(s), tiles_p - 1), layer, 0, 0))
    cache_shape = jax.ShapeDtypeStruct((rows.n_p, DEPTH, rows.seq_p, D_BRANCH), F32)
    in_specs = [
        pl.BlockSpec((ROW_TILE, D_MODEL), lambda s: (tile_of(s), 0)),
        pl.BlockSpec((None, None, N_MOD, D_MODEL), lambda s: (layer, rows.mod_row(tile_of(s)), 0, 0)),
        pl.BlockSpec((None, None, 1, D_MODEL), lambda s: (layer, 1, 0, 0)),
        pl.BlockSpec((None, D_MODEL, D_BRANCH), lambda s: (layer, 0, jnp.minimum(s, N_PROJ - 1))),
        tab_spec, tab_spec,
    ]
    args = [x, pat, norm_g, w_in, *rope_tabs]
    aliases = {}
    if cache_out is not None:
        in_specs += [pl.BlockSpec(memory_space=pl.ANY)] * 2
        aliases = {len(args): 5, len(args) + 1: 6}
        args += list(cache_out)
    return pl.pallas_call(
        functools.partial(_inproj_kernel, tiles_p=tiles_p, n_alias=len(aliases)),
        grid=(N_PROJ - 1 + rows.n_tok // ROW_TILE,),
        in_specs=in_specs,
        out_specs=[p_spec, s_spec, all_spec, s_spec, s_spec, cache_spec, cache_spec, all_spec],
        out_shape=[
            jax.ShapeDtypeStruct((rows.tok_p, D_BRANCH), F32),
            jax.ShapeDtypeStruct((rows.tok_s, D_BRANCH), F32),
            jax.ShapeDtypeStruct((rows.n_tok, D_BRANCH), BF16),
            jax.ShapeDtypeStruct((rows.tok_s, D_BRANCH), BF16),
            jax.ShapeDtypeStruct((rows.tok_s, D_BRANCH), BF16),
            cache_shape, cache_shape,
            jax.ShapeDtypeStruct((rows.n_tok, D_BRANCH), F32),
        ],
        scratch_shapes=[pltpu.VMEM((N_PROJ, D_MODEL, D_BRANCH), BF16), pltpu.VMEM((ROW_TILE, D_MODEL), BF16)],
        input_output_aliases=aliases,
        compiler_params=_cparams(("arbitrary",)),
        name="inproj",
    )(*args)


def _s5_kernel(*refs, n_seq, n_steps, backward):
    if backward:
        u_ref, part_ref, w1_ref, w2_ref, w3_ref, ar_ref, ai_ref, h0_ref, wglu_ref, ya_ref, hfin_ref, xs_scr, h_scr = refs
    else:
        u_ref, dsk_ref, w1_ref, w2_ref, w3_ref, ar_ref, ai_ref, h0_ref, part_ref, hfin_ref, xs_scr, h_scr = refs
    rows = n_seq * n_steps
    pairs = n_steps // 2
    half = pairs * n_seq

    @pl.when(pl.program_id(0) == 0)
    def _():
        h_scr[...] = h0_ref[...]

    u = jnp.swapaxes(u_ref[...], 0, 1).reshape(rows, D_BRANCH)
    u4 = u.reshape(pairs, 2, n_seq, D_BRANCH)
    u_even = u4[:, 0].reshape(half, D_BRANCH).astype(BF16)
    u_odd = u4[:, 1].reshape(half, D_BRANCH).astype(BF16)
    u_first = u_odd if backward else u_even
    base = 0 if backward else n_seq
    incoming = half if backward else 0
    order = range(pairs - 1, -1, -1) if backward else range(pairs)

    def project(j):
        cols = slice(j * LANES, (j + 1) * LANES)
        lhs = jnp.concatenate([u_even[:, cols], u_odd[:, cols]], axis=1)
        xs_scr[j, base:base + half, :] = _dot(lhs, w1_ref[j])

    def scan(j):
        lanes = slice(j * SLAB_STATE, (j + 1) * SLAB_STATE)
        ar = jnp.broadcast_to(ar_ref[:, lanes], (n_seq, SLAB_STATE))
        ai = jnp.broadcast_to(ai_ref[:, lanes], (n_seq, SLAB_STATE))
        hr = h_scr[0, :, lanes]
        hi = h_scr[1, :, lanes]
        xs_scr[j, incoming:incoming + n_seq, 0:SLAB_STATE] = hr
        xs_scr[j, incoming:incoming + n_seq, SLAB_STATE:2 * SLAB_STATE] = hi
        for q in order:
            r = slice(base + q * n_seq, base + (q + 1) * n_seq)
            xr = xs_scr[j, r, 0:SLAB_STATE]
            xi = xs_scr[j, r, SLAB_STATE:2 * SLAB_STATE]
            hr, hi = ar * hr - ai * hi + xr, ar * hi + ai * hr + xi
            xs_scr[j, r, 0:SLAB_STATE] = hr
            xs_scr[j, r, SLAB_STATE:2 * SLAB_STATE] = hi
        h_scr[0, :, lanes] = hr
        h_scr[1, :, lanes] = hi

    def read_out(j):
        z = _dot(xs_scr[j].astype(BF16), w2_ref[j])
        direct = _dot(u_first[:, j * LANES:(j + 1) * LANES], w3_ref[j])
        lo = z[0:half]
        hi = z[n_seq:n_seq + half]
        if backward:
            even, odd = lo[:, 0:LANES], hi[:, LANES:2 * LANES] + direct
        else:
            even, odd = lo[:, LANES:2 * LANES] + direct, hi[:, 0:LANES]
        both = jnp.stack([even.reshape(pairs, n_seq, LANES), odd.reshape(pairs, n_seq, LANES)], axis=1)
        return both.reshape(rows, LANES)

    ys = []
    project(0)
    for j in range(N_SLAB):
        if j + 1 < N_SLAB:
            project(j + 1)
        scan(j)
        ys.append(read_out(j))
    y = jnp.concatenate(ys, axis=1)

    if backward:
        g = jax.nn.gelu(part_ref[...] + y)
        ya = g * jax.nn.sigmoid(_dot(g.astype(BF16), wglu_ref[...]))
        ya_ref[...] = jnp.swapaxes(ya.reshape(n_steps, n_seq, D_BRANCH), 0, 1).astype(BF16)
    else:
        part_ref[...] = y + dsk_ref[...] * u

    @pl.when(pl.program_id(0) == pl.num_programs(0) - 1)
    def _():
        hfin_ref[...] = h_scr[...]


def _s5(u, ssm, h0, h0_row, n_seq, layer):
    n_tok = u.shape[0]
    seq = n_tok // n_seq
    n_steps = S5_CHUNK_ROWS // n_seq
    n_chunks = seq // n_steps
    u3 = u.reshape(n_seq, seq, D_BRANCH)

    def specs(backward):
        d = int(backward)
        chunk = (lambda s: n_chunks - 1 - s) if backward else (lambda s: s)
        seq_block = pl.BlockSpec((n_seq, n_steps, D_BRANCH), lambda s: (0, chunk(s), 0))
        part_block = pl.BlockSpec((S5_CHUNK_ROWS, D_BRANCH), lambda s: (chunk(s), 0))
        params = [
            pl.BlockSpec((None, None, N_SLAB, 2 * LANES, 2 * SLAB_STATE), lambda s: (layer, d, 0, 0, 0)),
            pl.BlockSpec((None, None, N_SLAB, 2 * SLAB_STATE, 2 * LANES), lambda s: (layer, d, 0, 0, 0)),
            pl.BlockSpec((None, None, N_SLAB, LANES, LANES), lambda s: (layer, d, 0, 0, 0)),
            pl.BlockSpec((None, None, 1, N_STATE), lambda s: (layer, d, 0, 0)),
            pl.BlockSpec((None, None, 1, N_STATE), lambda s: (layer, d, 0, 0)),
            pl.BlockSpec((None, None, 2, n_seq, N_STATE), lambda s: (h0_row, d, 0, 0, 0)),
        ]
        return seq_block, part_block, params

    param_args = (ssm["w1"], ssm["w2"], ssm["w3"], ssm["a2_re"], ssm["a2_im"], h0)
    fin_spec = pl.BlockSpec((2, n_seq, N_STATE), lambda s: (0, 0, 0))
    fin_shape = jax.ShapeDtypeStruct((2, n_seq, N_STATE), F32)
    state_rows = S5_CHUNK_ROWS // 2 + n_seq
    scratch = [pltpu.VMEM((N_SLAB, state_rows, 2 * SLAB_STATE), F32), pltpu.VMEM((2, n_seq, N_STATE), F32)]

    seq_block, part_block, params = specs(False)
    part, fin_f = pl.pallas_call(
        functools.partial(_s5_kernel, n_seq=n_seq, n_steps=n_steps, backward=False),
        grid=(n_chunks,),
        in_specs=[seq_block, pl.BlockSpec((None, 1, D_BRANCH), lambda s: (layer, 0, 0))] + params,
        out_specs=[part_block, fin_spec],
        out_shape=[jax.ShapeDtypeStruct((n_tok, D_BRANCH), F32), fin_shape],
        scratch_shapes=scratch,
        compiler_params=_cparams(("arbitrary",)),
        name="s5_fwd",
    )(u3, ssm["d_skip"], *param_args)

    seq_block, part_block, params = specs(True)
    ya, fin_b = pl.pallas_call(
        functools.partial(_s5_kernel, n_seq=n_seq, n_steps=n_steps, backward=True),
        grid=(n_chunks,),
        in_specs=[seq_block, part_block] + params + [pl.BlockSpec((None, D_BRANCH, D_BRANCH), lambda s: (layer, 0, 0))],
        out_specs=[seq_block, fin_spec],
        out_shape=[jax.ShapeDtypeStruct((n_seq, seq, D_BRANCH), BF16), fin_shape],
        scratch_shapes=scratch,
        compiler_params=_cparams(("arbitrary",)),
        name="s5_bwd",
    )(u3, part, *param_args, ssm["w_glu"])
    return ya.reshape(n_tok, D_BRANCH), jnp.stack([fin_f, fin_b], axis=0)


def _s5_params(lam_re, lam_im, log_dt, b_re, b_im, c_re, c_im):
    dt = jnp.exp(log_dt)[..., None]
    mag = jnp.exp(lam_re * dt)
    abr = mag * jnp.cos(lam_im * dt)
    abi = mag * jnp.sin(lam_im * dt)
    den = lam_re * lam_re + lam_im * lam_im
    nr = abr - 1.0
    kr = (nr * lam_re + abi * lam_im) / den
    ki = (abi * lam_re - nr * lam_im) / den
    bbr = kr[..., None] * b_re - ki[..., None] * b_im
    bbi = kr[..., None] * b_im + ki[..., None] * b_re
    per_slab = S5_GROUPS // N_SLAB

    def block_diag(w, n_in, n_out):
        t = w.reshape(2, N_SLAB, per_slab, n_out, n_in).transpose(0, 1, 2, 4, 3)
        t = jnp.tile(t.reshape(2, N_SLAB, per_slab * n_in, n_out), (1, 1, 1, per_slab))
        row_group = lax.broadcasted_iota(jnp.int32, t.shape[-2:], 0) // n_in
        col_group = lax.broadcasted_iota(jnp.int32, t.shape[-2:], 1) // n_out
        return jnp.where(row_group == col_group, t, 0.0)

    def pack_in(w):
        return block_diag(w, S5_GROUP, S5_STATE)

    def pack_out(w):
        return block_diag(w, S5_STATE, S5_GROUP)

    abbr = abr[..., None] * bbr - abi[..., None] * bbi
    abbi = abr[..., None] * bbi + abi[..., None] * bbr
    car = c_re * abr[:, :, None, :] - c_im * abi[:, :, None, :]
    cai = c_re * abi[:, :, None, :] + c_im * abr[:, :, None, :]
    b_pack = jnp.concatenate([pack_in(bbr), pack_in(bbi)], axis=-1)
    ab_pack = jnp.concatenate([pack_in(abbr), pack_in(abbi)], axis=-1)
    w1 = jnp.stack([jnp.concatenate([ab_pack[0], b_pack[0]], axis=-2),
                    jnp.concatenate([b_pack[1], ab_pack[1]], axis=-2)]).astype(BF16)
    c_pack = jnp.concatenate([pack_out(c_re), -pack_out(c_im)], axis=-2)
    ca_pack = jnp.concatenate([pack_out(car), -pack_out(cai)], axis=-2)
    w2 = jnp.concatenate([c_pack, ca_pack], axis=-1).astype(BF16)
    direct = jnp.einsum("dgop,dgpi->dgoi", c_re, bbr) - jnp.einsum("dgop,dgpi->dgoi", c_im, bbi)
    w3 = block_diag(direct, S5_GROUP, S5_GROUP).astype(BF16)
    a2r = abr * abr - abi * abi
    a2i = 2.0 * abr * abi
    return dict(w1=w1, w2=w2, w3=w3, a2_re=a2r.reshape(2, 1, N_STATE), a2_im=a2i.reshape(2, 1, N_STATE))


def _attn_kernel(*refs, n_ctx, lam_init, layer):
    if n_ctx:
        lam_ref, q_ref, k_ref, v_ref, ck_ref, cv_ref, g_ref, o_ref, k_scr, v_scr = refs
    else:
        lam_ref, q_ref, k_ref, v_ref, g_ref, o_ref, k_scr, v_scr = refs
    n_own = k_ref.shape[0]

    @pl.when(pl.program_id(1) == 0)
    def _():
        k_scr[0:n_own, :] = k_ref[...].astype(BF16)
        v_scr[0:n_own, :] = v_ref[...].astype(BF16)
        if n_ctx:
            k_scr[n_own:n_own + n_ctx, :] = ck_ref[...].astype(BF16)
            v_scr[n_own:n_own + n_ctx, :] = cv_ref[...].astype(BF16)

    lam = lam_ref[layer]
    q = q_ref[...]
    lane = lax.broadcasted_iota(jnp.int32, (q.shape[0], LANES), 1)

    def scores(h):
        lanes = slice(h * LANES, (h + 1) * LANES)
        qh = q[:, lanes]
        kh = k_scr[:, lanes]
        out = []
        for m in range(2):
            qm = jnp.where((lane < DH) if m == 0 else (lane >= DH), qh, jnp.zeros_like(qh))
            out.append(lax.dot_general(qm, kh, (((1,), (1,)), ((), ())), preferred_element_type=F32))
        return out

    sc = scores(0)
    for h in range(N_HEADS):
        lanes = slice(h * LANES, (h + 1) * LANES)
        nxt = scores(h + 1) if h + 1 < N_HEADS else None
        e1, e2 = [jnp.exp2(s_m - jnp.max(s_m, axis=-1, keepdims=True)) for s_m in sc]
        l1 = jnp.sum(e1, axis=-1, keepdims=True)
        l2 = jnp.sum(e2, axis=-1, keepdims=True)
        a = (e1 - e2 * (lam * l1 / l2)).astype(BF16)
        o = _dot(a, v_scr[:, lanes]) * (1.0 / l1)
        ms = jnp.mean(o * o, axis=-1, keepdims=True)
        o_ref[:, lanes] = (((o * lax.rsqrt(ms + EPS)) * g_ref[...]) * (1.0 - lam_init)).astype(BF16)
        sc = nxt


def _attention(q, k, v, lam, attn_g, row0, n_seq, seq, lam_init, layer, cache=None):
    tq = min(Q_TILE, seq)
    q_tiles = seq // tq
    q0 = row0 // tq
    n_ctx = 0 if cache is None else cache[0].shape[2]
    if k.ndim == 4:
        own = pl.BlockSpec((None, None, seq, D_BRANCH), lambda b, i: (b, layer, 0, 0))
    else:
        own = pl.BlockSpec((seq, D_BRANCH), lambda b, i: (b, 0))
    in_specs = [
        pl.BlockSpec(memory_space=pltpu.SMEM),
        pl.BlockSpec((tq, D_BRANCH), lambda b, i: (q0 + b * q_tiles + i, 0)),
        own, own,
    ]
    args = [lam, q, k, v]
    if n_ctx:
        in_specs += [pl.BlockSpec((None, None, n_ctx, D_BRANCH), lambda b, i: (b, layer, 0, 0))] * 2
        args += list(cache)
    in_specs.append(pl.BlockSpec((None, 1, DV), lambda b, i: (layer, 0, 0)))
    args.append(attn_g)
    return pl.pallas_call(
        functools.partial(_attn_kernel, n_ctx=n_ctx, lam_init=lam_init, layer=layer),
        grid=(n_seq, q_tiles),
        in_specs=in_specs,
        out_specs=pl.BlockSpec((tq, D_BRANCH), lambda b, i: (b * q_tiles + i, 0)),
        out_shape=jax.ShapeDtypeStruct((n_seq * seq, D_BRANCH), BF16),
        scratch_shapes=[pltpu.VMEM((seq + n_ctx, D_BRANCH), BF16)] * 2,
        compiler_params=_cparams(("parallel", "arbitrary")),
        name="attn",
    )(*args)


def _pool_kernel(z_ref, w_ref, sc_ref, o_ref, pad_scr):
    seq = z_ref.shape[0]
    padded = seq + 2 * POOL_PAD
    z = z_ref[...]
    zeros = jnp.zeros((POOL_PAD, D_BRANCH), F32)
    pad_scr[0:POOL_PAD, :] = zeros
    pad_scr[POOL_PAD:POOL_PAD + seq, :] = z
    pad_scr[POOL_PAD + seq:padded, :] = zeros
    t = lax.broadcasted_iota(jnp.int32, (seq, POOL_GROUP), 0)
    outs = []
    for gi, w in enumerate(POOL_WINDOWS):
        lanes = slice(gi * POOL_GROUP, (gi + 1) * POOL_GROUP)
        run = pad_scr[:, lanes]
        span = 1
        while span < w:
            run = run + pltpu.roll(run, padded - span, 0)
            span *= 2
        tot = pltpu.roll(run, w // 2, 0)[POOL_PAD:POOL_PAD + seq]
        cnt = jnp.minimum(t + w // 2, seq) - jnp.maximum(t - w // 2, 0)
        pooled = tot / cnt.astype(F32) - z[:, lanes]
        outs.append(_dot(pooled.astype(BF16), w_ref[gi]))
    o_ref[...] = (jnp.concatenate(outs, axis=1) * sc_ref[...]).astype(BF16)


def _pool(z, w_pool, pool_scale, row0, n_seq, seq, layer):
    s0 = row0 // seq
    return pl.pallas_call(
        _pool_kernel,
        grid=(n_seq,),
        in_specs=[
            pl.BlockSpec((seq, D_BRANCH), lambda b: (s0 + b, 0)),
            pl.BlockSpec((None, len(POOL_WINDOWS), POOL_GROUP, POOL_GROUP), lambda b: (layer, 0, 0, 0)),
            pl.BlockSpec((None, 1, D_BRANCH), lambda b: (layer, 0, 0)),
        ],
        out_specs=pl.BlockSpec((seq, D_BRANCH), lambda b: (b, 0)),
        out_shape=jax.ShapeDtypeStruct((n_seq * seq, D_BRANCH), BF16),
        scratch_shapes=[pltpu.VMEM((seq + 2 * POOL_PAD, D_BRANCH), F32)],
        compiler_params=_cparams(("parallel",)),
        name="pool",
    )(z, w_pool, pool_scale)


def _merge_kernel(x_ref, pat_ref, g_ref, *refs, tiles_p):
    branch_refs = refs[:2 * N_BRANCH]
    wg0_ref, wg1_ref, wbr_ref, wo_ref, o_ref, wg_res, wbr_res, wo_res, n_scr, m_scr = refs[2 * N_BRANCH:]
    s = pl.program_id(0)
    prompt = jnp.maximum(s - (N_BRANCH - 1), 0) < tiles_p
    x = x_ref[...]

    def normed():
        return _norm_mod(x, g_ref[...], pat_ref[3:4], pat_ref[4:5]).astype(BF16)

    def branch_in(br):
        return jnp.where(prompt, branch_refs[2 * br][...], branch_refs[2 * br + 1][...])

    def finish(merged):
        o_ref[...] = x + pat_ref[5:6] * _dot(merged.astype(BF16), wo_res[...])

    @pl.when(s < N_BRANCH)
    def _():
        @pl.when(s == 0)
        def _():
            n_scr[...] = normed()
            m_scr[...] = jnp.zeros_like(m_scr)
            wo_res[...] = wo_ref[...].astype(BF16)

        wg = jnp.concatenate([wg0_ref[...], wg1_ref[...]], axis=1).astype(BF16)
        wbr = wbr_ref[...].astype(BF16)
        wg_res[s] = wg
        wbr_res[s] = wbr
        gate = jax.nn.sigmoid(_dot(n_scr[...], wg))
        for br in range(N_BRANCH):
            @pl.when(s == br)
            def _(br=br):
                m_scr[...] += gate * _dot(branch_in(br), wbr)

        @pl.when(s == N_BRANCH - 1)
        def _():
            finish(m_scr[...])

    @pl.when(s >= N_BRANCH)
    def _():
        n = normed()
        merged = None
        for br in range(N_BRANCH):
            part = jax.nn.sigmoid(_dot(n, wg_res[br])) * _dot(branch_in(br), wbr_res[br])
            merged = part if merged is None else merged + part
        finish(merged)


def _merge(rows, x, pat, norm_g, branches, w_in, w_branch, w_out, layer):
    tiles_p = rows.tiles_p
    half = D_MODEL // 2
    gate0 = N_PROJ * D_BRANCH // half
    tile_of = lambda s: jnp.maximum(s - (N_BRANCH - 1), 0)
    br_of = lambda s: jnp.minimum(s, N_BRANCH - 1)
    p_spec = pl.BlockSpec((ROW_TILE, D_BRANCH), lambda s: (jnp.minimum(tile_of(s), tiles_p - 1), 0))
    s_spec = pl.BlockSpec((ROW_TILE, D_BRANCH), lambda s: (jnp.maximum(tile_of(s) - tiles_p, 0), 0))
    return pl.pallas_call(
        functools.partial(_merge_kernel, tiles_p=tiles_p),
        grid=(N_BRANCH - 1 + rows.n_tok // ROW_TILE,),
        in_specs=[
            pl.BlockSpec((ROW_TILE, D_MODEL), lambda s: (tile_of(s), 0)),
            pl.BlockSpec((None, None, N_MOD, D_MODEL), lambda s: (layer, rows.mod_row(tile_of(s)), 0, 0)),
            pl.BlockSpec((None, None, 1, D_MODEL), lambda s: (layer, 1, 0, 0)),
            p_spec, s_spec, p_spec, s_spec, p_spec, s_spec,
            pl.BlockSpec((None, D_MODEL, half), lambda s: (layer, 0, gate0 + 2 * br_of(s))),
            pl.BlockSpec((None, D_MODEL, half), lambda s: (layer, 0, gate0 + 2 * br_of(s) + 1)),
            pl.BlockSpec((None, None, D_BRANCH, D_MODEL), lambda s: (layer, br_of(s), 0, 0)),
            _resident((None, D_MODEL, D_MODEL), lambda s: (layer, 0, 0)),
        ],
        out_specs=pl.BlockSpec((ROW_TILE, D_MODEL), lambda s: (tile_of(s), 0)),
        out_shape=jax.ShapeDtypeStruct((rows.n_tok, D_MODEL), F32),
        scratch_shapes=[
            pltpu.VMEM((N_BRANCH, D_MODEL, D_MODEL), BF16),
            pltpu.VMEM((N_BRANCH, D_BRANCH, D_MODEL), BF16),
            pltpu.VMEM((D_MODEL, D_MODEL), BF16),
            pltpu.VMEM((ROW_TILE, D_MODEL), BF16),
            pltpu.VMEM((ROW_TILE, D_MODEL), F32),
        ],
        compiler_params=_cparams(("arbitrary",)),
        name="merge",
    )(x, pat, norm_g, *[y for pair in branches for y in pair], w_in, w_in, w_branch, w_out)


def _rope_tables(seq):
    n_rows = seq // GRID_W
    row = jnp.repeat(jnp.arange(n_rows, dtype=F32), GRID_W)
    col = jnp.tile(jnp.arange(GRID_W, dtype=F32), n_rows)
    n_freq = DH // 4
    inv = ROPE_BASE ** (-jnp.arange(n_freq, dtype=F32) / n_freq)
    ar = row[:, None] * inv
    ac = col[:, None] * inv
    cos = jnp.concatenate([jnp.cos(ar)] * 2 + [jnp.cos(ac)] * 2, axis=1)
    sin = jnp.concatenate([-jnp.sin(ar), jnp.sin(ar), -jnp.sin(ac), jnp.sin(ac)], axis=1)
    return jnp.tile(cos, (1, LANES // DH)), jnp.tile(sin, (1, LANES // DH))


def kernel(x_prompt, x_sample, cache_k, cache_v, state_ssm, c, c_ctx, norm_g, w_mod, b_mod, w_ffn_in, w_ffn_out, w_in, ssm_lam_re, ssm_lam_im, ssm_log_dt, ssm_b_re, ssm_b_im, ssm_c_re, ssm_c_im, ssm_d, w_glu, lam_q1, lam_k1, lam_q2, lam_k2, attn_norm_g, w_pool, pool_scale, w_branch, w_out, final_norm_g):
    n_p, seq_p, _ = x_prompt.shape
    n_s, seq_s, _ = x_sample.shape
    n_past = cache_k.shape[2]
    rows = _Rows(n_p, seq_p, n_s, seq_s)
    tok_p = rows.tok_p

    cvec = jnp.concatenate([c_ctx[None, :], c, jnp.zeros((16 - 1 - n_s, D_MODEL), F32)], axis=0)
    pat = _adaln(cvec, w_mod, b_mod).reshape(DEPTH, 16, N_MOD, D_MODEL)

    w_pl = w_pool.astype(BF16)

    x = (x_prompt.reshape(tok_p, D_MODEL), x_sample.reshape(rows.tok_s, D_MODEL))
    rope_tabs = _rope_tables(seq_s)
    cache = (cache_k.reshape(n_s, DEPTH, n_past, D_BRANCH), cache_v.reshape(n_s, DEPTH, n_past, D_BRANCH))
    h0_p = jnp.zeros((1, 2, 2, n_p, N_STATE), F32)
    h0_s = state_ssm.reshape(n_s, DEPTH, 2, 2, N_STATE).transpose(1, 2, 3, 0, 4)

    ssm = jax.vmap(_s5_params)(ssm_lam_re, ssm_lam_im, ssm_log_dt, ssm_b_re, ssm_b_im, ssm_c_re, ssm_c_im)
    ssm.update(d_skip=ssm_d[:, None, :], w_glu=w_glu.astype(BF16))

    lam_inits = [0.8 - 0.6 * math.exp(-0.3 * l) for l in range(DEPTH)]
    lam = (jnp.exp(jnp.sum(lam_q1 * lam_k1, axis=-1)) - jnp.exp(jnp.sum(lam_q2 * lam_k2, axis=-1))
           + jnp.asarray(lam_inits, F32))
    ng = norm_g.reshape(DEPTH, 3, 1, D_MODEL)
    attn_g = attn_norm_g.reshape(DEPTH, 1, DV)
    scale_c = pool_scale.reshape(DEPTH, 1, D_BRANCH)

    new_cache, new_s = None, []
    for l in range(DEPTH):
        x = _ffn(rows, x, pat, ng, w_ffn_in, w_ffn_out, l, 0, 0)
        u_p, u_s, q, k_s, v_s, k_new, v_new, z = _inproj(rows, x, pat, ng, w_in, l, rope_tabs, new_cache)
        new_cache = (k_new, v_new)

        ya_p, h_fin = _s5(u_p, ssm, h0_p, 0, n_p, l)
        ya_s, _ = _s5(u_s, ssm, h0_s, l, n_s, l)
        yb = (_attention(q, k_new, v_new, lam, attn_g, 0, n_p, seq_p, lam_inits[l], l),
              _attention(q, k_s, v_s, lam, attn_g, tok_p, n_s, seq_s, lam_inits[l], l, cache))
        yc = (_pool(z, w_pl, scale_c, 0, n_p, seq_p, l), _pool(z, w_pl, scale_c, tok_p, n_s, seq_s, l))

        x = _merge(rows, x, pat, ng, ((ya_p, ya_s), yb, yc), w_in, w_branch, w_out, l)
        last = l == DEPTH - 1
        x = _ffn(rows, x, pat, ng, w_ffn_in, w_ffn_out, l, 1, 6, final_norm_g[None, :] if last else None)
        new_s.append(h_fin)

    y_p, y_s = x
    new_state = jnp.stack(new_s, axis=0).transpose(3, 0, 1, 2, 4)
    return (y_p.reshape(n_p, seq_p, D_MODEL), y_s.reshape(n_s, seq_s, D_MODEL),
            new_cache[0].reshape(n_p, DEPTH, seq_p, N_HEADS, 2, DH), new_cache[1].reshape(n_p, DEPTH, seq_p, N_HEADS, DV),
            new_state.reshape(n_p, DEPTH, 2, 2, S5_GROUPS, S5_STATE))
```

```python
import functools
import math

import jax
import jax.numpy as jnp
from jax import lax
from jax.experimental import pallas as pl
from jax.experimental.pallas import tpu as pltpu

F32 = jnp.float32
BF16 = jnp.bfloat16

D_MODEL = 1024
DEPTH = 4
GRID_W = 64
D_BRANCH = 512
S5_GROUP = 16
S5_GROUPS = 32
S5_STATE = 64
N_STATE = S5_GROUPS * S5_STATE
DH = 64
N_HEADS = 4
DV = 128
POOL_WINDOWS = (2, 4, 8, 16)
POOL_GROUP = 128
POOL_PAD = 16
D_FF = 2816
N_MOD = 9
N_BRANCH = 3
N_PROJ = 5
ROPE_BASE = 10000.0
EPS = 1e-6

LANES = 128
N_SLAB = D_BRANCH // LANES
SLAB_STATE = N_STATE // N_SLAB
ROW_TILE = 1024
MERGE_ROW_TILE = 512
FF_CHUNK = 256
S5_CHUNK_ROWS = 1024
Q_TILE = 512
VMEM_LIMIT = 56 * 1024 * 1024


def _cparams(sem):
    return pltpu.CompilerParams(dimension_semantics=sem, vmem_limit_bytes=VMEM_LIMIT)


def _resident(block_shape, index_map):
    return pl.BlockSpec(block_shape, index_map, pipeline_mode=pl.Buffered(1))


def _norm_mod(x, g, shift, scale):
    ms = jnp.mean(x * x, axis=-1, keepdims=True)
    y = (x * lax.rsqrt(ms + EPS)) * g
    return y * (1.0 + scale) + shift


def _dot(a, b):
    return jnp.dot(a, b, preferred_element_type=F32)


class _Rows:
    def __init__(self, n_p, seq_p, n_s, seq_s, tile):
        self.n_p, self.seq_p, self.n_s, self.seq_s, self.tile = n_p, seq_p, n_s, seq_s, tile
        self.tok_p = n_p * seq_p
        self.tok_s = n_s * seq_s
        self.n_tok = self.tok_p + self.tok_s
        assert self.tok_p % tile == 0 and seq_s % tile == 0 and tile % seq_p == 0
        self.tiles_p = self.tok_p // tile

    def mod_row(self, i):
        return jnp.where(i < self.tiles_p, 0, 1 + (i - self.tiles_p) // (self.seq_s // self.tile))


def _adaln_kernel(c_ref, w_ref, b_ref, o_ref):
    c = c_ref[...]
    s = (c * jax.nn.sigmoid(c)).astype(BF16)
    o_ref[...] = _dot(s, w_ref[...].astype(BF16)) + b_ref[...]


def _adaln(cvec, w_mod, b_mod):
    tn = 1024
    n_rows = cvec.shape[0]
    return pl.pallas_call(
        _adaln_kernel,
        grid=(DEPTH, N_MOD * D_MODEL // tn),
        in_specs=[
            pl.BlockSpec((n_rows, D_MODEL), lambda l, j: (0, 0)),
            pl.BlockSpec((None, D_MODEL, tn), lambda l, j: (l, 0, j)),
            pl.BlockSpec((None, 1, tn), lambda l, j: (l, 0, j)),
        ],
        out_specs=pl.BlockSpec((None, n_rows, tn), lambda l, j: (l, 0, j)),
        out_shape=jax.ShapeDtypeStruct((DEPTH, n_rows, N_MOD * D_MODEL), F32),
        compiler_params=_cparams(("parallel", "parallel")),
        name="adaln",
    )(cvec, w_mod, b_mod.reshape(DEPTH, 1, N_MOD * D_MODEL))


def _ffn_kernel(*refs, k0, tiles_p, n_chunks, split_in, final):
    refs = list(refs)
    s = pl.program_id(0)
    tile = jnp.maximum(s - (n_chunks - 1), 0)
    prompt = tile < tiles_p
    if split_in:
        xp_ref, xs_ref = refs[:2]
        refs = refs[2:]
        x = jnp.where(prompt, xp_ref[...], xs_ref[...])
    else:
        x = refs.pop(0)[...]
    pat_ref, g_ref, wa_ref, wb_ref, wo_ref = refs[:5]
    refs = refs[5:]
    wa_res, wb_res, wo_res, n_scr, acc_scr, act_scr = refs[-6:]
    refs = refs[:-6]

    def normed():
        return _norm_mod(x, g_ref[...], pat_ref[k0:k0 + 1], pat_ref[k0 + 1:k0 + 2]).astype(BF16)

    def finish(y):
        out = x + (0.5 * pat_ref[k0 + 2:k0 + 3]) * y
        if not final:
            refs[0][...] = out
            return
        fg_ref, op_ref, os_ref = refs
        ms = jnp.mean(out * out, axis=-1, keepdims=True)
        out = (out * lax.rsqrt(ms + EPS)) * fg_ref[...]

        @pl.when(prompt)
        def _():
            op_ref[...] = out

        @pl.when(jnp.logical_not(prompt))
        def _():
            os_ref[...] = out

    @pl.when(s < n_chunks)
    def _():
        @pl.when(s == 0)
        def _():
            n_scr[...] = normed()
            acc_scr[...] = jnp.zeros_like(acc_scr)

        wa = wa_ref[...].astype(BF16)
        wb = wb_ref[...].astype(BF16)
        wo = wo_ref[...].astype(BF16)
        wa_res[s] = wa
        wb_res[s] = wb
        wo_res[pl.ds(pl.multiple_of(s * FF_CHUNK, FF_CHUNK), FF_CHUNK), :] = wo
        n = n_scr[...]
        a = _dot(n, wa)
        b = _dot(n, wb)
        acc_scr[...] += _dot((a * jax.nn.sigmoid(a) * b).astype(BF16), wo)

        @pl.when(s == n_chunks - 1)
        def _():
            finish(acc_scr[...])

    @pl.when(s >= n_chunks)
    def _():
        n = normed()
        for j in range(n_chunks):
            a = _dot(n, wa_res[j])
            b = _dot(n, wb_res[j])
            act_scr[:, j * FF_CHUNK:(j + 1) * FF_CHUNK] = (a * jax.nn.sigmoid(a) * b).astype(BF16)
        finish(_dot(act_scr[...], wo_res[...]))


def _ffn(rows, x, pat, norm_g, w_ffn_in, w_ffn_out, layer, sub, k0, final_g=None):
    split_in = isinstance(x, tuple)
    final = final_g is not None
    n_chunks = D_FF // FF_CHUNK
    tiles_p = rows.tiles_p
    tile_of = lambda s: jnp.maximum(s - (n_chunks - 1), 0)
    chunk_of = lambda s: jnp.minimum(s, n_chunks - 1)
    tile = (rows.tile, D_MODEL)
    p_spec = pl.BlockSpec(tile, lambda s: (jnp.minimum(tile_of(s), tiles_p - 1), 0))
    s_spec = pl.BlockSpec(tile, lambda s: (jnp.maximum(tile_of(s) - tiles_p, 0), 0))
    all_spec = pl.BlockSpec(tile, lambda s: (tile_of(s), 0))
    in_specs = ([p_spec, s_spec] if split_in else [all_spec]) + [
        pl.BlockSpec((None, None, N_MOD, D_MODEL), lambda s: (layer, rows.mod_row(tile_of(s)), 0, 0)),
        pl.BlockSpec((None, None, 1, D_MODEL), lambda s: (layer, 2 * sub, 0, 0)),
        pl.BlockSpec((None, None, D_MODEL, FF_CHUNK), lambda s: (layer, sub, 0, chunk_of(s))),
        pl.BlockSpec((None, None, D_MODEL, FF_CHUNK), lambda s: (layer, sub, 0, n_chunks + chunk_of(s))),
        pl.BlockSpec((None, None, FF_CHUNK, D_MODEL), lambda s: (layer, sub, chunk_of(s), 0)),
    ]
    args = (list(x) if split_in else [x]) + [pat, norm_g, w_ffn_in, w_ffn_in, w_ffn_out]
    if final:
        in_specs.append(pl.BlockSpec((1, D_MODEL), lambda s: (0, 0)))
        args.append(final_g)
        out_specs = [p_spec, s_spec]
        out_shape = [jax.ShapeDtypeStruct((rows.tok_p, D_MODEL), F32), jax.ShapeDtypeStruct((rows.tok_s, D_MODEL), F32)]
    else:
        out_specs = all_spec
        out_shape = jax.ShapeDtypeStruct((rows.n_tok, D_MODEL), F32)
    return pl.pallas_call(
        functools.partial(_ffn_kernel, k0=k0, tiles_p=tiles_p, n_chunks=n_chunks, split_in=split_in, final=final),
        grid=(n_chunks - 1 + rows.n_tok // rows.tile,),
        in_specs=in_specs,
        out_specs=out_specs,
        out_shape=out_shape,
        scratch_shapes=[
            pltpu.VMEM((n_chunks, D_MODEL, FF_CHUNK), BF16),
            pltpu.VMEM((n_chunks, D_MODEL, FF_CHUNK), BF16),
            pltpu.VMEM((D_FF, D_MODEL), BF16),
            pltpu.VMEM((rows.tile, D_MODEL), BF16),
            pltpu.VMEM((rows.tile, D_MODEL), F32),
            pltpu.VMEM((rows.tile, D_FF), BF16),
        ],
        compiler_params=_cparams(("arbitrary",)),
        name="ffn",
    )(*args)


def _rope(x, cos, sin):
    lane = lax.broadcasted_iota(jnp.int32, (x.shape[0], LANES), 1)
    first_half = (lane % 32) < 16
    out = []
    for c in range(x.shape[1] // LANES):
        xc = x[:, c * LANES:(c + 1) * LANES]
        partner = jnp.where(first_half, pltpu.roll(xc, LANES - 16, 1), pltpu.roll(xc, 16, 1))
        out.append(xc * cos + partner * sin)
    return jnp.concatenate(out, axis=1)


def _inproj_kernel(*refs, tiles_p, n_alias):
    x_ref, pat_ref, g_ref, w_ref, cos_ref, sin_ref = refs[:6]
    up_ref, us_ref, q_ref, ks_ref, vs_ref, kc_ref, vc_ref, z_ref, w_res, n_scr = refs[6 + n_alias:]
    s = pl.program_id(0)
    latent = jnp.maximum(s - (N_PROJ - 1), 0) >= tiles_p
    scale = DH ** -0.5 * math.log2(math.e)

    def normed():
        return _norm_mod(x_ref[...], g_ref[...], pat_ref[3:4], pat_ref[4:5]).astype(BF16)

    def store_prompt(c, y):
        if c == 0:
            up_ref[...] = y
        elif c == 1:
            q_ref[...] = (y * scale).astype(BF16)
        elif c == 2:
            kc_ref[...] = y.reshape(kc_ref.shape)
        elif c == 3:
            vc_ref[...] = y.reshape(vc_ref.shape)
        else:
            z_ref[...] = y

    @pl.when(s < N_PROJ)
    def _():
        @pl.when(s == 0)
        def _():
            n_scr[...] = normed()

        w = w_ref[...].astype(BF16)
        w_res[s] = w
        y = _dot(n_scr[...], w)
        for c in range(N_PROJ):
            @pl.when(s == c)
            def _(c=c):
                store_prompt(c, y)

    @pl.when(s >= N_PROJ)
    def _():
        n = normed()
        col = lambda c: _dot(n, w_res[c])

        @pl.when(latent)
        def _():
            q_ref[...] = (_rope(col(1), cos_ref[...], sin_ref[...]) * scale).astype(BF16)
            ks_ref[...] = _rope(col(2), cos_ref[...], sin_ref[...]).astype(BF16)
            us_ref[...] = col(0)
            vs_ref[...] = col(3).astype(BF16)
            z_ref[...] = col(4)

        @pl.when(jnp.logical_not(latent))
        def _():
            for c in range(N_PROJ):
                store_prompt(c, col(c))


def _inproj(rows, x, pat, norm_g, w_in, layer, rope_tabs, cache_out):
    tiles_p = rows.tiles_p
    seqs_per_tile = rows.tile // rows.seq_p
    tabs_per_seq = rows.seq_s // rows.tile
    tile_of = lambda s: jnp.maximum(s - (N_PROJ - 1), 0)
    tab_spec = pl.BlockSpec((rows.tile,LANES), lambda s: (jnp.maximum(tile_of(s) - tiles_p, 0) % tabs_per_seq, 0))
    tile = (rows.tile,D_BRANCH)
    p_spec = pl.BlockSpec(tile, lambda s: (jnp.minimum(tile_of(s), tiles_p - 1), 0))
    s_spec = pl.BlockSpec(tile, lambda s: (jnp.maximum(tile_of(s) - tiles_p, 0), 0))
    all_spec = pl.BlockSpec(tile, lambda s: (tile_of(s), 0))
    cache_spec = pl.BlockSpec((seqs_per_tile, None, rows.seq_p, D_BRANCH),
                              lambda s: (jnp.minimum(tile_of(s), tiles_p - 1), layer, 0, 0))
    cache_shape = jax.ShapeDtypeStruct((rows.n_p, DEPTH, rows.seq_p, D_BRANCH), F32)
    in_specs = [
        pl.BlockSpec((rows.tile,D_MODEL), lambda s: (tile_of(s), 0)),
        pl.BlockSpec((None, None, N_MOD, D_MODEL), lambda s: (layer, rows.mod_row(tile_of(s)), 0, 0)),
        pl.BlockSpec((None, None, 1, D_MODEL), lambda s: (layer, 1, 0, 0)),
        pl.BlockSpec((None, D_MODEL, D_BRANCH), lambda s: (layer, 0, jnp.minimum(s, N_PROJ - 1))),
        tab_spec, tab_spec,
    ]
    args = [x, pat, norm_g, w_in, *rope_tabs]
    aliases = {}
    if cache_out is not None:
        in_specs += [pl.BlockSpec(memory_space=pl.ANY)] * 2
        aliases = {len(args): 5, len(args) + 1: 6}
        args += list(cache_out)
    return pl.pallas_call(
        functools.partial(_inproj_kernel, tiles_p=tiles_p, n_alias=len(aliases)),
        grid=(N_PROJ - 1 + rows.n_tok // rows.tile,),
        in_specs=in_specs,
        out_specs=[p_spec, s_spec, all_spec, s_spec, s_spec, cache_spec, cache_spec, all_spec],
        out_shape=[
            jax.ShapeDtypeStruct((rows.tok_p, D_BRANCH), F32),
            jax.ShapeDtypeStruct((rows.tok_s, D_BRANCH), F32),
            jax.ShapeDtypeStruct((rows.n_tok, D_BRANCH), BF16),
            jax.ShapeDtypeStruct((rows.tok_s, D_BRANCH), BF16),
            jax.ShapeDtypeStruct((rows.tok_s, D_BRANCH), BF16),
            cache_shape, cache_shape,
            jax.ShapeDtypeStruct((rows.n_tok, D_BRANCH), F32),
        ],
        scratch_shapes=[pltpu.VMEM((N_PROJ, D_MODEL, D_BRANCH), BF16), pltpu.VMEM((rows.tile,D_MODEL), BF16)],
        input_output_aliases=aliases,
        compiler_params=_cparams(("arbitrary",)),
        name="inproj",
    )(*args)


def _s5_kernel(*refs, n_seq, n_steps, backward):
    if backward:
        u_ref, part_ref, w1_ref, w2_ref, w3_ref, ar_ref, ai_ref, h0_ref, wglu_ref, ya_ref, hfin_ref, xs_scr, h_scr = refs
    else:
        u_ref, dsk_ref, w1_ref, w2_ref, w3_ref, ar_ref, ai_ref, h0_ref, part_ref, hfin_ref, xs_scr, h_scr = refs
    rows = n_seq * n_steps
    pairs = n_steps // 2
    half = pairs * n_seq

    @pl.when(pl.program_id(0) == 0)
    def _():
        h_scr[...] = h0_ref[...]

    u = jnp.swapaxes(u_ref[...], 0, 1).reshape(rows, D_BRANCH)
    u4 = u.reshape(pairs, 2, n_seq, D_BRANCH)
    u_even = u4[:, 0].reshape(half, D_BRANCH).astype(BF16)
    u_odd = u4[:, 1].reshape(half, D_BRANCH).astype(BF16)
    u_first = u_odd if backward else u_even
    base = 0 if backward else n_seq
    incoming = half if backward else 0
    order = range(pairs - 1, -1, -1) if backward else range(pairs)

    def project(j):
        cols = slice(j * LANES, (j + 1) * LANES)
        lhs = jnp.concatenate([u_even[:, cols], u_odd[:, cols]], axis=1)
        xs_scr[j, base:base + half, :] = _dot(lhs, w1_ref[j])

    def scan(j):
        lanes = slice(j * SLAB_STATE, (j + 1) * SLAB_STATE)
        ar = jnp.broadcast_to(ar_ref[:, lanes], (n_seq, SLAB_STATE))
        ai = jnp.broadcast_to(ai_ref[:, lanes], (n_seq, SLAB_STATE))
        hr = h_scr[0, :, lanes]
        hi = h_scr[1, :, lanes]
        xs_scr[j, incoming:incoming + n_seq, 0:SLAB_STATE] = hr
        xs_scr[j, incoming:incoming + n_seq, SLAB_STATE:2 * SLAB_STATE] = hi
        for q in order:
            r = slice(base + q * n_seq, base + (q + 1) * n_seq)
            xr = xs_scr[j, r, 0:SLAB_STATE]
            xi = xs_scr[j, r, SLAB_STATE:2 * SLAB_STATE]
            hr, hi = ar * hr - ai * hi + xr, ar * hi + ai * hr + xi
            xs_scr[j, r, 0:SLAB_STATE] = hr
            xs_scr[j, r, SLAB_STATE:2 * SLAB_STATE] = hi
        h_scr[0, :, lanes] = hr
        h_scr[1, :, lanes] = hi

    def read_out(j):
        z = _dot(xs_scr[j].astype(BF16), w2_ref[j])
        direct = _dot(u_first[:, j * LANES:(j + 1) * LANES], w3_ref[j])
        lo = z[0:half]
        hi = z[n_seq:n_seq + half]
        if backward:
            even, odd = lo[:, 0:LANES], hi[:, LANES:2 * LANES] + direct
        else:
            even, odd = lo[:, LANES:2 * LANES] + direct, hi[:, 0:LANES]
        both = jnp.stack([even.reshape(pairs, n_seq, LANES), odd.reshape(pairs, n_seq, LANES)], axis=1)
        return both.reshape(rows, LANES)

    ys = []
    project(0)
    for j in range(N_SLAB):
        if j + 1 < N_SLAB:
            project(j + 1)
        scan(j)
        ys.append(read_out(j))
    y = jnp.concatenate(ys, axis=1)

    if backward:
        g = jax.nn.gelu(part_ref[...] + y)
        ya = g * jax.nn.sigmoid(_dot(g.astype(BF16), wglu_ref[...]))
        ya_ref[...] = jnp.swapaxes(ya.reshape(n_steps, n_seq, D_BRANCH), 0, 1).astype(BF16)
    else:
        part_ref[...] = y + dsk_ref[...] * u

    @pl.when(pl.program_id(0) == pl.num_programs(0) - 1)
    def _():
        hfin_ref[...] = h_scr[...]


def _s5(u, ssm, h0, h0_row, n_seq, layer):
    n_tok = u.shape[0]
    seq = n_tok // n_seq
    n_steps = S5_CHUNK_ROWS // n_seq
    n_chunks = seq // n_steps
    u3 = u.reshape(n_seq, seq, D_BRANCH)

    def specs(backward):
        d = int(backward)
        chunk = (lambda s: n_chunks - 1 - s) if backward else (lambda s: s)
        seq_block = pl.BlockSpec((n_seq, n_steps, D_BRANCH), lambda s: (0, chunk(s), 0))
        part_block = pl.BlockSpec((S5_CHUNK_ROWS, D_BRANCH), lambda s: (chunk(s), 0))
        params = [
            pl.BlockSpec((None, None, N_SLAB, 2 * LANES, 2 * SLAB_STATE), lambda s: (layer, d, 0, 0, 0)),
            pl.BlockSpec((None, None, N_SLAB, 2 * SLAB_STATE, 2 * LANES), lambda s: (layer, d, 0, 0, 0)),
            pl.BlockSpec((None, None, N_SLAB, LANES, LANES), lambda s: (layer, d, 0, 0, 0)),
            pl.BlockSpec((None, None, 1, N_STATE), lambda s: (layer, d, 0, 0)),
            pl.BlockSpec((None, None, 1, N_STATE), lambda s: (layer, d, 0, 0)),
            pl.BlockSpec((None, None, 2, n_seq, N_STATE), lambda s: (h0_row, d, 0, 0, 0)),
        ]
        return seq_block, part_block, params

    param_args = (ssm["w1"], ssm["w2"], ssm["w3"], ssm["a2_re"], ssm["a2_im"], h0)
    fin_spec = pl.BlockSpec((2, n_seq, N_STATE), lambda s: (0, 0, 0))
    fin_shape = jax.ShapeDtypeStruct((2, n_seq, N_STATE), F32)
    state_rows = S5_CHUNK_ROWS // 2 + n_seq
    scratch = [pltpu.VMEM((N_SLAB, state_rows, 2 * SLAB_STATE), F32), pltpu.VMEM((2, n_seq, N_STATE), F32)]

    seq_block, part_block, params = specs(False)
    part, fin_f = pl.pallas_call(
        functools.partial(_s5_kernel, n_seq=n_seq, n_steps=n_steps, backward=False),
        grid=(n_chunks,),
        in_specs=[seq_block, pl.BlockSpec((None, 1, D_BRANCH), lambda s: (layer, 0, 0))] + params,
        out_specs=[part_block, fin_spec],
        out_shape=[jax.ShapeDtypeStruct((n_tok, D_BRANCH), F32), fin_shape],
        scratch_shapes=scratch,
        compiler_params=_cparams(("arbitrary",)),
        name="s5_fwd",
    )(u3, ssm["d_skip"], *param_args)

    seq_block, part_block, params = specs(True)
    ya, fin_b = pl.pallas_call(
        functools.partial(_s5_kernel, n_seq=n_seq, n_steps=n_steps, backward=True),
        grid=(n_chunks,),
        in_specs=[seq_block, part_block] + params + [pl.BlockSpec((None, D_BRANCH, D_BRANCH), lambda s: (layer, 0, 0))],
        out_specs=[seq_block, fin_spec],
        out_shape=[jax.ShapeDtypeStruct((n_seq, seq, D_BRANCH), BF16), fin_shape],
        scratch_shapes=scratch,
        compiler_params=_cparams(("arbitrary",)),
        name="s5_bwd",
    )(u3, part, *param_args, ssm["w_glu"])
    return ya.reshape(n_tok, D_BRANCH), jnp.stack([fin_f, fin_b], axis=0)


def _s5_params(lam_re, lam_im, log_dt, b_re, b_im, c_re, c_im):
    dt = jnp.exp(log_dt)[..., None]
    mag = jnp.exp(lam_re * dt)
    abr = mag * jnp.cos(lam_im * dt)
    abi = mag * jnp.sin(lam_im * dt)
    den = lam_re * lam_re + lam_im * lam_im
    nr = abr - 1.0
    kr = (nr * lam_re + abi * lam_im) / den
    ki = (abi * lam_re - nr * lam_im) / den
    bbr = kr[..., None] * b_re - ki[..., None] * b_im
    bbi = kr[..., None] * b_im + ki[..., None] * b_re
    per_slab = S5_GROUPS // N_SLAB

    def block_diag(w, n_in, n_out):
        t = w.reshape(2, N_SLAB, per_slab, n_out, n_in).transpose(0, 1, 2, 4, 3)
        t = jnp.tile(t.reshape(2, N_SLAB, per_slab * n_in, n_out), (1, 1, 1, per_slab))
        row_group = lax.broadcasted_iota(jnp.int32, t.shape[-2:], 0) // n_in
        col_group = lax.broadcasted_iota(jnp.int32, t.shape[-2:], 1) // n_out
        return jnp.where(row_group == col_group, t, 0.0)

    def pack_in(w):
        return block_diag(w, S5_GROUP, S5_STATE)

    def pack_out(w):
        return block_diag(w, S5_STATE, S5_GROUP)

    abbr = abr[..., None] * bbr - abi[..., None] * bbi
    abbi = abr[..., None] * bbi + abi[..., None] * bbr
    car = c_re * abr[:, :, None, :] - c_im * abi[:, :, None, :]
    cai = c_re * abi[:, :, None, :] + c_im * abr[:, :, None, :]
    b_pack = jnp.concatenate([pack_in(bbr), pack_in(bbi)], axis=-1)
    ab_pack = jnp.concatenate([pack_in(abbr), pack_in(abbi)], axis=-1)
    w1 = jnp.stack([jnp.concatenate([ab_pack[0], b_pack[0]], axis=-2),
                    jnp.concatenate([b_pack[1], ab_pack[1]], axis=-2)]).astype(BF16)
    c_pack = jnp.concatenate([pack_out(c_re), -pack_out(c_im)], axis=-2)
    ca_pack = jnp.concatenate([pack_out(car), -pack_out(cai)], axis=-2)
    w2 = jnp.concatenate([c_pack, ca_pack], axis=-1).astype(BF16)
    direct = jnp.einsum("dgop,dgpi->dgoi", c_re, bbr) - jnp.einsum("dgop,dgpi->dgoi", c_im, bbi)
    w3 = block_diag(direct, S5_GROUP, S5_GROUP).astype(BF16)
    a2r = abr * abr - abi * abi
    a2i = 2.0 * abr * abi
    return dict(w1=w1, w2=w2, w3=w3, a2_re=a2r.reshape(2, 1, N_STATE), a2_im=a2i.reshape(2, 1, N_STATE))


def _attn_kernel(*refs, n_ctx, lam_init, layer):
    if n_ctx:
        lam_ref, q_ref, k_ref, v_ref, ck_ref, cv_ref, g_ref, o_ref, k_scr, v_scr = refs
    else:
        lam_ref, q_ref, k_ref, v_ref, g_ref, o_ref, k_scr, v_scr = refs
    n_own = k_ref.shape[0]

    @pl.when(pl.program_id(1) == 0)
    def _():
        k_scr[0:n_own, :] = k_ref[...].astype(BF16)
        v_scr[0:n_own, :] = v_ref[...].astype(BF16)
        if n_ctx:
            k_scr[n_own:n_own + n_ctx, :] = ck_ref[...].astype(BF16)
            v_scr[n_own:n_own + n_ctx, :] = cv_ref[...].astype(BF16)

    lam = lam_ref[layer]
    q = q_ref[...]
    lane = lax.broadcasted_iota(jnp.int32, (q.shape[0], LANES), 1)

    def scores(h):
        lanes = slice(h * LANES, (h + 1) * LANES)
        qh = q[:, lanes]
        kh = k_scr[:, lanes]
        out = []
        for m in range(2):
            qm = jnp.where((lane < DH) if m == 0 else (lane >= DH), qh, jnp.zeros_like(qh))
            out.append(lax.dot_general(qm, kh, (((1,), (1,)), ((), ())), preferred_element_type=F32))
        return out

    sc = scores(0)
    for h in range(N_HEADS):
        lanes = slice(h * LANES, (h + 1) * LANES)
        nxt = scores(h + 1) if h + 1 < N_HEADS else None
        e1, e2 = [jnp.exp2(s_m - jnp.max(s_m, axis=-1, keepdims=True)) for s_m in sc]
        l1 = jnp.sum(e1, axis=-1, keepdims=True)
        l2 = jnp.sum(e2, axis=-1, keepdims=True)
        a = (e1 - e2 * (lam * l1 / l2)).astype(BF16)
        o = _dot(a, v_scr[:, lanes]) * (1.0 / l1)
        ms = jnp.mean(o * o, axis=-1, keepdims=True)
        o_ref[:, lanes] = (((o * lax.rsqrt(ms + EPS)) * g_ref[...]) * (1.0 - lam_init)).astype(BF16)
        sc = nxt


def _attention(q, k, v, lam, attn_g, row0, n_seq, seq, lam_init, layer, cache=None):
    tq = min(Q_TILE, seq)
    q_tiles = seq // tq
    q0 = row0 // tq
    n_ctx = 0 if cache is None else cache[0].shape[2]
    if k.ndim == 4:
        own = pl.BlockSpec((None, None, seq, D_BRANCH), lambda b, i: (b, layer, 0, 0))
    else:
        own = pl.BlockSpec((seq, D_BRANCH), lambda b, i: (b, 0))
    in_specs = [
        pl.BlockSpec(memory_space=pltpu.SMEM),
        pl.BlockSpec((tq, D_BRANCH), lambda b, i: (q0 + b * q_tiles + i, 0)),
        own, own,
    ]
    args = [lam, q, k, v]
    if n_ctx:
        in_specs += [pl.BlockSpec((None, None, n_ctx, D_BRANCH), lambda b, i: (b, layer, 0, 0))] * 2
        args += list(cache)
    in_specs.append(pl.BlockSpec((None, 1, DV), lambda b, i: (layer, 0, 0)))
    args.append(attn_g)
    return pl.pallas_call(
        functools.partial(_attn_kernel, n_ctx=n_ctx, lam_init=lam_init, layer=layer),
        grid=(n_seq, q_tiles),
        in_specs=in_specs,
        out_specs=pl.BlockSpec((tq, D_BRANCH), lambda b, i: (b * q_tiles + i, 0)),
        out_shape=jax.ShapeDtypeStruct((n_seq * seq, D_BRANCH), BF16),
        scratch_shapes=[pltpu.VMEM((seq + n_ctx, D_BRANCH), BF16)] * 2,
        compiler_params=_cparams(("parallel", "arbitrary")),
        name="attn",
    )(*args)


def _pool_kernel(z_ref, w_ref, sc_ref, o_ref, pad_scr):
    seq = z_ref.shape[0]
    padded = seq + 2 * POOL_PAD
    z = z_ref[...]
    zeros = jnp.zeros((POOL_PAD, D_BRANCH), F32)
    pad_scr[0:POOL_PAD, :] = zeros
    pad_scr[POOL_PAD:POOL_PAD + seq, :] = z
    pad_scr[POOL_PAD + seq:padded, :] = zeros
    t = lax.broadcasted_iota(jnp.int32, (seq, POOL_GROUP), 0)
    outs = []
    for gi, w in enumerate(POOL_WINDOWS):
        lanes = slice(gi * POOL_GROUP, (gi + 1) * POOL_GROUP)
        run = pad_scr[:, lanes]
        span = 1
        while span < w:
            run = run + pltpu.roll(run, padded - span, 0)
            span *= 2
        tot = pltpu.roll(run, w // 2, 0)[POOL_PAD:POOL_PAD + seq]
        cnt = jnp.minimum(t + w // 2, seq) - jnp.maximum(t - w // 2, 0)
        pooled = tot / cnt.astype(F32) - z[:, lanes]
        outs.append(_dot(pooled.astype(BF16), w_ref[gi]))
    o_ref[...] = (jnp.concatenate(outs, axis=1) * sc_ref[...]).astype(BF16)


def _pool(z, w_pool, pool_scale, row0, n_seq, seq, layer):
    s0 = row0 // seq
    return pl.pallas_call(
        _pool_kernel,
        grid=(n_seq,),
        in_specs=[
            pl.BlockSpec((seq, D_BRANCH), lambda b: (s0 + b, 0)),
            pl.BlockSpec((None, len(POOL_WINDOWS), POOL_GROUP, POOL_GROUP), lambda b: (layer, 0, 0, 0)),
            pl.BlockSpec((None, 1, D_BRANCH), lambda b: (layer, 0, 0)),
        ],
        out_specs=pl.BlockSpec((seq, D_BRANCH), lambda b: (b, 0)),
        out_shape=jax.ShapeDtypeStruct((n_seq * seq, D_BRANCH), BF16),
        scratch_shapes=[pltpu.VMEM((seq + 2 * POOL_PAD, D_BRANCH), F32)],
        compiler_params=_cparams(("parallel",)),
        name="pool",
    )(z, w_pool, pool_scale)


def _merge_kernel(x_ref, pat_ref, g_ref, *refs, tiles_p):
    branch_refs = refs[:2 * N_BRANCH]
    wg0_ref, wg1_ref, wbr_ref, wo_ref, o_ref, wg_res, wbr_res, wo_res, n_scr, m_scr = refs[2 * N_BRANCH:]
    s = pl.program_id(0)
    prompt = jnp.maximum(s - (N_BRANCH - 1), 0) < tiles_p
    x = x_ref[...]

    def normed():
        return _norm_mod(x, g_ref[...], pat_ref[3:4], pat_ref[4:5]).astype(BF16)

    def branch_in(br):
        return jnp.where(prompt, branch_refs[2 * br][...], branch_refs[2 * br + 1][...])

    def finish(merged):
        o_ref[...] = x + pat_ref[5:6] * _dot(merged.astype(BF16), wo_res[...])

    @pl.when(s < N_BRANCH)
    def _():
        @pl.when(s == 0)
        def _():
            n_scr[...] = normed()
            m_scr[...] = jnp.zeros_like(m_scr)
            wo_res[...] = wo_ref[...].astype(BF16)

        wg = jnp.concatenate([wg0_ref[...], wg1_ref[...]], axis=1).astype(BF16)
        wbr = wbr_ref[...].astype(BF16)
        wg_res[s] = wg
        wbr_res[s] = wbr
        gate = jax.nn.sigmoid(_dot(n_scr[...], wg))
        for br in range(N_BRANCH):
            @pl.when(s == br)
            def _(br=br):
                m_scr[...] += gate * _dot(branch_in(br), wbr)

        @pl.when(s == N_BRANCH - 1)
        def _():
            finish(m_scr[...])

    @pl.when(s >= N_BRANCH)
    def _():
        n = normed()
        merged = None
        for br in range(N_BRANCH):
            part = jax.nn.sigmoid(_dot(n, wg_res[br])) * _dot(branch_in(br), wbr_res[br])
            merged = part if merged is None else merged + part
        finish(merged)


def _merge(rows, x, pat, norm_g, branches, w_in, w_branch, w_out, layer):
    tiles_p = rows.tiles_p
    half = D_MODEL // 2
    gate0 = N_PROJ * D_BRANCH // half
    tile_of = lambda s: jnp.maximum(s - (N_BRANCH - 1), 0)
    br_of = lambda s: jnp.minimum(s, N_BRANCH - 1)
    p_spec = pl.BlockSpec((rows.tile,D_BRANCH), lambda s: (jnp.minimum(tile_of(s), tiles_p - 1), 0))
    s_spec = pl.BlockSpec((rows.tile,D_BRANCH), lambda s: (jnp.maximum(tile_of(s) - tiles_p, 0), 0))
    return pl.pallas_call(
        functools.partial(_merge_kernel, tiles_p=tiles_p),
        grid=(N_BRANCH - 1 + rows.n_tok // rows.tile,),
        in_specs=[
            pl.BlockSpec((rows.tile,D_MODEL), lambda s: (tile_of(s), 0)),
            pl.BlockSpec((None, None, N_MOD, D_MODEL), lambda s: (layer, rows.mod_row(tile_of(s)), 0, 0)),
            pl.BlockSpec((None, None, 1, D_MODEL), lambda s: (layer, 1, 0, 0)),
            p_spec, s_spec, p_spec, s_spec, p_spec, s_spec,
            pl.BlockSpec((None, D_MODEL, half), lambda s: (layer, 0, gate0 + 2 * br_of(s))),
            pl.BlockSpec((None, D_MODEL, half), lambda s: (layer, 0, gate0 + 2 * br_of(s) + 1)),
            pl.BlockSpec((None, None, D_BRANCH, D_MODEL), lambda s: (layer, br_of(s), 0, 0)),
            _resident((None, D_MODEL, D_MODEL), lambda s: (layer, 0, 0)),
        ],
        out_specs=pl.BlockSpec((rows.tile,D_MODEL), lambda s: (tile_of(s), 0)),
        out_shape=jax.ShapeDtypeStruct((rows.n_tok, D_MODEL), F32),
        scratch_shapes=[
            pltpu.VMEM((N_BRANCH, D_MODEL, D_MODEL), BF16),
            pltpu.VMEM((N_BRANCH, D_BRANCH, D_MODEL), BF16),
            pltpu.VMEM((D_MODEL, D_MODEL), BF16),
            pltpu.VMEM((rows.tile,D_MODEL), BF16),
            pltpu.VMEM((rows.tile,D_MODEL), F32),
        ],
        compiler_params=_cparams(("arbitrary",)),
        name="merge",
    )(x, pat, norm_g, *[y for pair in branches for y in pair], w_in, w_in, w_branch, w_out)


def _rope_tables(seq):
    n_rows = seq // GRID_W
    row = jnp.repeat(jnp.arange(n_rows, dtype=F32), GRID_W)
    col = jnp.tile(jnp.arange(GRID_W, dtype=F32), n_rows)
    n_freq = DH // 4
    inv = ROPE_BASE ** (-jnp.arange(n_freq, dtype=F32) / n_freq)
    ar = row[:, None] * inv
    ac = col[:, None] * inv
    cos = jnp.concatenate([jnp.cos(ar)] * 2 + [jnp.cos(ac)] * 2, axis=1)
    sin = jnp.concatenate([-jnp.sin(ar), jnp.sin(ar), -jnp.sin(ac), jnp.sin(ac)], axis=1)
    return jnp.tile(cos, (1, LANES // DH)), jnp.tile(sin, (1, LANES // DH))


def kernel(x_prompt, x_sample, cache_k, cache_v, state_ssm, c, c_ctx, norm_g, w_mod, b_mod, w_ffn_in, w_ffn_out, w_in, ssm_lam_re, ssm_lam_im, ssm_log_dt, ssm_b_re, ssm_b_im, ssm_c_re, ssm_c_im, ssm_d, w_glu, lam_q1, lam_k1, lam_q2, lam_k2, attn_norm_g, w_pool, pool_scale, w_branch, w_out, final_norm_g):
    n_p, seq_p, _ = x_prompt.shape
    n_s, seq_s, _ = x_sample.shape
    n_past = cache_k.shape[2]
    rows = _Rows(n_p, seq_p, n_s, seq_s, ROW_TILE)
    merge_rows = _Rows(n_p, seq_p, n_s, seq_s, MERGE_ROW_TILE)
    tok_p = rows.tok_p

    cvec = jnp.concatenate([c_ctx[None, :], c, jnp.zeros((16 - 1 - n_s, D_MODEL), F32)], axis=0)
    pat = _adaln(cvec, w_mod, b_mod).reshape(DEPTH, 16, N_MOD, D_MODEL)

    w_pl = w_pool.astype(BF16)

    x = (x_prompt.reshape(tok_p, D_MODEL), x_sample.reshape(rows.tok_s, D_MODEL))
    rope_tabs = _rope_tables(seq_s)
    cache = (cache_k.reshape(n_s, DEPTH, n_past, D_BRANCH), cache_v.reshape(n_s, DEPTH, n_past, D_BRANCH))
    h0_p = jnp.zeros((1, 2, 2, n_p, N_STATE), F32)
    h0_s = state_ssm.reshape(n_s, DEPTH, 2, 2, N_STATE).transpose(1, 2, 3, 0, 4)

    ssm = jax.vmap(_s5_params)(ssm_lam_re, ssm_lam_im, ssm_log_dt, ssm_b_re, ssm_b_im, ssm_c_re, ssm_c_im)
    ssm.update(d_skip=ssm_d[:, None, :], w_glu=w_glu.astype(BF16))

    lam_inits = [0.8 - 0.6 * math.exp(-0.3 * l) for l in range(DEPTH)]
    lam = (jnp.exp(jnp.sum(lam_q1 * lam_k1, axis=-1)) - jnp.exp(jnp.sum(lam_q2 * lam_k2, axis=-1))
           + jnp.asarray(lam_inits, F32))
    ng = norm_g.reshape(DEPTH, 3, 1, D_MODEL)
    attn_g = attn_norm_g.reshape(DEPTH, 1, DV)
    scale_c = pool_scale.reshape(DEPTH, 1, D_BRANCH)

    new_cache, new_s = None, []
    for l in range(DEPTH):
        x = _ffn(merge_rows if l == 0 else rows, x, pat, ng, w_ffn_in, w_ffn_out, l, 0, 0)
        u_p, u_s, q, k_s, v_s, k_new, v_new, z = _inproj(rows, x, pat, ng, w_in, l, rope_tabs, new_cache)
        new_cache = (k_new, v_new)

        ya_p, h_fin = _s5(u_p, ssm, h0_p, 0, n_p, l)
        ya_s, _ = _s5(u_s, ssm, h0_s, l, n_s, l)
        yb = (_attention(q, k_new, v_new, lam, attn_g, 0, n_p, seq_p, lam_inits[l], l),
              _attention(q, k_s, v_s, lam, attn_g, tok_p, n_s, seq_s, lam_inits[l], l, cache))
        yc = (_pool(z, w_pl, scale_c, 0, n_p, seq_p, l), _pool(z, w_pl, scale_c, tok_p, n_s, seq_s, l))

        x = _merge(merge_rows, x, pat, ng, ((ya_p, ya_s), yb, yc), w_in, w_branch, w_out, l)
        last = l == DEPTH - 1
        x = _ffn(merge_rows if last else rows, x, pat, ng, w_ffn_in, w_ffn_out, l, 1, 6,
                 final_norm_g[None, :] if last else None)
        new_s.append(h_fin)

    y_p, y_s = x
    new_state = jnp.stack(new_s, axis=0).transpose(3, 0, 1, 2, 4)
    return (y_p.reshape(n_p, seq_p, D_MODEL), y_s.reshape(n_s, seq_s, D_MODEL),
            new_cache[0].reshape(n_p, DEPTH, seq_p, N_HEADS, 2, DH), new_cache[1].reshape(n_p, DEPTH, seq_p, N_HEADS, DV),
            new_state.reshape(n_p, DEPTH, 2, 2, S5_GROUPS, S5_STATE))
```

```python
import functools
import math

import jax
import jax.numpy as jnp
from jax import lax
from jax.experimental import pallas as pl
from jax.experimental.pallas import tpu as pltpu

F32 = jnp.float32
BF16 = jnp.bfloat16

D_MODEL = 1024
DEPTH = 4
GRID_W = 64
D_BRANCH = 512
S5_GROUP = 16
S5_GROUPS = 32
S5_STATE = 64
N_STATE = S5_GROUPS * S5_STATE
DH = 64
N_HEADS = 4
DV = 128
POOL_WINDOWS = (2, 4, 8, 16)
POOL_GROUP = 128
POOL_PAD = 16
D_FF = 2816
N_MOD = 9
N_BRANCH = 3
N_PROJ = 5
ROPE_BASE = 10000.0
EPS = 1e-6

LANES = 128
SUBLANES = 8
N_SLAB = D_BRANCH // LANES
SLAB_STATE = N_STATE // N_SLAB
ROW_TILE = 1024
MERGE_ROW_TILE = 512
FF_CHUNK = 256
S5_CHUNK_ROWS = 1024
Q_TILE = 512
VMEM_LIMIT = 56 * 1024 * 1024


def _cparams(sem):
    return pltpu.CompilerParams(dimension_semantics=sem, vmem_limit_bytes=VMEM_LIMIT)


def _resident(block_shape, index_map):
    return pl.BlockSpec(block_shape, index_map, pipeline_mode=pl.Buffered(1))


def _norm_mod(x, g, shift, scale):
    ms = jnp.mean(x * x, axis=-1, keepdims=True)
    y = (x * lax.rsqrt(ms + EPS)) * g
    return y * (1.0 + scale) + shift


def _dot(a, b):
    return jnp.dot(a, b, preferred_element_type=F32)


class _Rows:
    def __init__(self, n_p, seq_p, n_s, seq_s, tile):
        self.n_p, self.seq_p, self.n_s, self.seq_s, self.tile = n_p, seq_p, n_s, seq_s, tile
        self.tok_p = n_p * seq_p
        self.tok_s = n_s * seq_s
        self.n_tok = self.tok_p + self.tok_s
        assert self.tok_p % tile == 0 and seq_s % tile == 0 and tile % seq_p == 0
        self.tiles_p = self.tok_p // tile

    def mod_row(self, i):
        return jnp.where(i < self.tiles_p, 0, 1 + (i - self.tiles_p) // (self.seq_s // self.tile))


def _adaln_kernel(c_ref, w_ref, b_ref, o_ref):
    c = c_ref[...]
    s = (c * jax.nn.sigmoid(c)).astype(BF16)
    o_ref[...] = _dot(s, w_ref[...].astype(BF16)) + b_ref[...]


def _adaln(cvec, w_mod, b_mod):
    tn = 1024
    n_rows = cvec.shape[0]
    return pl.pallas_call(
        _adaln_kernel,
        grid=(DEPTH, N_MOD * D_MODEL // tn),
        in_specs=[
            pl.BlockSpec((n_rows, D_MODEL), lambda l, j: (0, 0)),
            pl.BlockSpec((None, D_MODEL, tn), lambda l, j: (l, 0, j)),
            pl.BlockSpec((None, 1, tn), lambda l, j: (l, 0, j)),
        ],
        out_specs=pl.BlockSpec((None, n_rows, tn), lambda l, j: (l, 0, j)),
        out_shape=jax.ShapeDtypeStruct((DEPTH, n_rows, N_MOD * D_MODEL), F32),
        compiler_params=_cparams(("parallel", "parallel")),
        name="adaln",
    )(cvec, w_mod, b_mod.reshape(DEPTH, 1, N_MOD * D_MODEL))


def _ffn_kernel(*refs, k0, tiles_p, n_chunks, split_in, final):
    refs = list(refs)
    s = pl.program_id(0)
    tile = jnp.maximum(s - (n_chunks - 1), 0)
    prompt = tile < tiles_p
    if split_in:
        xp_ref, xs_ref = refs[:2]
        refs = refs[2:]
        x = jnp.where(prompt, xp_ref[...], xs_ref[...])
    else:
        x = refs.pop(0)[...]
    pat_ref, g_ref, wa_ref, wb_ref, wo_ref = refs[:5]
    refs = refs[5:]
    wa_res, wb_res, wo_res, n_scr, acc_scr, act_scr = refs[-6:]
    refs = refs[:-6]

    def normed():
        return _norm_mod(x, g_ref[...], pat_ref[k0:k0 + 1], pat_ref[k0 + 1:k0 + 2]).astype(BF16)

    def finish(y):
        out = x + (0.5 * pat_ref[k0 + 2:k0 + 3]) * y
        if not final:
            refs[0][...] = out
            return
        fg_ref, op_ref, os_ref = refs
        ms = jnp.mean(out * out, axis=-1, keepdims=True)
        out = (out * lax.rsqrt(ms + EPS)) * fg_ref[...]

        @pl.when(prompt)
        def _():
            op_ref[...] = out

        @pl.when(jnp.logical_not(prompt))
        def _():
            os_ref[...] = out

    @pl.when(s < n_chunks)
    def _():
        @pl.when(s == 0)
        def _():
            n_scr[...] = normed()
            acc_scr[...] = jnp.zeros_like(acc_scr)

        wa = wa_ref[...].astype(BF16)
        wb = wb_ref[...].astype(BF16)
        wo = wo_ref[...].astype(BF16)
        wa_res[s] = wa
        wb_res[s] = wb
        wo_res[pl.ds(pl.multiple_of(s * FF_CHUNK, FF_CHUNK), FF_CHUNK), :] = wo
        n = n_scr[...]
        a = _dot(n, wa)
        b = _dot(n, wb)
        acc_scr[...] += _dot((a * jax.nn.sigmoid(a) * b).astype(BF16), wo)

        @pl.when(s == n_chunks - 1)
        def _():
            finish(acc_scr[...])

    @pl.when(s >= n_chunks)
    def _():
        n = normed()
        for j in range(n_chunks):
            a = _dot(n, wa_res[j])
            b = _dot(n, wb_res[j])
            act_scr[:, j * FF_CHUNK:(j + 1) * FF_CHUNK] = (a * jax.nn.sigmoid(a) * b).astype(BF16)
        finish(_dot(act_scr[...], wo_res[...]))


def _ffn(rows, x, pat, norm_g, w_ffn_in, w_ffn_out, layer, sub, k0, final_g=None):
    split_in = isinstance(x, tuple)
    final = final_g is not None
    n_chunks = D_FF // FF_CHUNK
    tiles_p = rows.tiles_p
    tile_of = lambda s: jnp.maximum(s - (n_chunks - 1), 0)
    chunk_of = lambda s: jnp.minimum(s, n_chunks - 1)
    tile = (rows.tile, D_MODEL)
    p_spec = pl.BlockSpec(tile, lambda s: (jnp.minimum(tile_of(s), tiles_p - 1), 0))
    s_spec = pl.BlockSpec(tile, lambda s: (jnp.maximum(tile_of(s) - tiles_p, 0), 0))
    all_spec = pl.BlockSpec(tile, lambda s: (tile_of(s), 0))
    in_specs = ([p_spec, s_spec] if split_in else [all_spec]) + [
        pl.BlockSpec((None, None, N_MOD, D_MODEL), lambda s: (layer, rows.mod_row(tile_of(s)), 0, 0)),
        pl.BlockSpec((None, None, 1, D_MODEL), lambda s: (layer, 2 * sub, 0, 0)),
        pl.BlockSpec((None, None, D_MODEL, FF_CHUNK), lambda s: (layer, sub, 0, chunk_of(s))),
        pl.BlockSpec((None, None, D_MODEL, FF_CHUNK), lambda s: (layer, sub, 0, n_chunks + chunk_of(s))),
        pl.BlockSpec((None, None, FF_CHUNK, D_MODEL), lambda s: (layer, sub, chunk_of(s), 0)),
    ]
    args = (list(x) if split_in else [x]) + [pat, norm_g, w_ffn_in, w_ffn_in, w_ffn_out]
    if final:
        in_specs.append(pl.BlockSpec((1, D_MODEL), lambda s: (0, 0)))
        args.append(final_g)
        out_specs = [p_spec, s_spec]
        out_shape = [jax.ShapeDtypeStruct((rows.tok_p, D_MODEL), F32), jax.ShapeDtypeStruct((rows.tok_s, D_MODEL), F32)]
    else:
        out_specs = all_spec
        out_shape = jax.ShapeDtypeStruct((rows.n_tok, D_MODEL), F32)
    return pl.pallas_call(
        functools.partial(_ffn_kernel, k0=k0, tiles_p=tiles_p, n_chunks=n_chunks, split_in=split_in, final=final),
        grid=(n_chunks - 1 + rows.n_tok // rows.tile,),
        in_specs=in_specs,
        out_specs=out_specs,
        out_shape=out_shape,
        scratch_shapes=[
            pltpu.VMEM((n_chunks, D_MODEL, FF_CHUNK), BF16),
            pltpu.VMEM((n_chunks, D_MODEL, FF_CHUNK), BF16),
            pltpu.VMEM((D_FF, D_MODEL), BF16),
            pltpu.VMEM((rows.tile, D_MODEL), BF16),
            pltpu.VMEM((rows.tile, D_MODEL), F32),
            pltpu.VMEM((rows.tile, D_FF), BF16),
        ],
        compiler_params=_cparams(("arbitrary",)),
        name="ffn",
    )(*args)


def _rope(x, cos, sin):
    lane = lax.broadcasted_iota(jnp.int32, (x.shape[0], LANES), 1)
    first_half = (lane % 32) < 16
    out = []
    for c in range(x.shape[1] // LANES):
        xc = x[:, c * LANES:(c + 1) * LANES]
        partner = jnp.where(first_half, pltpu.roll(xc, LANES - 16, 1), pltpu.roll(xc, 16, 1))
        out.append(xc * cos + partner * sin)
    return jnp.concatenate(out, axis=1)


def _inproj_kernel(*refs, tiles_p, n_alias):
    x_ref, pat_ref, g_ref, w_ref, cos_ref, sin_ref = refs[:6]
    up_ref, us_ref, q_ref, ks_ref, vs_ref, kc_ref, vc_ref, z_ref, w_res, n_scr = refs[6 + n_alias:]
    s = pl.program_id(0)
    latent = jnp.maximum(s - (N_PROJ - 1), 0) >= tiles_p
    scale = DH ** -0.5 * math.log2(math.e)

    def normed():
        return _norm_mod(x_ref[...], g_ref[...], pat_ref[3:4], pat_ref[4:5]).astype(BF16)

    def store_prompt(c, y):
        if c == 0:
            up_ref[...] = y
        elif c == 1:
            q_ref[...] = (y * scale).astype(BF16)
        elif c == 2:
            kc_ref[...] = y.reshape(kc_ref.shape)
        elif c == 3:
            vc_ref[...] = y.reshape(vc_ref.shape)
        else:
            z_ref[...] = y

    @pl.when(s < N_PROJ)
    def _():
        @pl.when(s == 0)
        def _():
            n_scr[...] = normed()

        w = w_ref[...].astype(BF16)
        w_res[s] = w
        y = _dot(n_scr[...], w)
        for c in range(N_PROJ):
            @pl.when(s == c)
            def _(c=c):
                store_prompt(c, y)

    @pl.when(s >= N_PROJ)
    def _():
        n = normed()
        col = lambda c: _dot(n, w_res[c])

        @pl.when(latent)
        def _():
            q_ref[...] = (_rope(col(1), cos_ref[...], sin_ref[...]) * scale).astype(BF16)
            ks_ref[...] = _rope(col(2), cos_ref[...], sin_ref[...]).astype(BF16)
            us_ref[...] = col(0)
            vs_ref[...] = col(3).astype(BF16)
            z_ref[...] = col(4)

        @pl.when(jnp.logical_not(latent))
        def _():
            for c in range(N_PROJ):
                store_prompt(c, col(c))


def _inproj(rows, x, pat, norm_g, w_in, layer, rope_tabs, cache_out):
    tiles_p = rows.tiles_p
    seqs_per_tile = rows.tile // rows.seq_p
    tabs_per_seq = rows.seq_s // rows.tile
    tile_of = lambda s: jnp.maximum(s - (N_PROJ - 1), 0)
    tab_spec = pl.BlockSpec((rows.tile,LANES), lambda s: (jnp.maximum(tile_of(s) - tiles_p, 0) % tabs_per_seq, 0))
    tile = (rows.tile,D_BRANCH)
    p_spec = pl.BlockSpec(tile, lambda s: (jnp.minimum(tile_of(s), tiles_p - 1), 0))
    s_spec = pl.BlockSpec(tile, lambda s: (jnp.maximum(tile_of(s) - tiles_p, 0), 0))
    all_spec = pl.BlockSpec(tile, lambda s: (tile_of(s), 0))
    cache_spec = pl.BlockSpec((seqs_per_tile, None, rows.seq_p, D_BRANCH),
                              lambda s: (jnp.minimum(tile_of(s), tiles_p - 1), layer, 0, 0))
    cache_shape = jax.ShapeDtypeStruct((rows.n_p, DEPTH, rows.seq_p, D_BRANCH), F32)
    in_specs = [
        pl.BlockSpec((rows.tile,D_MODEL), lambda s: (tile_of(s), 0)),
        pl.BlockSpec((None, None, N_MOD, D_MODEL), lambda s: (layer, rows.mod_row(tile_of(s)), 0, 0)),
        pl.BlockSpec((None, None, 1, D_MODEL), lambda s: (layer, 1, 0, 0)),
        pl.BlockSpec((None, D_MODEL, D_BRANCH), lambda s: (layer, 0, jnp.minimum(s, N_PROJ - 1))),
        tab_spec, tab_spec,
    ]
    args = [x, pat, norm_g, w_in, *rope_tabs]
    aliases = {}
    if cache_out is not None:
        in_specs += [pl.BlockSpec(memory_space=pl.ANY)] * 2
        aliases = {len(args): 5, len(args) + 1: 6}
        args += list(cache_out)
    return pl.pallas_call(
        functools.partial(_inproj_kernel, tiles_p=tiles_p, n_alias=len(aliases)),
        grid=(N_PROJ - 1 + rows.n_tok // rows.tile,),
        in_specs=in_specs,
        out_specs=[p_spec, s_spec, all_spec, s_spec, s_spec, cache_spec, cache_spec, all_spec],
        out_shape=[
            jax.ShapeDtypeStruct((rows.tok_p, D_BRANCH), F32),
            jax.ShapeDtypeStruct((rows.tok_s, D_BRANCH), F32),
            jax.ShapeDtypeStruct((rows.n_tok, D_BRANCH), BF16),
            jax.ShapeDtypeStruct((rows.tok_s, D_BRANCH), BF16),
            jax.ShapeDtypeStruct((rows.tok_s, D_BRANCH), BF16),
            cache_shape, cache_shape,
            jax.ShapeDtypeStruct((rows.n_tok, D_BRANCH), F32),
        ],
        scratch_shapes=[pltpu.VMEM((N_PROJ, D_MODEL, D_BRANCH), BF16), pltpu.VMEM((rows.tile,D_MODEL), BF16)],
        input_output_aliases=aliases,
        compiler_params=_cparams(("arbitrary",)),
        name="inproj",
    )(*args)


def _s5_kernel(*refs, n_seq, n_steps, backward):
    if backward:
        u_ref, part_ref, w1_ref, w2_ref, w3_ref, ar_ref, ai_ref, h0_ref, wglu_ref, ya_ref, hfin_ref, xs_scr, h_scr = refs
    else:
        u_ref, dsk_ref, w1_ref, w2_ref, w3_ref, ar_ref, ai_ref, h0_ref, part_ref, hfin_ref, xs_scr, h_scr = refs
    rows = n_seq * n_steps
    pairs = n_steps // 2
    half = pairs * n_seq

    @pl.when(pl.program_id(0) == 0)
    def _():
        h_scr[...] = h0_ref[...]

    u = jnp.swapaxes(u_ref[...], 0, 1).reshape(rows, D_BRANCH)
    u4 = u.reshape(pairs, 2, n_seq, D_BRANCH)
    u_even = u4[:, 0].reshape(half, D_BRANCH).astype(BF16)
    u_odd = u4[:, 1].reshape(half, D_BRANCH).astype(BF16)
    u_first = u_odd if backward else u_even
    base = 0 if backward else n_seq
    incoming = half if backward else 0
    order = range(pairs - 1, -1, -1) if backward else range(pairs)

    def project(j):
        cols = slice(j * LANES, (j + 1) * LANES)
        lhs = jnp.concatenate([u_even[:, cols], u_odd[:, cols]], axis=1)
        xs_scr[j, base:base + half, :] = _dot(lhs, w1_ref[j])

    def scan(j):
        lanes = slice(j * SLAB_STATE, (j + 1) * SLAB_STATE)
        ar = jnp.broadcast_to(ar_ref[:, lanes], (n_seq, SLAB_STATE))
        ai = jnp.broadcast_to(ai_ref[:, lanes], (n_seq, SLAB_STATE))
        hr = h_scr[0, :, lanes]
        hi = h_scr[1, :, lanes]
        xs_scr[j, incoming:incoming + n_seq, 0:SLAB_STATE] = hr
        xs_scr[j, incoming:incoming + n_seq, SLAB_STATE:2 * SLAB_STATE] = hi
        for q in order:
            r = slice(base + q * n_seq, base + (q + 1) * n_seq)
            xr = xs_scr[j, r, 0:SLAB_STATE]
            xi = xs_scr[j, r, SLAB_STATE:2 * SLAB_STATE]
            hr, hi = ar * hr - ai * hi + xr, ar * hi + ai * hr + xi
            xs_scr[j, r, 0:SLAB_STATE] = hr
            xs_scr[j, r, SLAB_STATE:2 * SLAB_STATE] = hi
        h_scr[0, :, lanes] = hr
        h_scr[1, :, lanes] = hi

    def read_out(j):
        z = _dot(xs_scr[j].astype(BF16), w2_ref[j])
        direct = _dot(u_first[:, j * LANES:(j + 1) * LANES], w3_ref[j])
        lo = z[0:half]
        hi = z[n_seq:n_seq + half]
        if backward:
            even, odd = lo[:, 0:LANES], hi[:, LANES:2 * LANES] + direct
        else:
            even, odd = lo[:, LANES:2 * LANES] + direct, hi[:, 0:LANES]
        both = jnp.stack([even.reshape(pairs, n_seq, LANES), odd.reshape(pairs, n_seq, LANES)], axis=1)
        return both.reshape(rows, LANES)

    ys = []
    project(0)
    for j in range(N_SLAB):
        if j + 1 < N_SLAB:
            project(j + 1)
        scan(j)
        ys.append(read_out(j))
    y = jnp.concatenate(ys, axis=1)

    if backward:
        g = jax.nn.gelu(part_ref[...] + y)
        ya = g * jax.nn.sigmoid(_dot(g.astype(BF16), wglu_ref[...]))
        ya_ref[...] = jnp.swapaxes(ya.reshape(n_steps, n_seq, D_BRANCH), 0, 1).astype(BF16)
    else:
        part_ref[...] = y + dsk_ref[...] * u

    @pl.when(pl.program_id(0) == pl.num_programs(0) - 1)
    def _():
        hfin_ref[...] = h_scr[...]


def _s5(u, ssm, h0, h0_row, n_seq, layer):
    n_tok = u.shape[0]
    seq = n_tok // n_seq
    n_steps = S5_CHUNK_ROWS // n_seq
    n_chunks = seq // n_steps
    u3 = u.reshape(n_seq, seq, D_BRANCH)

    def specs(backward):
        d = int(backward)
        chunk = (lambda s: n_chunks - 1 - s) if backward else (lambda s: s)
        seq_block = pl.BlockSpec((n_seq, n_steps, D_BRANCH), lambda s: (0, chunk(s), 0))
        part_block = pl.BlockSpec((S5_CHUNK_ROWS, D_BRANCH), lambda s: (chunk(s), 0))
        params = [
            pl.BlockSpec((None, None, N_SLAB, 2 * LANES, 2 * SLAB_STATE), lambda s: (layer, d, 0, 0, 0)),
            pl.BlockSpec((None, None, N_SLAB, 2 * SLAB_STATE, 2 * LANES), lambda s: (layer, d, 0, 0, 0)),
            pl.BlockSpec((None, None, N_SLAB, LANES, LANES), lambda s: (layer, d, 0, 0, 0)),
            pl.BlockSpec((None, None, 1, N_STATE), lambda s: (layer, d, 0, 0)),
            pl.BlockSpec((None, None, 1, N_STATE), lambda s: (layer, d, 0, 0)),
            pl.BlockSpec((None, None, 2, n_seq, N_STATE), lambda s: (h0_row, d, 0, 0, 0)),
        ]
        return seq_block, part_block, params

    param_args = (ssm["w1"], ssm["w2"], ssm["w3"], ssm["a2_re"], ssm["a2_im"], h0)
    fin_spec = pl.BlockSpec((2, n_seq, N_STATE), lambda s: (0, 0, 0))
    fin_shape = jax.ShapeDtypeStruct((2, n_seq, N_STATE), F32)
    state_rows = S5_CHUNK_ROWS // 2 + n_seq
    scratch = [pltpu.VMEM((N_SLAB, state_rows, 2 * SLAB_STATE), F32), pltpu.VMEM((2, n_seq, N_STATE), F32)]

    seq_block, part_block, params = specs(False)
    part, fin_f = pl.pallas_call(
        functools.partial(_s5_kernel, n_seq=n_seq, n_steps=n_steps, backward=False),
        grid=(n_chunks,),
        in_specs=[seq_block, pl.BlockSpec((None, 1, D_BRANCH), lambda s: (layer, 0, 0))] + params,
        out_specs=[part_block, fin_spec],
        out_shape=[jax.ShapeDtypeStruct((n_tok, D_BRANCH), F32), fin_shape],
        scratch_shapes=scratch,
        compiler_params=_cparams(("arbitrary",)),
        name="s5_fwd",
    )(u3, ssm["d_skip"], *param_args)

    seq_block, part_block, params = specs(True)
    ya, fin_b = pl.pallas_call(
        functools.partial(_s5_kernel, n_seq=n_seq, n_steps=n_steps, backward=True),
        grid=(n_chunks,),
        in_specs=[seq_block, part_block] + params + [pl.BlockSpec((None, D_BRANCH, D_BRANCH), lambda s: (layer, 0, 0))],
        out_specs=[seq_block, fin_spec],
        out_shape=[jax.ShapeDtypeStruct((n_seq, seq, D_BRANCH), BF16), fin_shape],
        scratch_shapes=scratch,
        compiler_params=_cparams(("arbitrary",)),
        name="s5_bwd",
    )(u3, part, *param_args, ssm["w_glu"])
    return ya.reshape(n_tok, D_BRANCH), jnp.stack([fin_f, fin_b], axis=0)


def _s5_params(lam_re, lam_im, log_dt, b_re, b_im, c_re, c_im):
    dt = jnp.exp(log_dt)[..., None]
    mag = jnp.exp(lam_re * dt)
    abr = mag * jnp.cos(lam_im * dt)
    abi = mag * jnp.sin(lam_im * dt)
    den = lam_re * lam_re + lam_im * lam_im
    nr = abr - 1.0
    kr = (nr * lam_re + abi * lam_im) / den
    ki = (abi * lam_re - nr * lam_im) / den
    bbr = kr[..., None] * b_re - ki[..., None] * b_im
    bbi = kr[..., None] * b_im + ki[..., None] * b_re
    per_slab = S5_GROUPS // N_SLAB

    def block_diag(w, n_in, n_out):
        t = w.reshape(2, N_SLAB, per_slab, n_out, n_in).transpose(0, 1, 2, 4, 3)
        t = jnp.tile(t.reshape(2, N_SLAB, per_slab * n_in, n_out), (1, 1, 1, per_slab))
        row_group = lax.broadcasted_iota(jnp.int32, t.shape[-2:], 0) // n_in
        col_group = lax.broadcasted_iota(jnp.int32, t.shape[-2:], 1) // n_out
        return jnp.where(row_group == col_group, t, 0.0)

    def pack_in(w):
        return block_diag(w, S5_GROUP, S5_STATE)

    def pack_out(w):
        return block_diag(w, S5_STATE, S5_GROUP)

    abbr = abr[..., None] * bbr - abi[..., None] * bbi
    abbi = abr[..., None] * bbi + abi[..., None] * bbr
    car = c_re * abr[:, :, None, :] - c_im * abi[:, :, None, :]
    cai = c_re * abi[:, :, None, :] + c_im * abr[:, :, None, :]
    b_pack = jnp.concatenate([pack_in(bbr), pack_in(bbi)], axis=-1)
    ab_pack = jnp.concatenate([pack_in(abbr), pack_in(abbi)], axis=-1)
    w1 = jnp.stack([jnp.concatenate([ab_pack[0], b_pack[0]], axis=-2),
                    jnp.concatenate([b_pack[1], ab_pack[1]], axis=-2)]).astype(BF16)
    c_pack = jnp.concatenate([pack_out(c_re), -pack_out(c_im)], axis=-2)
    ca_pack = jnp.concatenate([pack_out(car), -pack_out(cai)], axis=-2)
    w2 = jnp.concatenate([c_pack, ca_pack], axis=-1).astype(BF16)
    direct = jnp.einsum("dgop,dgpi->dgoi", c_re, bbr) - jnp.einsum("dgop,dgpi->dgoi", c_im, bbi)
    w3 = block_diag(direct, S5_GROUP, S5_GROUP).astype(BF16)
    a2r = abr * abr - abi * abi
    a2i = 2.0 * abr * abi
    return dict(w1=w1, w2=w2, w3=w3, a2_re=a2r.reshape(2, 1, N_STATE), a2_im=a2i.reshape(2, 1, N_STATE))


def _attn_kernel(*refs, n_ctx, lam_init, layer):
    if n_ctx:
        lam_ref, q_ref, k_ref, v_ref, ck_ref, cv_ref, g_ref, o_ref, k_scr, v_scr = refs
    else:
        lam_ref, q_ref, k_ref, v_ref, g_ref, o_ref, k_scr, v_scr = refs
    n_own = k_ref.shape[0]

    @pl.when(pl.program_id(1) == 0)
    def _():
        k_scr[0:n_own, :] = k_ref[...].astype(BF16)
        v_scr[0:n_own, :] = v_ref[...].astype(BF16)
        if n_ctx:
            k_scr[n_own:n_own + n_ctx, :] = ck_ref[...].astype(BF16)
            v_scr[n_own:n_own + n_ctx, :] = cv_ref[...].astype(BF16)

    lam = lam_ref[layer]
    q = q_ref[...]
    lane = lax.broadcasted_iota(jnp.int32, (q.shape[0], LANES), 1)

    def scores(h):
        lanes = slice(h * LANES, (h + 1) * LANES)
        qh = q[:, lanes]
        kh = k_scr[:, lanes]
        out = []
        for m in range(2):
            qm = jnp.where((lane < DH) if m == 0 else (lane >= DH), qh, jnp.zeros_like(qh))
            out.append(lax.dot_general(qm, kh, (((1,), (1,)), ((), ())), preferred_element_type=F32))
        return out

    sc = scores(0)
    for h in range(N_HEADS):
        lanes = slice(h * LANES, (h + 1) * LANES)
        nxt = scores(h + 1) if h + 1 < N_HEADS else None
        e1, e2 = [jnp.exp2(s_m - jnp.max(s_m, axis=-1, keepdims=True)) for s_m in sc]
        l1 = jnp.sum(e1, axis=-1, keepdims=True)
        l2 = jnp.sum(e2, axis=-1, keepdims=True)
        a = (e1 - e2 * (lam * l1 / l2)).astype(BF16)
        o = _dot(a, v_scr[:, lanes]) * (1.0 / l1)
        ms = jnp.mean(o * o, axis=-1, keepdims=True)
        o_ref[:, lanes] = (((o * lax.rsqrt(ms + EPS)) * g_ref[...]) * (1.0 - lam_init)).astype(BF16)
        sc = nxt


def _attention(q, k, v, lam, attn_g, row0, n_seq, seq, lam_init, layer, cache=None):
    tq = min(Q_TILE, seq)
    q_tiles = seq // tq
    q0 = row0 // tq
    n_ctx = 0 if cache is None else cache[0].shape[2]
    if k.ndim == 4:
        own = pl.BlockSpec((None, None, seq, D_BRANCH), lambda b, i: (b, layer, 0, 0))
    else:
        own = pl.BlockSpec((seq, D_BRANCH), lambda b, i: (b, 0))
    in_specs = [
        pl.BlockSpec(memory_space=pltpu.SMEM),
        pl.BlockSpec((tq, D_BRANCH), lambda b, i: (q0 + b * q_tiles + i, 0)),
        own, own,
    ]
    args = [lam, q, k, v]
    if n_ctx:
        in_specs += [pl.BlockSpec((None, None, n_ctx, D_BRANCH), lambda b, i: (b, layer, 0, 0))] * 2
        args += list(cache)
    in_specs.append(pl.BlockSpec((None, 1, DV), lambda b, i: (layer, 0, 0)))
    args.append(attn_g)
    return pl.pallas_call(
        functools.partial(_attn_kernel, n_ctx=n_ctx, lam_init=lam_init, layer=layer),
        grid=(n_seq, q_tiles),
        in_specs=in_specs,
        out_specs=pl.BlockSpec((tq, D_BRANCH), lambda b, i: (b * q_tiles + i, 0)),
        out_shape=jax.ShapeDtypeStruct((n_seq * seq, D_BRANCH), BF16),
        scratch_shapes=[pltpu.VMEM((seq + n_ctx, D_BRANCH), BF16)] * 2,
        compiler_params=_cparams(("parallel", "arbitrary")),
        name="attn",
    )(*args)


def _pool_kernel(z_ref, w_ref, sc_ref, o_ref, pad_scr):
    seq = z_ref.shape[0]
    padded = seq + 2 * POOL_PAD
    z = z_ref[...]
    zeros = jnp.zeros((POOL_PAD, D_BRANCH), F32)
    pad_scr[0:POOL_PAD, :] = zeros
    pad_scr[POOL_PAD:POOL_PAD + seq, :] = z
    pad_scr[POOL_PAD + seq:padded, :] = zeros
    t = lax.broadcasted_iota(jnp.int32, (seq, POOL_GROUP), 0)
    outs = []
    for gi, w in enumerate(POOL_WINDOWS):
        lanes = slice(gi * POOL_GROUP, (gi + 1) * POOL_GROUP)
        run = pad_scr[:, lanes]
        span = 1
        while span < w:
            run = run + pltpu.roll(run, padded - span, 0)
            span *= 2
        tot = pltpu.roll(run, w // 2, 0)[POOL_PAD:POOL_PAD + seq]
        cnt = jnp.minimum(t + w // 2, seq) - jnp.maximum(t - w // 2, 0)
        pooled = tot / cnt.astype(F32) - z[:, lanes]
        outs.append(_dot(pooled.astype(BF16), w_ref[gi]))
    o_ref[...] = (jnp.concatenate(outs, axis=1) * sc_ref[...]).astype(BF16)


def _pool(z, w_pool, pool_scale, row0, n_seq, seq, layer):
    s0 = row0 // seq
    return pl.pallas_call(
        _pool_kernel,
        grid=(n_seq,),
        in_specs=[
            pl.BlockSpec((seq, D_BRANCH), lambda b: (s0 + b, 0)),
            pl.BlockSpec((None, len(POOL_WINDOWS), POOL_GROUP, POOL_GROUP), lambda b: (layer, 0, 0, 0)),
            pl.BlockSpec((None, 1, D_BRANCH), lambda b: (layer, 0, 0)),
        ],
        out_specs=pl.BlockSpec((seq, D_BRANCH), lambda b: (b, 0)),
        out_shape=jax.ShapeDtypeStruct((n_seq * seq, D_BRANCH), BF16),
        scratch_shapes=[pltpu.VMEM((seq + 2 * POOL_PAD, D_BRANCH), F32)],
        compiler_params=_cparams(("parallel",)),
        name="pool",
    )(z, w_pool, pool_scale)


def _merge_kernel(x_ref, pat_ref, g_ref, *refs, tiles_p):
    branch_refs = refs[:2 * N_BRANCH]
    wg0_ref, wg1_ref, wbr_ref, wo_ref, o_ref, wg_res, wbr_res, wo_res, n_scr, m_scr = refs[2 * N_BRANCH:]
    s = pl.program_id(0)
    prompt = jnp.maximum(s - (N_BRANCH - 1), 0) < tiles_p
    x = x_ref[...]

    def normed():
        return _norm_mod(x, g_ref[...], pat_ref[3:4], pat_ref[4:5]).astype(BF16)

    def branch_in(br):
        return jnp.where(prompt, branch_refs[2 * br][...], branch_refs[2 * br + 1][...])

    def finish(merged):
        o_ref[...] = x + pat_ref[5:6] * _dot(merged.astype(BF16), wo_res[...])

    @pl.when(s < N_BRANCH)
    def _():
        @pl.when(s == 0)
        def _():
            n_scr[...] = normed()
            m_scr[...] = jnp.zeros_like(m_scr)
            wo_res[...] = wo_ref[...].astype(BF16)

        wg = jnp.concatenate([wg0_ref[...], wg1_ref[...]], axis=1).astype(BF16)
        wbr = wbr_ref[...].astype(BF16)
        wg_res[s] = wg
        wbr_res[s] = wbr
        gate = jax.nn.sigmoid(_dot(n_scr[...], wg))
        for br in range(N_BRANCH):
            @pl.when(s == br)
            def _(br=br):
                m_scr[...] += gate * _dot(branch_in(br), wbr)

        @pl.when(s == N_BRANCH - 1)
        def _():
            finish(m_scr[...])

    @pl.when(s >= N_BRANCH)
    def _():
        n = normed()
        merged = None
        for br in range(N_BRANCH):
            part = jax.nn.sigmoid(_dot(n, wg_res[br])) * _dot(branch_in(br), wbr_res[br])
            merged = part if merged is None else merged + part
        finish(merged)


def _merge(rows, x, pat, norm_g, branches, w_in, w_branch, w_out, layer):
    tiles_p = rows.tiles_p
    half = D_MODEL // 2
    gate0 = N_PROJ * D_BRANCH // half
    tile_of = lambda s: jnp.maximum(s - (N_BRANCH - 1), 0)
    br_of = lambda s: jnp.minimum(s, N_BRANCH - 1)
    p_spec = pl.BlockSpec((rows.tile,D_BRANCH), lambda s: (jnp.minimum(tile_of(s), tiles_p - 1), 0))
    s_spec = pl.BlockSpec((rows.tile,D_BRANCH), lambda s: (jnp.maximum(tile_of(s) - tiles_p, 0), 0))
    return pl.pallas_call(
        functools.partial(_merge_kernel, tiles_p=tiles_p),
        grid=(N_BRANCH - 1 + rows.n_tok // rows.tile,),
        in_specs=[
            pl.BlockSpec((rows.tile,D_MODEL), lambda s: (tile_of(s), 0)),
            pl.BlockSpec((None, None, N_MOD, D_MODEL), lambda s: (layer, rows.mod_row(tile_of(s)), 0, 0)),
            pl.BlockSpec((None, None, 1, D_MODEL), lambda s: (layer, 1, 0, 0)),
            p_spec, s_spec, p_spec, s_spec, p_spec, s_spec,
            pl.BlockSpec((None, D_MODEL, half), lambda s: (layer, 0, gate0 + 2 * br_of(s))),
            pl.BlockSpec((None, D_MODEL, half), lambda s: (layer, 0, gate0 + 2 * br_of(s) + 1)),
            pl.BlockSpec((None, None, D_BRANCH, D_MODEL), lambda s: (layer, br_of(s), 0, 0)),
            _resident((None, D_MODEL, D_MODEL), lambda s: (layer, 0, 0)),
        ],
        out_specs=pl.BlockSpec((rows.tile,D_MODEL), lambda s: (tile_of(s), 0)),
        out_shape=jax.ShapeDtypeStruct((rows.n_tok, D_MODEL), F32),
        scratch_shapes=[
            pltpu.VMEM((N_BRANCH, D_MODEL, D_MODEL), BF16),
            pltpu.VMEM((N_BRANCH, D_BRANCH, D_MODEL), BF16),
            pltpu.VMEM((D_MODEL, D_MODEL), BF16),
            pltpu.VMEM((rows.tile,D_MODEL), BF16),
            pltpu.VMEM((rows.tile,D_MODEL), F32),
        ],
        compiler_params=_cparams(("arbitrary",)),
        name="merge",
    )(x, pat, norm_g, *[y for pair in branches for y in pair], w_in, w_in, w_branch, w_out)


def _rope_tables(seq):
    n_rows = seq // GRID_W
    row = jnp.repeat(jnp.arange(n_rows, dtype=F32), GRID_W)
    col = jnp.tile(jnp.arange(GRID_W, dtype=F32), n_rows)
    n_freq = DH // 4
    inv = ROPE_BASE ** (-jnp.arange(n_freq, dtype=F32) / n_freq)
    ar = row[:, None] * inv
    ac = col[:, None] * inv
    cos = jnp.concatenate([jnp.cos(ar)] * 2 + [jnp.cos(ac)] * 2, axis=1)
    sin = jnp.concatenate([-jnp.sin(ar), jnp.sin(ar), -jnp.sin(ac), jnp.sin(ac)], axis=1)
    return jnp.tile(cos, (1, LANES // DH)), jnp.tile(sin, (1, LANES // DH))


def kernel(x_prompt, x_sample, cache_k, cache_v, state_ssm, c, c_ctx, norm_g, w_mod, b_mod, w_ffn_in, w_ffn_out, w_in, ssm_lam_re, ssm_lam_im, ssm_log_dt, ssm_b_re, ssm_b_im, ssm_c_re, ssm_c_im, ssm_d, w_glu, lam_q1, lam_k1, lam_q2, lam_k2, attn_norm_g, w_pool, pool_scale, w_branch, w_out, final_norm_g):
    n_p, seq_p, _ = x_prompt.shape
    n_s, seq_s, _ = x_sample.shape
    n_past = cache_k.shape[2]
    rows = _Rows(n_p, seq_p, n_s, seq_s, ROW_TILE)
    merge_rows = _Rows(n_p, seq_p, n_s, seq_s, MERGE_ROW_TILE)
    tok_p = rows.tok_p

    mod_rows = -(-(1 + n_s) // SUBLANES) * SUBLANES
    cvec = jnp.concatenate([c_ctx[None, :], c, jnp.zeros((mod_rows - 1 - n_s, D_MODEL), F32)], axis=0)
    pat = _adaln(cvec, w_mod, b_mod).reshape(DEPTH, mod_rows, N_MOD, D_MODEL)

    w_pl = w_pool.astype(BF16)

    x = (x_prompt.reshape(tok_p, D_MODEL), x_sample.reshape(rows.tok_s, D_MODEL))
    rope_tabs = _rope_tables(seq_s)
    cache = (cache_k.reshape(n_s, DEPTH, n_past, D_BRANCH), cache_v.reshape(n_s, DEPTH, n_past, D_BRANCH))
    h0_p = jnp.zeros((1, 2, 2, n_p, N_STATE), F32)
    h0_s = state_ssm.reshape(n_s, DEPTH, 2, 2, N_STATE).transpose(1, 2, 3, 0, 4)

    ssm = jax.vmap(_s5_params)(ssm_lam_re, ssm_lam_im, ssm_log_dt, ssm_b_re, ssm_b_im, ssm_c_re, ssm_c_im)
    ssm.update(d_skip=ssm_d[:, None, :], w_glu=w_glu.astype(BF16))

    lam_inits = [0.8 - 0.6 * math.exp(-0.3 * l) for l in range(DEPTH)]
    lam = (jnp.exp(jnp.sum(lam_q1 * lam_k1, axis=-1)) - jnp.exp(jnp.sum(lam_q2 * lam_k2, axis=-1))
           + jnp.asarray(lam_inits, F32))
    ng = norm_g.reshape(DEPTH, 3, 1, D_MODEL)
    attn_g = attn_norm_g.reshape(DEPTH, 1, DV)
    scale_c = pool_scale.reshape(DEPTH, 1, D_BRANCH)

    new_cache, new_s = None, []
    for l in range(DEPTH):
        x = _ffn(merge_rows if l == 0 else rows, x, pat, ng, w_ffn_in, w_ffn_out, l, 0, 0)
        u_p, u_s, q, k_s, v_s, k_new, v_new, z = _inproj(rows, x, pat, ng, w_in, l, rope_tabs, new_cache)
        new_cache = (k_new, v_new)

        ya_p, h_fin = _s5(u_p, ssm, h0_p, 0, n_p, l)
        ya_s, _ = _s5(u_s, ssm, h0_s, l, n_s, l)
        yb = (_attention(q, k_new, v_new, lam, attn_g, 0, n_p, seq_p, lam_inits[l], l),
              _attention(q, k_s, v_s, lam, attn_g, tok_p, n_s, seq_s, lam_inits[l], l, cache))
        yc = (_pool(z, w_pl, scale_c, 0, n_p, seq_p, l), _pool(z, w_pl, scale_c, tok_p, n_s, seq_s, l))

        x = _merge(merge_rows, x, pat, ng, ((ya_p, ya_s), yb, yc), w_in, w_branch, w_out, l)
        last = l == DEPTH - 1
        x = _ffn(merge_rows if last else rows, x, pat, ng, w_ffn_in, w_ffn_out, l, 1, 6,
                 final_norm_g[None, :] if last else None)
        new_s.append(h_fin)

    y_p, y_s = x
    new_state = jnp.stack(new_s, axis=0).transpose(3, 0, 1, 2, 4)
    return (y_p.reshape(n_p, seq_p, D_MODEL), y_s.reshape(n_s, seq_s, D_MODEL),
            new_cache[0].reshape(n_p, DEPTH, seq_p, N_HEADS, 2, DH), new_cache[1].reshape(n_p, DEPTH, seq_p, N_HEADS, DV),
            new_state.reshape(n_p, DEPTH, 2, 2, S5_GROUPS, S5_STATE))
```

```python
import functools
import math

import jax
import jax.numpy as jnp
from jax import lax
from jax.experimental import pallas as pl
from jax.experimental.pallas import tpu as pltpu

F32 = jnp.float32
BF16 = jnp.bfloat16

D_MODEL = 1024
DEPTH = 4
GRID_W = 64
D_BRANCH = 512
S5_GROUP = 16
S5_GROUPS = 32
S5_STATE = 64
N_STATE = S5_GROUPS * S5_STATE
DH = 64
N_HEADS = 4
DV = 128
POOL_WINDOWS = (2, 4, 8, 16)
POOL_GROUP = 128
POOL_PAD = 16
D_FF = 2816
N_MOD = 9
N_BRANCH = 3
N_PROJ = 5
ROPE_BASE = 10000.0
EPS = 1e-6

LANES = 128
SUBLANES = 8
N_SLAB = D_BRANCH // LANES
SLAB_STATE = N_STATE // N_SLAB
ROW_TILE = 1024
MERGE_ROW_TILE = 512
FF_CHUNK = 256
S5_CHUNK_ROWS = 1024
Q_TILE = 512
VMEM_LIMIT = 56 * 1024 * 1024


def _cparams(sem):
    return pltpu.CompilerParams(dimension_semantics=sem, vmem_limit_bytes=VMEM_LIMIT)


def _resident(block_shape, index_map):
    return pl.BlockSpec(block_shape, index_map, pipeline_mode=pl.Buffered(1))


def _norm_mod(x, g, shift, scale):
    ms = jnp.mean(x * x, axis=-1, keepdims=True)
    y = (x * lax.rsqrt(ms + EPS)) * g
    return y * (1.0 + scale) + shift


def _dot(a, b):
    return jnp.dot(a, b, preferred_element_type=F32)


class _Rows:
    def __init__(self, n_p, seq_p, n_s, seq_s, tile):
        self.n_p, self.seq_p, self.n_s, self.seq_s, self.tile = n_p, seq_p, n_s, seq_s, tile
        self.tok_p = n_p * seq_p
        self.tok_s = n_s * seq_s
        self.n_tok = self.tok_p + self.tok_s
        assert self.tok_p % tile == 0 and seq_s % tile == 0 and tile % seq_p == 0
        self.tiles_p = self.tok_p // tile

    def mod_row(self, i):
        return jnp.where(i < self.tiles_p, 0, 1 + (i - self.tiles_p) // (self.seq_s // self.tile))


def _adaln_kernel(c_ref, w_ref, b_ref, o_ref):
    c = c_ref[...]
    s = (c * jax.nn.sigmoid(c)).astype(BF16)
    o_ref[...] = _dot(s, w_ref[...].astype(BF16)) + b_ref[...]


def _adaln(cvec, w_mod, b_mod):
    tn = 1024
    n_rows = cvec.shape[0]
    return pl.pallas_call(
        _adaln_kernel,
        grid=(DEPTH, N_MOD * D_MODEL // tn),
        in_specs=[
            pl.BlockSpec((n_rows, D_MODEL), lambda l, j: (0, 0)),
            pl.BlockSpec((None, D_MODEL, tn), lambda l, j: (l, 0, j)),
            pl.BlockSpec((None, 1, tn), lambda l, j: (l, 0, j)),
        ],
        out_specs=pl.BlockSpec((None, n_rows, tn), lambda l, j: (l, 0, j)),
        out_shape=jax.ShapeDtypeStruct((DEPTH, n_rows, N_MOD * D_MODEL), F32),
        compiler_params=_cparams(("parallel", "parallel")),
        name="adaln",
    )(cvec, w_mod, b_mod.reshape(DEPTH, 1, N_MOD * D_MODEL))


def _ffn_kernel(*refs, k0, tiles_p, n_chunks, split_in, final):
    refs = list(refs)
    s = pl.program_id(0)
    tile = jnp.maximum(s - (n_chunks - 1), 0)
    prompt = tile < tiles_p
    if split_in:
        xp_ref, xs_ref = refs[:2]
        refs = refs[2:]
        x = jnp.where(prompt, xp_ref[...], xs_ref[...])
    else:
        x = refs.pop(0)[...]
    pat_ref, g_ref, wa_ref, wb_ref, wo_ref = refs[:5]
    refs = refs[5:]
    wa_res, wb_res, wo_res, n_scr, acc_scr, act_scr = refs[-6:]
    refs = refs[:-6]

    def normed():
        return _norm_mod(x, g_ref[...], pat_ref[k0:k0 + 1], pat_ref[k0 + 1:k0 + 2]).astype(BF16)

    def finish(y):
        out = x + (0.5 * pat_ref[k0 + 2:k0 + 3]) * y
        if not final:
            refs[0][...] = out
            return
        fg_ref, op_ref, os_ref = refs
        ms = jnp.mean(out * out, axis=-1, keepdims=True)
        out = (out * lax.rsqrt(ms + EPS)) * fg_ref[...]

        @pl.when(prompt)
        def _():
            op_ref[...] = out

        @pl.when(jnp.logical_not(prompt))
        def _():
            os_ref[...] = out

    @pl.when(s < n_chunks)
    def _():
        @pl.when(s == 0)
        def _():
            n_scr[...] = normed()
            acc_scr[...] = jnp.zeros_like(acc_scr)

        wa = wa_ref[...].astype(BF16)
        wb = wb_ref[...].astype(BF16)
        wo = wo_ref[...].astype(BF16)
        wa_res[s] = wa
        wb_res[s] = wb
        wo_res[pl.ds(pl.multiple_of(s * FF_CHUNK, FF_CHUNK), FF_CHUNK), :] = wo
        n = n_scr[...]
        a = _dot(n, wa)
        b = _dot(n, wb)
        acc_scr[...] += _dot((a * jax.nn.sigmoid(a) * b).astype(BF16), wo)

        @pl.when(s == n_chunks - 1)
        def _():
            finish(acc_scr[...])

    @pl.when(s >= n_chunks)
    def _():
        n = normed()
        for j in range(n_chunks):
            a = _dot(n, wa_res[j])
            b = _dot(n, wb_res[j])
            act_scr[:, j * FF_CHUNK:(j + 1) * FF_CHUNK] = (a * jax.nn.sigmoid(a) * b).astype(BF16)
        finish(_dot(act_scr[...], wo_res[...]))


def _ffn(rows, x, pat, norm_g, w_ffn_in, w_ffn_out, layer, sub, k0, final_g=None):
    split_in = isinstance(x, tuple)
    final = final_g is not None
    n_chunks = D_FF // FF_CHUNK
    tiles_p = rows.tiles_p
    tile_of = lambda s: jnp.maximum(s - (n_chunks - 1), 0)
    chunk_of = lambda s: jnp.minimum(s, n_chunks - 1)
    tile = (rows.tile, D_MODEL)
    p_spec = pl.BlockSpec(tile, lambda s: (jnp.minimum(tile_of(s), tiles_p - 1), 0))
    s_spec = pl.BlockSpec(tile, lambda s: (jnp.maximum(tile_of(s) - tiles_p, 0), 0))
    all_spec = pl.BlockSpec(tile, lambda s: (tile_of(s), 0))
    in_specs = ([p_spec, s_spec] if split_in else [all_spec]) + [
        pl.BlockSpec((None, None, N_MOD, D_MODEL), lambda s: (layer, rows.mod_row(tile_of(s)), 0, 0)),
        pl.BlockSpec((None, None, 1, D_MODEL), lambda s: (layer, 2 * sub, 0, 0)),
        pl.BlockSpec((None, None, D_MODEL, FF_CHUNK), lambda s: (layer, sub, 0, chunk_of(s))),
        pl.BlockSpec((None, None, D_MODEL, FF_CHUNK), lambda s: (layer, sub, 0, n_chunks + chunk_of(s))),
        pl.BlockSpec((None, None, FF_CHUNK, D_MODEL), lambda s: (layer, sub, chunk_of(s), 0)),
    ]
    args = (list(x) if split_in else [x]) + [pat, norm_g, w_ffn_in, w_ffn_in, w_ffn_out]
    if final:
        in_specs.append(pl.BlockSpec((1, D_MODEL), lambda s: (0, 0)))
        args.append(final_g)
        out_specs = [p_spec, s_spec]
        out_shape = [jax.ShapeDtypeStruct((rows.tok_p, D_MODEL), F32), jax.ShapeDtypeStruct((rows.tok_s, D_MODEL), F32)]
    else:
        out_specs = all_spec
        out_shape = jax.ShapeDtypeStruct((rows.n_tok, D_MODEL), F32)
    return pl.pallas_call(
        functools.partial(_ffn_kernel, k0=k0, tiles_p=tiles_p, n_chunks=n_chunks, split_in=split_in, final=final),
        grid=(n_chunks - 1 + rows.n_tok // rows.tile,),
        in_specs=in_specs,
        out_specs=out_specs,
        out_shape=out_shape,
        scratch_shapes=[
            pltpu.VMEM((n_chunks, D_MODEL, FF_CHUNK), BF16),
            pltpu.VMEM((n_chunks, D_MODEL, FF_CHUNK), BF16),
            pltpu.VMEM((D_FF, D_MODEL), BF16),
            pltpu.VMEM((rows.tile, D_MODEL), BF16),
            pltpu.VMEM((rows.tile, D_MODEL), F32),
            pltpu.VMEM((rows.tile, D_FF), BF16),
        ],
        compiler_params=_cparams(("arbitrary",)),
        name="ffn",
    )(*args)


def _rope(x, cos, sin):
    lane = lax.broadcasted_iota(jnp.int32, (x.shape[0], LANES), 1)
    first_half = (lane % 32) < 16
    out = []
    for c in range(x.shape[1] // LANES):
        xc = x[:, c * LANES:(c + 1) * LANES]
        partner = jnp.where(first_half, pltpu.roll(xc, LANES - 16, 1), pltpu.roll(xc, 16, 1))
        out.append(xc * cos + partner * sin)
    return jnp.concatenate(out, axis=1)


def _inproj_kernel(*refs, tiles_p, n_alias):
    x_ref, pat_ref, g_ref, w_ref, cos_ref, sin_ref = refs[:6]
    up_ref, us_ref, q_ref, ks_ref, vs_ref, kc_ref, vc_ref, z_ref, w_res, n_scr = refs[6 + n_alias:]
    s = pl.program_id(0)
    latent = jnp.maximum(s - (N_PROJ - 1), 0) >= tiles_p
    scale = DH ** -0.5 * math.log2(math.e)

    def normed():
        return _norm_mod(x_ref[...], g_ref[...], pat_ref[3:4], pat_ref[4:5]).astype(BF16)

    def store_prompt(c, y):
        if c == 0:
            up_ref[...] = y
        elif c == 1:
            q_ref[...] = (y * scale).astype(BF16)
        elif c == 2:
            kc_ref[...] = y.reshape(kc_ref.shape)
        elif c == 3:
            vc_ref[...] = y.reshape(vc_ref.shape)
        else:
            z_ref[...] = y

    @pl.when(s < N_PROJ)
    def _():
        @pl.when(s == 0)
        def _():
            n_scr[...] = normed()

        w = w_ref[...].astype(BF16)
        w_res[s] = w
        y = _dot(n_scr[...], w)
        for c in range(N_PROJ):
            @pl.when(s == c)
            def _(c=c):
                store_prompt(c, y)

    @pl.when(s >= N_PROJ)
    def _():
        n = normed()
        col = lambda c: _dot(n, w_res[c])

        @pl.when(latent)
        def _():
            q_ref[...] = (_rope(col(1), cos_ref[...], sin_ref[...]) * scale).astype(BF16)
            ks_ref[...] = _rope(col(2), cos_ref[...], sin_ref[...]).astype(BF16)
            us_ref[...] = col(0)
            vs_ref[...] = col(3).astype(BF16)
            z_ref[...] = col(4)

        @pl.when(jnp.logical_not(latent))
        def _():
            for c in range(N_PROJ):
                store_prompt(c, col(c))


def _inproj(rows, x, pat, norm_g, w_in, layer, rope_tabs, cache_out):
    tiles_p = rows.tiles_p
    seqs_per_tile = rows.tile // rows.seq_p
    tabs_per_seq = rows.seq_s // rows.tile
    tile_of = lambda s: jnp.maximum(s - (N_PROJ - 1), 0)
    tab_spec = pl.BlockSpec((rows.tile,LANES), lambda s: (jnp.maximum(tile_of(s) - tiles_p, 0) % tabs_per_seq, 0))
    tile = (rows.tile,D_BRANCH)
    p_spec = pl.BlockSpec(tile, lambda s: (jnp.minimum(tile_of(s), tiles_p - 1), 0))
    s_spec = pl.BlockSpec(tile, lambda s: (jnp.maximum(tile_of(s) - tiles_p, 0), 0))
    all_spec = pl.BlockSpec(tile, lambda s: (tile_of(s), 0))
    cache_spec = pl.BlockSpec((seqs_per_tile, None, rows.seq_p, D_BRANCH),
                              lambda s: (jnp.minimum(tile_of(s), tiles_p - 1), layer, 0, 0))
    cache_shape = jax.ShapeDtypeStruct((rows.n_p, DEPTH, rows.seq_p, D_BRANCH), F32)
    in_specs = [
        pl.BlockSpec((rows.tile,D_MODEL), lambda s: (tile_of(s), 0)),
        pl.BlockSpec((None, None, N_MOD, D_MODEL), lambda s: (layer, rows.mod_row(tile_of(s)), 0, 0)),
        pl.BlockSpec((None, None, 1, D_MODEL), lambda s: (layer, 1, 0, 0)),
        pl.BlockSpec((None, D_MODEL, D_BRANCH), lambda s: (layer, 0, jnp.minimum(s, N_PROJ - 1))),
        tab_spec, tab_spec,
    ]
    args = [x, pat, norm_g, w_in, *rope_tabs]
    aliases = {}
    if cache_out is not None:
        in_specs += [pl.BlockSpec(memory_space=pl.ANY)] * 2
        aliases = {len(args): 5, len(args) + 1: 6}
        args += list(cache_out)
    return pl.pallas_call(
        functools.partial(_inproj_kernel, tiles_p=tiles_p, n_alias=len(aliases)),
        grid=(N_PROJ - 1 + rows.n_tok // rows.tile,),
        in_specs=in_specs,
        out_specs=[p_spec, s_spec, all_spec, s_spec, s_spec, cache_spec, cache_spec, all_spec],
        out_shape=[
            jax.ShapeDtypeStruct((rows.tok_p, D_BRANCH), F32),
            jax.ShapeDtypeStruct((rows.tok_s, D_BRANCH), F32),
            jax.ShapeDtypeStruct((rows.n_tok, D_BRANCH), BF16),
            jax.ShapeDtypeStruct((rows.tok_s, D_BRANCH), BF16),
            jax.ShapeDtypeStruct((rows.tok_s, D_BRANCH), BF16),
            cache_shape, cache_shape,
            jax.ShapeDtypeStruct((rows.n_tok, D_BRANCH), F32),
        ],
        scratch_shapes=[pltpu.VMEM((N_PROJ, D_MODEL, D_BRANCH), BF16), pltpu.VMEM((rows.tile,D_MODEL), BF16)],
        input_output_aliases=aliases,
        compiler_params=_cparams(("arbitrary",)),
        name="inproj",
    )(*args)


def _s5_kernel(*refs, n_seq, n_steps, backward):
    if backward:
        u_ref, part_ref, w1_ref, w2_ref, w3_ref, ar_ref, ai_ref, h0_ref, wglu_ref, ya_ref, hfin_ref, xs_scr, h_scr = refs
    else:
        u_ref, dsk_ref, w1_ref, w2_ref, w3_ref, ar_ref, ai_ref, h0_ref, part_ref, hfin_ref, xs_scr, h_scr = refs
    rows = n_seq * n_steps
    pairs = n_steps // 2
    half = pairs * n_seq

    @pl.when(pl.program_id(0) == 0)
    def _():
        h_scr[...] = h0_ref[...]

    u = jnp.swapaxes(u_ref[...], 0, 1).reshape(rows, D_BRANCH)
    u4 = u.reshape(pairs, 2, n_seq, D_BRANCH)
    u_even = u4[:, 0].reshape(half, D_BRANCH).astype(BF16)
    u_odd = u4[:, 1].reshape(half, D_BRANCH).astype(BF16)
    u_first = u_odd if backward else u_even
    base = 0 if backward else n_seq
    incoming = half if backward else 0
    order = range(pairs - 1, -1, -1) if backward else range(pairs)

    def project(j):
        cols = slice(j * LANES, (j + 1) * LANES)
        lhs = jnp.concatenate([u_even[:, cols], u_odd[:, cols]], axis=1)
        xs_scr[j, base:base + half, :] = _dot(lhs, w1_ref[j])

    def scan(j):
        lanes = slice(j * SLAB_STATE, (j + 1) * SLAB_STATE)
        ar = jnp.broadcast_to(ar_ref[:, lanes], (n_seq, SLAB_STATE))
        ai = jnp.broadcast_to(ai_ref[:, lanes], (n_seq, SLAB_STATE))
        hr = h_scr[0, :, lanes]
        hi = h_scr[1, :, lanes]
        xs_scr[j, incoming:incoming + n_seq, 0:SLAB_STATE] = hr
        xs_scr[j, incoming:incoming + n_seq, SLAB_STATE:2 * SLAB_STATE] = hi
        for q in order:
            r = slice(base + q * n_seq, base + (q + 1) * n_seq)
            xr = xs_scr[j, r, 0:SLAB_STATE]
            xi = xs_scr[j, r, SLAB_STATE:2 * SLAB_STATE]
            hr, hi = ar * hr - ai * hi + xr, ar * hi + ai * hr + xi
            xs_scr[j, r, 0:SLAB_STATE] = hr
            xs_scr[j, r, SLAB_STATE:2 * SLAB_STATE] = hi
        h_scr[0, :, lanes] = hr
        h_scr[1, :, lanes] = hi

    def read_out(j):
        z = _dot(xs_scr[j].astype(BF16), w2_ref[j])
        direct = _dot(u_first[:, j * LANES:(j + 1) * LANES], w3_ref[j])
        lo = z[0:half]
        hi = z[n_seq:n_seq + half]
        if backward:
            even, odd = lo[:, 0:LANES], hi[:, LANES:2 * LANES] + direct
        else:
            even, odd = lo[:, LANES:2 * LANES] + direct, hi[:, 0:LANES]
        both = jnp.stack([even.reshape(pairs, n_seq, LANES), odd.reshape(pairs, n_seq, LANES)], axis=1)
        return both.reshape(rows, LANES)

    ys = []
    project(0)
    for j in range(N_SLAB):
        if j + 1 < N_SLAB:
            project(j + 1)
        scan(j)
        ys.append(read_out(j))
    y = jnp.concatenate(ys, axis=1)

    if backward:
        g = jax.nn.gelu(part_ref[...] + y)
        ya = g * jax.nn.sigmoid(_dot(g.astype(BF16), wglu_ref[...]))
        ya_ref[...] = jnp.swapaxes(ya.reshape(n_steps, n_seq, D_BRANCH), 0, 1).astype(BF16)
    else:
        part_ref[...] = y + dsk_ref[...] * u

    @pl.when(pl.program_id(0) == pl.num_programs(0) - 1)
    def _():
        hfin_ref[...] = h_scr[...]


def _s5(u, ssm, h0, h0_row, n_seq, layer):
    n_tok = u.shape[0]
    seq = n_tok // n_seq
    n_steps = S5_CHUNK_ROWS // n_seq
    n_chunks = seq // n_steps
    u3 = u.reshape(n_seq, seq, D_BRANCH)

    def specs(backward):
        d = int(backward)
        chunk = (lambda s: n_chunks - 1 - s) if backward else (lambda s: s)
        seq_block = pl.BlockSpec((n_seq, n_steps, D_BRANCH), lambda s: (0, chunk(s), 0))
        part_block = pl.BlockSpec((S5_CHUNK_ROWS, D_BRANCH), lambda s: (chunk(s), 0))
        params = [
            pl.BlockSpec((None, None, N_SLAB, 2 * LANES, 2 * SLAB_STATE), lambda s: (layer, d, 0, 0, 0)),
            pl.BlockSpec((None, None, N_SLAB, 2 * SLAB_STATE, 2 * LANES), lambda s: (layer, d, 0, 0, 0)),
            pl.BlockSpec((None, None, N_SLAB, LANES, LANES), lambda s: (layer, d, 0, 0, 0)),
            pl.BlockSpec((None, None, 1, N_STATE), lambda s: (layer, d, 0, 0)),
            pl.BlockSpec((None, None, 1, N_STATE), lambda s: (layer, d, 0, 0)),
            pl.BlockSpec((None, None, 2, n_seq, N_STATE), lambda s: (h0_row, d, 0, 0, 0)),
        ]
        return seq_block, part_block, params

    param_args = (ssm["w1"], ssm["w2"], ssm["w3"], ssm["a2_re"], ssm["a2_im"], h0)
    fin_spec = pl.BlockSpec((2, n_seq, N_STATE), lambda s: (0, 0, 0))
    fin_shape = jax.ShapeDtypeStruct((2, n_seq, N_STATE), F32)
    state_rows = S5_CHUNK_ROWS // 2 + n_seq
    scratch = [pltpu.VMEM((N_SLAB, state_rows, 2 * SLAB_STATE), F32), pltpu.VMEM((2, n_seq, N_STATE), F32)]

    seq_block, part_block, params = specs(False)
    part, fin_f = pl.pallas_call(
        functools.partial(_s5_kernel, n_seq=n_seq, n_steps=n_steps, backward=False),
        grid=(n_chunks,),
        in_specs=[seq_block, pl.BlockSpec((None, 1, D_BRANCH), lambda s: (layer, 0, 0))] + params,
        out_specs=[part_block, fin_spec],
        out_shape=[jax.ShapeDtypeStruct((n_tok, D_BRANCH), F32), fin_shape],
        scratch_shapes=scratch,
        compiler_params=_cparams(("arbitrary",)),
        name="s5_fwd",
    )(u3, ssm["d_skip"], *param_args)

    seq_block, part_block, params = specs(True)
    ya, fin_b = pl.pallas_call(
        functools.partial(_s5_kernel, n_seq=n_seq, n_steps=n_steps, backward=True),
        grid=(n_chunks,),
        in_specs=[seq_block, part_block] + params + [pl.BlockSpec((None, D_BRANCH, D_BRANCH), lambda s: (layer, 0, 0))],
        out_specs=[seq_block, fin_spec],
        out_shape=[jax.ShapeDtypeStruct((n_seq, seq, D_BRANCH), BF16), fin_shape],
        scratch_shapes=scratch,
        compiler_params=_cparams(("arbitrary",)),
        name="s5_bwd",
    )(u3, part, *param_args, ssm["w_glu"])
    return ya.reshape(n_tok, D_BRANCH), jnp.stack([fin_f, fin_b], axis=0)


def _s5_params(lam_re, lam_im, log_dt, b_re, b_im, c_re, c_im):
    dt = jnp.exp(log_dt)[..., None]
    mag = jnp.exp(lam_re * dt)
    abr = mag * jnp.cos(lam_im * dt)
    abi = mag * jnp.sin(lam_im * dt)
    den = lam_re * lam_re + lam_im * lam_im
    nr = abr - 1.0
    kr = (nr * lam_re + abi * lam_im) / den
    ki = (abi * lam_re - nr * lam_im) / den
    bbr = kr[..., None] * b_re - ki[..., None] * b_im
    bbi = kr[..., None] * b_im + ki[..., None] * b_re
    per_slab = S5_GROUPS // N_SLAB

    def block_diag(w, n_in, n_out):
        t = w.reshape(2, N_SLAB, per_slab, n_out, n_in).transpose(0, 1, 2, 4, 3)
        t = t.reshape(2, N_SLAB, per_slab * n_in, n_out)
        repeat = (lax.broadcasted_iota(jnp.int32, (n_out, per_slab * n_out), 0)
                  == lax.broadcasted_iota(jnp.int32, (n_out, per_slab * n_out), 1) % n_out).astype(F32)
        t = jnp.einsum("djrc,cs->djrs", t, repeat, precision=lax.Precision.HIGHEST)
        row_group = lax.broadcasted_iota(jnp.int32, t.shape[-2:], 0) // n_in
        col_group = lax.broadcasted_iota(jnp.int32, t.shape[-2:], 1) // n_out
        return jnp.where(row_group == col_group, t, 0.0)

    def pack_in(w):
        return block_diag(w, S5_GROUP, S5_STATE)

    def pack_out(w):
        return block_diag(w, S5_STATE, S5_GROUP)

    abbr = abr[..., None] * bbr - abi[..., None] * bbi
    abbi = abr[..., None] * bbi + abi[..., None] * bbr
    car = c_re * abr[:, :, None, :] - c_im * abi[:, :, None, :]
    cai = c_re * abi[:, :, None, :] + c_im * abr[:, :, None, :]
    b_pack = jnp.concatenate([pack_in(bbr), pack_in(bbi)], axis=-1)
    ab_pack = jnp.concatenate([pack_in(abbr), pack_in(abbi)], axis=-1)
    w1 = jnp.stack([jnp.concatenate([ab_pack[0], b_pack[0]], axis=-2),
                    jnp.concatenate([b_pack[1], ab_pack[1]], axis=-2)]).astype(BF16)
    c_pack = jnp.concatenate([pack_out(c_re), -pack_out(c_im)], axis=-2)
    ca_pack = jnp.concatenate([pack_out(car), -pack_out(cai)], axis=-2)
    w2 = jnp.concatenate([c_pack, ca_pack], axis=-1).astype(BF16)
    direct = jnp.einsum("dgop,dgpi->dgoi", c_re, bbr) - jnp.einsum("dgop,dgpi->dgoi", c_im, bbi)
    w3 = block_diag(direct, S5_GROUP, S5_GROUP).astype(BF16)
    a2r = abr * abr - abi * abi
    a2i = 2.0 * abr * abi
    return dict(w1=w1, w2=w2, w3=w3, a2_re=a2r.reshape(2, 1, N_STATE), a2_im=a2i.reshape(2, 1, N_STATE))


def _attn_kernel(*refs, n_ctx, lam_init, layer):
    if n_ctx:
        lam_ref, q_ref, k_ref, v_ref, ck_ref, cv_ref, g_ref, o_ref, k_scr, v_scr = refs
    else:
        lam_ref, q_ref, k_ref, v_ref, g_ref, o_ref, k_scr, v_scr = refs
    n_own = k_ref.shape[0]

    @pl.when(pl.program_id(1) == 0)
    def _():
        k_scr[0:n_own, :] = k_ref[...].astype(BF16)
        v_scr[0:n_own, :] = v_ref[...].astype(BF16)
        if n_ctx:
            k_scr[n_own:n_own + n_ctx, :] = ck_ref[...].astype(BF16)
            v_scr[n_own:n_own + n_ctx, :] = cv_ref[...].astype(BF16)

    lam = lam_ref[layer]
    q = q_ref[...]
    lane = lax.broadcasted_iota(jnp.int32, (q.shape[0], LANES), 1)

    def scores(h):
        lanes = slice(h * LANES, (h + 1) * LANES)
        qh = q[:, lanes]
        kh = k_scr[:, lanes]
        out = []
        for m in range(2):
            qm = jnp.where((lane < DH) if m == 0 else (lane >= DH), qh, jnp.zeros_like(qh))
            out.append(lax.dot_general(qm, kh, (((1,), (1,)), ((), ())), preferred_element_type=F32))
        return out

    sc = scores(0)
    for h in range(N_HEADS):
        lanes = slice(h * LANES, (h + 1) * LANES)
        nxt = scores(h + 1) if h + 1 < N_HEADS else None
        e1, e2 = [jnp.exp2(s_m - jnp.max(s_m, axis=-1, keepdims=True)) for s_m in sc]
        l1 = jnp.sum(e1, axis=-1, keepdims=True)
        l2 = jnp.sum(e2, axis=-1, keepdims=True)
        a = (e1 - e2 * (lam * l1 / l2)).astype(BF16)
        o = _dot(a, v_scr[:, lanes]) * (1.0 / l1)
        ms = jnp.mean(o * o, axis=-1, keepdims=True)
        o_ref[:, lanes] = (((o * lax.rsqrt(ms + EPS)) * g_ref[...]) * (1.0 - lam_init)).astype(BF16)
        sc = nxt


def _attention(q, k, v, lam, attn_g, row0, n_seq, seq, lam_init, layer, cache=None):
    tq = min(Q_TILE, seq)
    q_tiles = seq // tq
    q0 = row0 // tq
    n_ctx = 0 if cache is None else cache[0].shape[2]
    if k.ndim == 4:
        own = pl.BlockSpec((None, None, seq, D_BRANCH), lambda b, i: (b, layer, 0, 0))
    else:
        own = pl.BlockSpec((seq, D_BRANCH), lambda b, i: (b, 0))
    in_specs = [
        pl.BlockSpec(memory_space=pltpu.SMEM),
        pl.BlockSpec((tq, D_BRANCH), lambda b, i: (q0 + b * q_tiles + i, 0)),
        own, own,
    ]
    args = [lam, q, k, v]
    if n_ctx:
        in_specs += [pl.BlockSpec((None, None, n_ctx, D_BRANCH), lambda b, i: (b, layer, 0, 0))] * 2
        args += list(cache)
    in_specs.append(pl.BlockSpec((None, 1, DV), lambda b, i: (layer, 0, 0)))
    args.append(attn_g)
    return pl.pallas_call(
        functools.partial(_attn_kernel, n_ctx=n_ctx, lam_init=lam_init, layer=layer),
        grid=(n_seq, q_tiles),
        in_specs=in_specs,
        out_specs=pl.BlockSpec((tq, D_BRANCH), lambda b, i: (b * q_tiles + i, 0)),
        out_shape=jax.ShapeDtypeStruct((n_seq * seq, D_BRANCH), BF16),
        scratch_shapes=[pltpu.VMEM((seq + n_ctx, D_BRANCH), BF16)] * 2,
        compiler_params=_cparams(("parallel", "arbitrary")),
        name="attn",
    )(*args)


def _pool_kernel(z_ref, w_ref, sc_ref, o_ref, pad_scr):
    seq = z_ref.shape[0]
    padded = seq + 2 * POOL_PAD
    z = z_ref[...]
    zeros = jnp.zeros((POOL_PAD, D_BRANCH), F32)
    pad_scr[0:POOL_PAD, :] = zeros
    pad_scr[POOL_PAD:POOL_PAD + seq, :] = z
    pad_scr[POOL_PAD + seq:padded, :] = zeros
    t = lax.broadcasted_iota(jnp.int32, (seq, POOL_GROUP), 0)
    outs = []
    for gi, w in enumerate(POOL_WINDOWS):
        lanes = slice(gi * POOL_GROUP, (gi + 1) * POOL_GROUP)
        run = pad_scr[:, lanes]
        span = 1
        while span < w:
            run = run + pltpu.roll(run, padded - span, 0)
            span *= 2
        tot = pltpu.roll(run, w // 2, 0)[POOL_PAD:POOL_PAD + seq]
        cnt = jnp.minimum(t + w // 2, seq) - jnp.maximum(t - w // 2, 0)
        pooled = tot / cnt.astype(F32) - z[:, lanes]
        outs.append(_dot(pooled.astype(BF16), w_ref[gi]))
    o_ref[...] = (jnp.concatenate(outs, axis=1) * sc_ref[...]).astype(BF16)


def _pool(z, w_pool, pool_scale, row0, n_seq, seq, layer):
    s0 = row0 // seq
    return pl.pallas_call(
        _pool_kernel,
        grid=(n_seq,),
        in_specs=[
            pl.BlockSpec((seq, D_BRANCH), lambda b: (s0 + b, 0)),
            pl.BlockSpec((None, len(POOL_WINDOWS), POOL_GROUP, POOL_GROUP), lambda b: (layer, 0, 0, 0)),
            pl.BlockSpec((None, 1, D_BRANCH), lambda b: (layer, 0, 0)),
        ],
        out_specs=pl.BlockSpec((seq, D_BRANCH), lambda b: (b, 0)),
        out_shape=jax.ShapeDtypeStruct((n_seq * seq, D_BRANCH), BF16),
        scratch_shapes=[pltpu.VMEM((seq + 2 * POOL_PAD, D_BRANCH), F32)],
        compiler_params=_cparams(("parallel",)),
        name="pool",
    )(z, w_pool, pool_scale)


def _merge_kernel(x_ref, pat_ref, g_ref, *refs, tiles_p):
    branch_refs = refs[:2 * N_BRANCH]
    wg0_ref, wg1_ref, wbr_ref, wo_ref, o_ref, wg_res, wbr_res, wo_res, n_scr, m_scr = refs[2 * N_BRANCH:]
    s = pl.program_id(0)
    prompt = jnp.maximum(s - (N_BRANCH - 1), 0) < tiles_p
    x = x_ref[...]

    def normed():
        return _norm_mod(x, g_ref[...], pat_ref[3:4], pat_ref[4:5]).astype(BF16)

    def branch_in(br):
        return jnp.where(prompt, branch_refs[2 * br][...], branch_refs[2 * br + 1][...])

    def finish(merged):
        o_ref[...] = x + pat_ref[5:6] * _dot(merged.astype(BF16), wo_res[...])

    @pl.when(s < N_BRANCH)
    def _():
        @pl.when(s == 0)
        def _():
            n_scr[...] = normed()
            m_scr[...] = jnp.zeros_like(m_scr)
            wo_res[...] = wo_ref[...].astype(BF16)

        wg = jnp.concatenate([wg0_ref[...], wg1_ref[...]], axis=1).astype(BF16)
        wbr = wbr_ref[...].astype(BF16)
        wg_res[s] = wg
        wbr_res[s] = wbr
        gate = jax.nn.sigmoid(_dot(n_scr[...], wg))
        for br in range(N_BRANCH):
            @pl.when(s == br)
            def _(br=br):
                m_scr[...] += gate * _dot(branch_in(br), wbr)

        @pl.when(s == N_BRANCH - 1)
        def _():
            finish(m_scr[...])

    @pl.when(s >= N_BRANCH)
    def _():
        n = normed()
        merged = None
        for br in range(N_BRANCH):
            part = jax.nn.sigmoid(_dot(n, wg_res[br])) * _dot(branch_in(br), wbr_res[br])
            merged = part if merged is None else merged + part
        finish(merged)


def _merge(rows, x, pat, norm_g, branches, w_in, w_branch, w_out, layer):
    tiles_p = rows.tiles_p
    half = D_MODEL // 2
    gate0 = N_PROJ * D_BRANCH // half
    tile_of = lambda s: jnp.maximum(s - (N_BRANCH - 1), 0)
    br_of = lambda s: jnp.minimum(s, N_BRANCH - 1)
    p_spec = pl.BlockSpec((rows.tile,D_BRANCH), lambda s: (jnp.minimum(tile_of(s), tiles_p - 1), 0))
    s_spec = pl.BlockSpec((rows.tile,D_BRANCH), lambda s: (jnp.maximum(tile_of(s) - tiles_p, 0), 0))
    return pl.pallas_call(
        functools.partial(_merge_kernel, tiles_p=tiles_p),
        grid=(N_BRANCH - 1 + rows.n_tok // rows.tile,),
        in_specs=[
            pl.BlockSpec((rows.tile,D_MODEL), lambda s: (tile_of(s), 0)),
            pl.BlockSpec((None, None, N_MOD, D_MODEL), lambda s: (layer, rows.mod_row(tile_of(s)), 0, 0)),
            pl.BlockSpec((None, None, 1, D_MODEL), lambda s: (layer, 1, 0, 0)),
            p_spec, s_spec, p_spec, s_spec, p_spec, s_spec,
            pl.BlockSpec((None, D_MODEL, half), lambda s: (layer, 0, gate0 + 2 * br_of(s))),
            pl.BlockSpec((None, D_MODEL, half), lambda s: (layer, 0, gate0 + 2 * br_of(s) + 1)),
            pl.BlockSpec((None, None, D_BRANCH, D_MODEL), lambda s: (layer, br_of(s), 0, 0)),
            _resident((None, D_MODEL, D_MODEL), lambda s: (layer, 0, 0)),
        ],
        out_specs=pl.BlockSpec((rows.tile,D_MODEL), lambda s: (tile_of(s), 0)),
        out_shape=jax.ShapeDtypeStruct((rows.n_tok, D_MODEL), F32),
        scratch_shapes=[
            pltpu.VMEM((N_BRANCH, D_MODEL, D_MODEL), BF16),
            pltpu.VMEM((N_BRANCH, D_BRANCH, D_MODEL), BF16),
            pltpu.VMEM((D_MODEL, D_MODEL), BF16),
            pltpu.VMEM((rows.tile,D_MODEL), BF16),
            pltpu.VMEM((rows.tile,D_MODEL), F32),
        ],
        compiler_params=_cparams(("arbitrary",)),
        name="merge",
    )(x, pat, norm_g, *[y for pair in branches for y in pair], w_in, w_in, w_branch, w_out)


def _rope_tables(seq):
    n_rows = seq // GRID_W
    row = jnp.repeat(jnp.arange(n_rows, dtype=F32), GRID_W)
    col = jnp.tile(jnp.arange(GRID_W, dtype=F32), n_rows)
    n_freq = DH // 4
    inv = ROPE_BASE ** (-jnp.arange(n_freq, dtype=F32) / n_freq)
    ar = row[:, None] * inv
    ac = col[:, None] * inv
    cos = jnp.concatenate([jnp.cos(ar)] * 2 + [jnp.cos(ac)] * 2, axis=1)
    sin = jnp.concatenate([-jnp.sin(ar), jnp.sin(ar), -jnp.sin(ac), jnp.sin(ac)], axis=1)
    return jnp.tile(cos, (1, LANES // DH)), jnp.tile(sin, (1, LANES // DH))


def kernel(x_prompt, x_sample, cache_k, cache_v, state_ssm, c, c_ctx, norm_g, w_mod, b_mod, w_ffn_in, w_ffn_out, w_in, ssm_lam_re, ssm_lam_im, ssm_log_dt, ssm_b_re, ssm_b_im, ssm_c_re, ssm_c_im, ssm_d, w_glu, lam_q1, lam_k1, lam_q2, lam_k2, attn_norm_g, w_pool, pool_scale, w_branch, w_out, final_norm_g):
    n_p, seq_p, _ = x_prompt.shape
    n_s, seq_s, _ = x_sample.shape
    n_past = cache_k.shape[2]
    rows = _Rows(n_p, seq_p, n_s, seq_s, ROW_TILE)
    merge_rows = _Rows(n_p, seq_p, n_s, seq_s, MERGE_ROW_TILE)
    tok_p = rows.tok_p

    mod_rows = -(-(1 + n_s) // SUBLANES) * SUBLANES
    cvec = jnp.concatenate([c_ctx[None, :], c, jnp.zeros((mod_rows - 1 - n_s, D_MODEL), F32)], axis=0)
    pat = _adaln(cvec, w_mod, b_mod).reshape(DEPTH, mod_rows, N_MOD, D_MODEL)

    w_pl = w_pool.astype(BF16)

    x = (x_prompt.reshape(tok_p, D_MODEL), x_sample.reshape(rows.tok_s, D_MODEL))
    rope_tabs = _rope_tables(seq_s)
    cache = (cache_k.reshape(n_s, DEPTH, n_past, D_BRANCH), cache_v.reshape(n_s, DEPTH, n_past, D_BRANCH))
    h0_p = jnp.zeros((1, 2, 2, n_p, N_STATE), F32)
    h0_s = state_ssm.reshape(n_s, DEPTH, 2, 2, N_STATE).transpose(1, 2, 3, 0, 4)

    ssm = jax.vmap(_s5_params)(ssm_lam_re, ssm_lam_im, ssm_log_dt, ssm_b_re, ssm_b_im, ssm_c_re, ssm_c_im)
    ssm.update(d_skip=ssm_d[:, None, :], w_glu=w_glu.astype(BF16))

    lam_inits = [0.8 - 0.6 * math.exp(-0.3 * l) for l in range(DEPTH)]
    lam = (jnp.exp(jnp.sum(lam_q1 * lam_k1, axis=-1)) - jnp.exp(jnp.sum(lam_q2 * lam_k2, axis=-1))
           + jnp.asarray(lam_inits, F32))
    ng = norm_g.reshape(DEPTH, 3, 1, D_MODEL)
    attn_g = attn_norm_g.reshape(DEPTH, 1, DV)
    scale_c = pool_scale.reshape(DEPTH, 1, D_BRANCH)

    new_cache, new_s = None, []
    for l in range(DEPTH):
        x = _ffn(merge_rows if l == 0 else rows, x, pat, ng, w_ffn_in, w_ffn_out, l, 0, 0)
        u_p, u_s, q, k_s, v_s, k_new, v_new, z = _inproj(rows, x, pat, ng, w_in, l, rope_tabs, new_cache)
        new_cache = (k_new, v_new)

        ya_p, h_fin = _s5(u_p, ssm, h0_p, 0, n_p, l)
        ya_s, _ = _s5(u_s, ssm, h0_s, l, n_s, l)
        yb = (_attention(q, k_new, v_new, lam, attn_g, 0, n_p, seq_p, lam_inits[l], l),
              _attention(q, k_s, v_s, lam, attn_g, tok_p, n_s, seq_s, lam_inits[l], l, cache))
        yc = (_pool(z, w_pl, scale_c, 0, n_p, seq_p, l), _pool(z, w_pl, scale_c, tok_p, n_s, seq_s, l))

        x = _merge(merge_rows, x, pat, ng, ((ya_p, ya_s), yb, yc), w_in, w_branch, w_out, l)
        last = l == DEPTH - 1
        x = _ffn(merge_rows if last else rows, x, pat, ng, w_ffn_in, w_ffn_out, l, 1, 6,
                 final_norm_g[None, :] if last else None)
        new_s.append(h_fin)

    y_p, y_s = x
    new_state = jnp.stack(new_s, axis=0).transpose(3, 0, 1, 2, 4)
    return (y_p.reshape(n_p, seq_p, D_MODEL), y_s.reshape(n_s, seq_s, D_MODEL),
            new_cache[0].reshape(n_p, DEPTH, seq_p, N_HEADS, 2, DH), new_cache[1].reshape(n_p, DEPTH, seq_p, N_HEADS, DV),
            new_state.reshape(n_p, DEPTH, 2, 2, S5_GROUPS, S5_STATE))
```
